```python
import jax, jax.numpy as jnp
from jax import lax
import numpy as np

D_MODEL = 1024
BATCH = 16
SEQ = 256
DEPTH = 1
DEC_BATCH = 2
DEC_SEQ = 1024
PAST_LEN = 512

GRID_W = 64
N_HEADS_A = 8
HEAD_DIM_A = 64
D_ATT = N_HEADS_A * HEAD_DIM_A
KH_MAX = 8
KW = 16
Q_BLOCK = 128
D_RNN = 512
N_RG_BLOCKS = 8
RG_BLOCK = D_RNN // N_RG_BLOCKS
CONV_W = 4
CONV_PAD = (2, 1)
RG_C = 8.0
N_EXPERTS = 64
TOP_K = 8
N_GROUPS = 8
TOPK_GROUP = 4
D_EXPERT = 256
D_SHARED = 256
ROUTED_SCALE = 2.5
N_MOD = 6
EPS = 1e-6
NEG = -1e30
D_IN = 3 * D_ATT + 2 * D_RNN + 2 * D_MODEL
SPLITS = (D_ATT, 2 * D_ATT, 3 * D_ATT, 3 * D_ATT + D_RNN, 3 * D_ATT + 2 * D_RNN, 3 * D_ATT + 2 * D_RNN + D_MODEL)

kernel_name = "hybrid_natten_rglru_moe_diffusion_step"


def rmsnorm(x, g):
    xf = x.astype(jnp.float32)
    y = xf * lax.rsqrt(jnp.mean(xf * xf, axis=-1, keepdims=True) + EPS)
    return (y * g.astype(jnp.float32)).astype(x.dtype)


def adaln(cvec, w, b):
    m = jax.nn.silu(cvec) @ w + b
    return m.reshape(cvec.shape[0], N_MOD, D_MODEL)


def modulate(h, m, i):
    return h * (1.0 + m[:, i + 1][:, None, :]) + m[:, i][:, None, :]


def dwconv(x, w, b):
    y = lax.conv_general_dilated(x, w[:, None, :], window_strides=(1,), padding=[CONV_PAD],
                                 dimension_numbers=('NWC', 'WIO', 'NWC'), feature_group_count=D_RNN)
    return y + b


def _lin_combine(e1, e2):
    a1, b1 = e1
    a2, b2 = e2
    return a1 * a2, a2 * b1 + b2


def rglru(xc, wa, ba, wx, bx, lam, h0, reverse):
    B, T, _ = xc.shape
    xb = xc.reshape(B, T, N_RG_BLOCKS, RG_BLOCK)
    r = jax.nn.sigmoid((jnp.einsum('btnd,nde->btne', xb, wa).reshape(B, T, D_RNN) + ba).astype(jnp.float32))
    i = jax.nn.sigmoid((jnp.einsum('btnd,nde->btne', xb, wx).reshape(B, T, D_RNN) + bx).astype(jnp.float32))
    log_a = -RG_C * r * jax.nn.softplus(-lam.astype(jnp.float32))
    a = jnp.exp(log_a)
    u = jnp.sqrt(jnp.maximum(1.0 - jnp.exp(2.0 * log_a), 0.0)) * (i * xc.astype(jnp.float32))
    if reverse:
        a, u = jnp.flip(a, 1), jnp.flip(u, 1)
    a_cum, u_cum = lax.associative_scan(_lin_combine, (a, u), axis=1)
    h = a_cum * h0.astype(jnp.float32)[:, None, :] + u_cum
    h_last = h[:, -1]
    if reverse:
        h = jnp.flip(h, 1)
    return h.astype(xc.dtype), h_last.astype(xc.dtype)


def rg_branch(xr, yg, conv_w, conv_b, rg_wa, rg_ba, rg_wx, rg_bx, rg_lam, h0_f, h0_b):
    xc = dwconv(xr, conv_w, conv_b)
    hf, hf_last = rglru(xc, rg_wa[0], rg_ba[0], rg_wx[0], rg_bx[0], rg_lam[0], h0_f, False)
    hb, hb_last = rglru(xc, rg_wa[1], rg_ba[1], rg_wx[1], rg_bx[1], rg_lam[1], h0_b, True)
    return (hf + hb) * jax.nn.gelu(yg), hf_last, hb_last


def merge(o_a, o_b, ga, gb, w_pa, w_pb, w_out):
    return (jax.nn.sigmoid(ga) * (o_a @ w_pa) + jax.nn.sigmoid(gb) * (o_b @ w_pb)) @ w_out


def ctx_attention(q, k, v):
    B, S, H, dh = q.shape
    nb = S // Q_BLOCK
    qb = q.reshape(B, nb, Q_BLOCK, H, dh).transpose(1, 0, 2, 3, 4)
    scale = HEAD_DIM_A ** -0.5

    def block(qi):
        s = jnp.einsum('bqhd,bshd->bhqs', qi, k).astype(jnp.float32) * scale
        p = jax.nn.softmax(s, axis=-1).astype(v.dtype)
        return jnp.einsum('bhqs,bshd->bqhd', p, v)

    o = lax.map(block, qb)
    return o.transpose(1, 0, 2, 3, 4).reshape(B, S, H * dh)


def na_attention(q, k, v, k_ctx, v_ctx, rpb):
    B, T, H, dh = q.shape
    rows = T // GRID_W
    kh = min(KH_MAX, rows)
    qg = q.reshape(B, rows, GRID_W, H, dh)
    kg = k.reshape(B, rows, GRID_W, H, dh)
    vg = v.reshape(B, rows, GRID_W, H, dh)
    r = jnp.arange(rows)
    row_idx = jnp.clip(r - kh // 2, 0, rows - kh)[:, None] + jnp.arange(kh)[None, :]
    k_rows = kg[:, row_idx]
    v_rows = vg[:, row_idx]
    w = jnp.arange(GRID_W)
    col_start = jnp.clip(w - KW // 2, 0, GRID_W - KW)
    col_in = (w[None, :] >= col_start[:, None]) & (w[None, :] < col_start[:, None] + KW)
    drow = row_idx - r[:, None] + (KH_MAX - 1)
    dcol = jnp.clip(w[None, :] - w[:, None], -(KW - 1), KW - 1) + (KW - 1)
    bias = rpb[:, drow[:, None, :, None], dcol[None, :, None, :]]
    scale = HEAD_DIM_A ** -0.5
    s_loc = jnp.einsum('brwhd,brjkhd->bhrwjk', qg, k_rows).astype(jnp.float32) * scale + bias[None].astype(jnp.float32)
    s_loc = jnp.where(col_in[:, None, :], s_loc, NEG)
    s_ctx = jnp.einsum('brwhd,bshd->bhrws', qg, k_ctx).astype(jnp.float32) * scale
    n_loc = kh * GRID_W
    s = jnp.concatenate([s_loc.reshape(B, H, rows, GRID_W, n_loc), s_ctx], axis=-1)
    p = jax.nn.softmax(s, axis=-1).astype(v.dtype)
    p_loc = p[..., :n_loc].reshape(B, H, rows, GRID_W, kh, GRID_W)
    p_ctx = p[..., n_loc:]
    o = jnp.einsum('bhrwjk,brjkhd->brwhd', p_loc, v_rows) + jnp.einsum('bhrws,bshd->brwhd', p_ctx, v_ctx)
    return o.reshape(B, T, H * dh)


def mixer_context(h, w_in, conv_w, conv_b, rg_wa, rg_ba, rg_wx, rg_bx, rg_lam, w_pa, w_pb, w_out):
    B, S, _ = h.shape
    q, k, v, xr, yg, ga, gb = jnp.split(h @ w_in, SPLITS, axis=-1)
    q = q.reshape(B, S, N_HEADS_A, HEAD_DIM_A)
    k = k.reshape(B, S, N_HEADS_A, HEAD_DIM_A)
    v = v.reshape(B, S, N_HEADS_A, HEAD_DIM_A)
    o_a = ctx_attention(q, k, v)
    h0 = jnp.zeros((B, D_RNN), h.dtype)
    o_b, hf, hb = rg_branch(xr, yg, conv_w, conv_b, rg_wa, rg_ba, rg_wx, rg_bx, rg_lam, h0, h0)
    return merge(o_a, o_b, ga, gb, w_pa, w_pb, w_out), k, v, hf, hb


def mixer_latent(h, k_ctx, v_ctx, h0_f, h0_b, rpb, w_in, conv_w, conv_b, rg_wa, rg_ba, rg_wx, rg_bx, rg_lam, w_pa, w_pb, w_out):
    B, T, _ = h.shape
    q, k, v, xr, yg, ga, gb = jnp.split(h @ w_in, SPLITS, axis=-1)
    q = q.reshape(B, T, N_HEADS_A, HEAD_DIM_A)
    k = k.reshape(B, T, N_HEADS_A, HEAD_DIM_A)
    v = v.reshape(B, T, N_HEADS_A, HEAD_DIM_A)
    o_a = na_attention(q, k, v, k_ctx, v_ctx, rpb)
    o_b, _, _ = rg_branch(xr, yg, conv_w, conv_b, rg_wa, rg_ba, rg_wx, rg_bx, rg_lam, h0_f, h0_b)
    return merge(o_a, o_b, ga, gb, w_pa, w_pb, w_out)


def moe(h, w_router, b_router, w_gate_e, w_up_e, w_down_e, w_gate_s, w_up_s, w_down_s):
    B, T, D = h.shape
    t = h.reshape(B * T, D)
    n = t.shape[0]
    s = jax.nn.sigmoid((t @ w_router).astype(jnp.float32))
    sb = s + b_router.astype(jnp.float32)
    grp = sb.reshape(n, N_GROUPS, N_EXPERTS // N_GROUPS)
    gscore = lax.top_k(grp, 2)[0].sum(-1)
    _, gidx = lax.top_k(gscore, TOPK_GROUP)
    gmask = jax.nn.one_hot(gidx, N_GROUPS, dtype=jnp.float32).sum(1) > 0
    emask = jnp.repeat(gmask, N_EXPERTS // N_GROUPS, axis=1)
    _, idx = lax.top_k(jnp.where(emask, sb, NEG), TOP_K)
    wk = jnp.take_along_axis(s, idx, axis=-1)
    wk = wk / jnp.sum(wk, axis=-1, keepdims=True) * ROUTED_SCALE
    gates = jnp.einsum('nk,nke->ne', wk, jax.nn.one_hot(idx, N_EXPERTS, dtype=jnp.float32)).astype(t.dtype)
    act = jax.nn.silu(jnp.einsum('nd,edf->nef', t, w_gate_e)) * jnp.einsum('nd,edf->nef', t, w_up_e) * gates[:, :, None]
    routed = jnp.einsum('nef,efd->nd', act, w_down_e)
    shared = (jax.nn.silu(t @ w_gate_s) * (t @ w_up_s)) @ w_down_s
    return (routed + shared).reshape(B, T, D)


def setup_inputs(seed: int = 0) -> dict:
    key = jax.random.key(seed)
    ks = jax.random.split(key, 40)

    def nrm(k, shape, scale):
        return jax.random.normal(k, shape, jnp.float32) * scale

    u = jax.random.uniform(ks[20], (DEPTH, 2, D_RNN), jnp.float32, minval=0.9, maxval=0.999)
    a0 = u ** (1.0 / RG_C)
    rg_lam = jnp.log(a0) - jnp.log1p(-a0)
    return {
        "x_prompt": nrm(ks[0], (BATCH, SEQ, D_MODEL), 1.0),
        "x_sample": nrm(ks[1], (DEC_BATCH, DEC_SEQ, D_MODEL), 1.0),
        "cache_k": nrm(ks[2], (DEC_BATCH, DEPTH, PAST_LEN, N_HEADS_A, HEAD_DIM_A), 1.0),
        "cache_v": nrm(ks[3], (DEC_BATCH, DEPTH, PAST_LEN, N_HEADS_A, HEAD_DIM_A), 1.0),
        "state_h_fwd": nrm(ks[4], (DEC_BATCH, DEPTH, D_RNN), 0.5),
        "state_h_bwd": nrm(ks[5], (DEC_BATCH, DEPTH, D_RNN), 0.5),
        "c": nrm(ks[6], (DEC_BATCH, D_MODEL), 1.0),
        "c_ctx": nrm(ks[7], (D_MODEL,), 1.0),
        "norm1_g": 1.0 + nrm(ks[8], (DEPTH, D_MODEL), 0.05),
        "norm2_g": 1.0 + nrm(ks[9], (DEPTH, D_MODEL), 0.05),
        "final_g": 1.0 + nrm(ks[10], (D_MODEL,), 0.05),
        "w_ada": nrm(ks[11], (DEPTH, D_MODEL, N_MOD * D_MODEL), 0.5 * D_MODEL ** -0.5),
        "b_ada": nrm(ks[12], (DEPTH, N_MOD * D_MODEL), 0.02),
        "w_in": nrm(ks[13], (DEPTH, D_MODEL, D_IN), D_MODEL ** -0.5),
        "rpb": nrm(ks[14], (DEPTH, N_HEADS_A, 2 * KH_MAX - 1, 2 * KW - 1), 0.5),
        "conv_w": nrm(ks[15], (DEPTH, CONV_W, D_RNN), CONV_W ** -0.5),
        "conv_b": nrm(ks[16], (DEPTH, D_RNN), 0.02),
        "rg_wa": nrm(ks[17], (DEPTH, 2, N_RG_BLOCKS, RG_BLOCK, RG_BLOCK), RG_BLOCK ** -0.5),
        "rg_ba": nrm(ks[18], (DEPTH, 2, D_RNN), 0.02),
        "rg_wx": nrm(ks[19], (DEPTH, 2, N_RG_BLOCKS, RG_BLOCK, RG_BLOCK), RG_BLOCK ** -0.5),
        "rg_bx": nrm(ks[21], (DEPTH, 2, D_RNN), 0.02),
        "rg_lam": rg_lam,
        "w_pa": nrm(ks[22], (DEPTH, D_ATT, D_MODEL), D_ATT ** -0.5),
        "w_pb": nrm(ks[23], (DEPTH, D_RNN, D_MODEL), D_RNN ** -0.5),
        "w_out": nrm(ks[24], (DEPTH, D_MODEL, D_MODEL), D_MODEL ** -0.5),
        "w_router": nrm(ks[25], (DEPTH, D_MODEL, N_EXPERTS), D_MODEL ** -0.5),
        "b_router": nrm(ks[26], (DEPTH, N_EXPERTS), 0.01),
        "w_gate_e": nrm(ks[27], (DEPTH, N_EXPERTS, D_MODEL, D_EXPERT), D_MODEL ** -0.5),
        "w_up_e": nrm(ks[28], (DEPTH, N_EXPERTS, D_MODEL, D_EXPERT), D_MODEL ** -0.5),
        "w_down_e": nrm(ks[29], (DEPTH, N_EXPERTS, D_EXPERT, D_MODEL), D_EXPERT ** -0.5),
        "w_gate_s": nrm(ks[30], (DEPTH, D_MODEL, D_SHARED), D_MODEL ** -0.5),
        "w_up_s": nrm(ks[31], (DEPTH, D_MODEL, D_SHARED), D_MODEL ** -0.5),
        "w_down_s": nrm(ks[32], (DEPTH, D_SHARED, D_MODEL), D_SHARED ** -0.5),
    }


def reference(x_prompt, x_sample, cache_k, cache_v, state_h_fwd, state_h_bwd, c, c_ctx,
              norm1_g, norm2_g, final_g, w_ada, b_ada, w_in, rpb, conv_w, conv_b,
              rg_wa, rg_ba, rg_wx, rg_bx, rg_lam, w_pa, w_pb, w_out,
              w_router, b_router, w_gate_e, w_up_e, w_down_e, w_gate_s, w_up_s, w_down_s):
    x_ctx = x_prompt
    x_lat = x_sample
    new_k, new_v, new_hf, new_hb = [], [], [], []
    for l in range(DEPTH):
        mix_w = (w_in[l], conv_w[l], conv_b[l], rg_wa[l], rg_ba[l], rg_wx[l], rg_bx[l], rg_lam[l],
                 w_pa[l], w_pb[l], w_out[l])
        moe_w = (w_router[l], b_router[l], w_gate_e[l], w_up_e[l], w_down_e[l],
                 w_gate_s[l], w_up_s[l], w_down_s[l])
        m = adaln(c_ctx[None, :], w_ada[l], b_ada[l])
        h = modulate(rmsnorm(x_ctx, norm1_g[l]), m, 0)
        o, kc, vc, hf, hb = mixer_context(h, *mix_w)
        x_ctx = x_ctx + m[:, 2][:, None, :] * o
        h = modulate(rmsnorm(x_ctx, norm2_g[l]), m, 3)
        x_ctx = x_ctx + m[:, 5][:, None, :] * moe(h, *moe_w)
        new_k.append(kc)
        new_v.append(vc)
        new_hf.append(hf)
        new_hb.append(hb)
        m = adaln(c, w_ada[l], b_ada[l])
        h = modulate(rmsnorm(x_lat, norm1_g[l]), m, 0)
        o = mixer_latent(h, cache_k[:, l], cache_v[:, l], state_h_fwd[:, l], state_h_bwd[:, l], rpb[l], *mix_w)
        x_lat = x_lat + m[:, 2][:, None, :] * o
        h = modulate(rmsnorm(x_lat, norm2_g[l]), m, 3)
        x_lat = x_lat + m[:, 5][:, None, :] * moe(h, *moe_w)
    y_prompt = rmsnorm(x_ctx, final_g)
    y_sample = rmsnorm(x_lat, final_g)
    return (y_prompt, y_sample, jnp.stack(new_k, axis=1), jnp.stack(new_v, axis=1),
            jnp.stack(new_hf, axis=1), jnp.stack(new_hb, axis=1))
```

```python
import functools

import jax
import jax.numpy as jnp
from jax import lax
from jax.experimental import pallas as pl
from jax.experimental.pallas import tpu as pltpu

F32 = jnp.float32
BF16 = jnp.bfloat16

D_MODEL = 1024
BATCH = 16
SEQ = 256
DEC_BATCH = 2
DEC_SEQ = 1024
PAST_LEN = 512
GRID_W = 64
N_HEADS_A = 8
HEAD_DIM_A = 64
D_ATT = N_HEADS_A * HEAD_DIM_A
KH = 8
KW = 16
D_RNN = 512
N_RG_BLOCKS = 8
CONV_W = 4
RG_C = 8.0
N_EXPERTS = 64
TOP_K = 8
N_GROUPS = 8
GROUP_SIZE = N_EXPERTS // N_GROUPS
TOPK_GROUP = 4
D_EXPERT = 256
ROUTED_SCALE = 2.5
N_MOD = 6
EPS = 1e-6
NEG = -1e30
D_IN = 3 * D_ATT + 2 * D_RNN + 2 * D_MODEL

N_CTX = BATCH * SEQ
N_LAT = DEC_BATCH * DEC_SEQ
N_TOK = N_CTX + N_LAT
GRID_ROWS = DEC_SEQ // GRID_W

LANES = 128
VMEM_LIMIT = 56 * 1024 * 1024


def _params(sem, vmem=VMEM_LIMIT):
    return pltpu.CompilerParams(dimension_semantics=sem, vmem_limit_bytes=vmem)


def _mod_row(i, tile_rows):
    n_ctx_tiles = N_CTX // tile_rows
    return jnp.where(i < n_ctx_tiles, 0, 1 + (i - n_ctx_tiles) // (DEC_SEQ // tile_rows))


def _norm_mod(x, g, shift, scale):
    y = x * lax.rsqrt(jnp.mean(x * x, axis=-1, keepdims=True) + EPS)
    return (y * g) * (1.0 + scale) + shift


def _silu(x):
    return x * jax.nn.sigmoid(x)


def _adaln_kernel(c_ref, w_ref, b_ref, o_ref):
    w = w_ref[...]
    rows = []
    for r in range(3):
        s = _silu(c_ref[r])
        rows.append(jnp.sum(w * s, axis=0, keepdims=True))
    rows.append(jnp.zeros((5, w.shape[1]), F32))
    o_ref[...] = jnp.concatenate(rows, axis=0) + b_ref[...]


def _adaln(cvecs, w_ada, b_ada):
    tn = 512
    n = N_MOD * D_MODEL
    return pl.pallas_call(
        _adaln_kernel,
        grid=(n // tn,),
        in_specs=[pl.BlockSpec((3, D_MODEL, 1), lambda j: (0, 0, 0)),
                  pl.BlockSpec((D_MODEL, tn), lambda j: (0, j)),
                  pl.BlockSpec((1, tn), lambda j: (0, j))],
        out_specs=pl.BlockSpec((8, tn), lambda j: (0, j)),
        out_shape=jax.ShapeDtypeStruct((8, n), F32),
        compiler_params=_params(("parallel",)),
        name="adaln",
    )(cvecs, w_ada, b_ada)


_TM = 512
_CTX_TILES = N_CTX // _TM
_LAT_TILES = N_LAT // _TM


def _ctx_map(i):
    return (jnp.minimum(i, _CTX_TILES - 1), 0)


def _lat_map(i):
    return (jnp.maximum(i - _CTX_TILES, 0), 0)


def _inproj_kernel(xp_ref, xs_ref, mod_ref, g_ref, w_ref,
                   q_ref, kc_ref, vc_ref, kl_ref, vl_ref, xr_ref, yg_ref, ga_ref, gb_ref):
    i = pl.program_id(0)
    is_ctx = i < _CTX_TILES
    x = jnp.where(is_ctx, xp_ref[...], xs_ref[...])
    m = mod_ref[0]
    h = _norm_mod(x, g_ref[...], m[:, 0:D_MODEL], m[:, D_MODEL:2 * D_MODEL]).astype(BF16)

    def proj(a, b):
        return jnp.dot(h, w_ref[:, a:b], preferred_element_type=F32)

    q_ref[...] = proj(0, D_ATT).astype(BF16)
    k = proj(D_ATT, 2 * D_ATT)
    v = proj(2 * D_ATT, 3 * D_ATT)

    @pl.when(is_ctx)
    def _():
        kc_ref[...] = k
        vc_ref[...] = v

    @pl.when(jnp.logical_not(is_ctx))
    def _():
        kl_ref[...] = k
        vl_ref[...] = v

    o = 3 * D_ATT
    xr_ref[...] = proj(o, o + D_RNN)
    yg_ref[...] = proj(o + D_RNN, o + 2 * D_RNN).astype(BF16)
    o += 2 * D_RNN
    ga_ref[...] = proj(o, o + D_MODEL).astype(BF16)
    gb_ref[...] = proj(o + D_MODEL, o + 2 * D_MODEL).astype(BF16)


def _inproj(xp, xs, mod3, g1, w_in):
    row = lambda i: (i, 0)
    return pl.pallas_call(
        _inproj_kernel,
        grid=(N_TOK // _TM,),
        in_specs=[pl.BlockSpec((_TM, D_MODEL), _ctx_map),
                  pl.BlockSpec((_TM, D_MODEL), _lat_map),
                  pl.BlockSpec((1, 1, N_MOD * D_MODEL), lambda i: (_mod_row(i, _TM), 0, 0)),
                  pl.BlockSpec((1, D_MODEL), lambda i: (0, 0)),
                  pl.BlockSpec((D_MODEL, D_IN), lambda i: (0, 0))],
        out_specs=[pl.BlockSpec((_TM, D_ATT), row),
                   pl.BlockSpec((_TM, D_ATT), _ctx_map),
                   pl.BlockSpec((_TM, D_ATT), _ctx_map),
                   pl.BlockSpec((_TM, D_ATT), _lat_map),
                   pl.BlockSpec((_TM, D_ATT), _lat_map),
                   pl.BlockSpec((_TM, D_RNN), row),
                   pl.BlockSpec((_TM, D_RNN), row),
                   pl.BlockSpec((_TM, D_MODEL), row),
                   pl.BlockSpec((_TM, D_MODEL), row)],
        out_shape=[jax.ShapeDtypeStruct((N_TOK, D_ATT), BF16),
                   jax.ShapeDtypeStruct((N_CTX, D_ATT), F32),
                   jax.ShapeDtypeStruct((N_CTX, D_ATT), F32),
                   jax.ShapeDtypeStruct((N_LAT, D_ATT), F32),
                   jax.ShapeDtypeStruct((N_LAT, D_ATT), F32),
                   jax.ShapeDtypeStruct((N_TOK, D_RNN), F32),
                   jax.ShapeDtypeStruct((N_TOK, D_RNN), BF16),
                   jax.ShapeDtypeStruct((N_TOK, D_MODEL), BF16),
                   jax.ShapeDtypeStruct((N_TOK, D_MODEL), BF16)],
        compiler_params=_params(("arbitrary",)),
        name="inproj",
    )(xp, xs, mod3, g1, w_in)


def _pair_attention(qp, segs):
    scale = HEAD_DIM_A ** -0.5
    lane = lax.broadcasted_iota(jnp.int32, (1, LANES), 1)
    outs = []
    for a in range(2):
        sel = (lane >= HEAD_DIM_A) if a else (lane < HEAD_DIM_A)
        qm = jnp.where(sel, qp, jnp.zeros_like(qp))
        ss = []
        for kp, _, bias_fn in segs:
            s = lax.dot_general(qm, kp, (((1,), (1,)), ((), ())), preferred_element_type=F32) * scale
            if bias_fn is not None:
                s = s + bias_fn(a)
            ss.append(s)
        m = functools.reduce(jnp.maximum, [jnp.max(s, axis=-1, keepdims=True) for s in ss])
        es = [jnp.exp(s - m) for s in ss]
        inv = 1.0 / functools.reduce(jnp.add, [jnp.sum(e, axis=-1, keepdims=True) for e in es])
        o = functools.reduce(jnp.add, [
            jnp.dot((e * inv).astype(BF16), vp, preferred_element_type=F32)
            for e, (_, vp, _) in zip(es, segs)])
        outs.append(o)
    return jnp.where(lane < HEAD_DIM_A, outs[0], outs[1])


def _attn_ctx_kernel(q_ref, k_ref, v_ref, o_ref):
    for j in range(D_ATT // LANES):
        c = slice(j * LANES, (j + 1) * LANES)
        segs = [(k_ref[:, c].astype(BF16), v_ref[:, c].astype(BF16), None)]
        o_ref[:, c] = _pair_attention(q_ref[:, c], segs).astype(BF16)


def _attn_ctx(q_all, kc, vc):
    blk = pl.BlockSpec((SEQ, D_ATT), lambda b: (b, 0))
    return pl.pallas_call(
        _attn_ctx_kernel,
        grid=(BATCH,),
        in_specs=[blk, blk, blk],
        out_specs=blk,
        out_shape=jax.ShapeDtypeStruct((N_CTX, D_ATT), BF16),
        compiler_params=_params(("parallel",)),
        name="attn_ctx",
    )(q_all, kc, vc)


_QROWS = 4
_QB = _QROWS * GRID_W
_KROWS = 12
_KB = _KROWS * GRID_W


def _attn_lat_kernel(q_ref, k_ref, v_ref, ck_ref, cv_ref, t2_ref, o_ref):
    i = pl.program_id(1)
    ks = jnp.where(i < 2, 0, GRID_ROWS - _KROWS)
    kstart = pl.multiple_of(ks * GRID_W, 256)
    lane = lax.broadcasted_iota(jnp.int32, (1, LANES), 1)

    def bias_for_head(h):
        rows = []
        for rq in range(_QROWS):
            r = _QROWS * i + rq
            rs = jnp.clip(r - KH // 2, 0, GRID_ROWS - KH)
            tiles = []
            for t in range(_KROWS // 2):
                kr = ks + 2 * t
                d = jnp.clip(kr - r + (KH - 1), -1, 2 * KH - 2)
                tile = t2_ref[h, d + 1]
                v0 = ((kr >= rs) & (kr < rs + KH)).astype(jnp.int32)
                v1 = ((kr + 1 >= rs) & (kr + 1 < rs + KH)).astype(jnp.int32)
                valid = jnp.where(lane < GRID_W, v0, v1) > 0
                tiles.append(jnp.where(valid, tile, NEG))
            rows.append(jnp.concatenate(tiles, axis=1))
        return jnp.concatenate(rows, axis=0)

    for j in range(D_ATT // LANES):
        c = slice(j * LANES, (j + 1) * LANES)
        k_loc = k_ref[pl.ds(kstart, _KB), c].astype(BF16)
        v_loc = v_ref[pl.ds(kstart, _KB), c].astype(BF16)
        segs = [(k_loc, v_loc, lambda a, j=j: bias_for_head(2 * j + a)),
                (ck_ref[:, c].astype(BF16), cv_ref[:, c].astype(BF16), None)]
        o_ref[:, c] = _pair_attention(q_ref[:, c], segs).astype(BF16)


def _attn_lat(q_all, kl, vl, ck, cv, t2):
    qblocks = DEC_SEQ // _QB
    return pl.pallas_call(
        _attn_lat_kernel,
        grid=(DEC_BATCH, qblocks),
        in_specs=[pl.BlockSpec((_QB, D_ATT), lambda b, i: (N_CTX // _QB + b * qblocks + i, 0)),
                  pl.BlockSpec((DEC_SEQ, D_ATT), lambda b, i: (b, 0)),
                  pl.BlockSpec((DEC_SEQ, D_ATT), lambda b, i: (b, 0)),
                  pl.BlockSpec((None, PAST_LEN, D_ATT), lambda b, i: (b, 0, 0)),
                  pl.BlockSpec((None, PAST_LEN, D_ATT), lambda b, i: (b, 0, 0)),
                  pl.BlockSpec((N_HEADS_A, 2 * KH, GRID_W, LANES), lambda b, i: (0, 0, 0, 0))],
        out_specs=pl.BlockSpec((_QB, D_ATT), lambda b, i: (b * qblocks + i, 0)),
        out_shape=jax.ShapeDtypeStruct((N_LAT, D_ATT), BF16),
        compiler_params=_params(("parallel", "arbitrary")),
        name="attn_lat",
    )(q_all, kl, vl, ck, cv, t2)


def _bias_tables(rpb):
    w = jnp.arange(GRID_W)
    dcol = jnp.clip(w[None, :] - w[:, None], -(KW - 1), KW - 1) + (KW - 1)
    col_start = jnp.clip(w - KW // 2, 0, GRID_W - KW)
    col_in = (w[None, :] >= col_start[:, None]) & (w[None, :] < col_start[:, None] + KW)
    t = jnp.where(col_in[None, None], rpb[:, :, dcol], NEG)
    neg = jnp.full((N_HEADS_A, 1, GRID_W, GRID_W), NEG, F32)
    left = jnp.concatenate([neg, t], axis=1)
    right = jnp.concatenate([t, neg], axis=1)
    return jnp.concatenate([left, right], axis=-1)


_RG_ROWS = DEC_SEQ
_RG_CHUNK = 256
_RG_CTX_STEPS = N_CTX // _RG_ROWS


def _rg_block(nb, T, xr_ref, yg_ref, cw_ref, cb_ref, w4_ref, b4_ref, lam_ref, h0f, h0b,
              ob_ref, xc_ref, af_ref, uf_ref, ab_ref, ub_ref, hf_ref, hb_ref):
    R = nb * T
    xr = xr_ref[...]
    t_idx = lax.broadcasted_iota(jnp.int32, (R, 1), 0) & (T - 1)
    cw = cw_ref[...]
    xc = jnp.zeros((R, D_RNN), F32) + cb_ref[...]
    for j in range(CONV_W):
        off = j - 2
        if off == 0:
            tap = xr
        else:
            tap = pltpu.roll(xr, shift=(-off) % R, axis=0)
            tap = jnp.where((t_idx + off >= 0) & (t_idx + off < T), tap, 0.0)
        xc = xc + cw[j:j + 1, :] * tap
    xc_ref[...] = xc

    lam = lam_ref[...]
    z = -lam
    sp = jnp.maximum(z, 0.0) + jnp.log1p(jnp.exp(-jnp.abs(z)))

    def gate_chunk(c, carry):
        rows = pl.ds(pl.multiple_of(c * _RG_CHUNK, _RG_CHUNK), _RG_CHUNK)
        x = xc_ref[rows, :]
        g4 = jnp.dot(x.astype(BF16), w4_ref[...], preferred_element_type=F32) + b4_ref[...]
        for d, (a_ref, u_ref) in enumerate(((af_ref, uf_ref), (ab_ref, ub_ref))):
            r = jax.nn.sigmoid(g4[:, (2 * d) * D_RNN:(2 * d + 1) * D_RNN])
            ig = jax.nn.sigmoid(g4[:, (2 * d + 1) * D_RNN:(2 * d + 2) * D_RNN])
            log_a = -RG_C * r * sp[d:d + 1, :]
            a_ref[rows, :] = jnp.exp(log_a)
            u_ref[rows, :] = jnp.sqrt(jnp.maximum(1.0 - jnp.exp(2.0 * log_a), 0.0)) * (ig * x)
        return carry

    lax.fori_loop(0, R // _RG_CHUNK, gate_chunk, 0)

    def step(t, carry):
        hf, hb = carry
        tb = T - 1 - t
        nf, nbk = [], []
        for bi in range(nb):
            rf = pl.ds(bi * T + t, 1)
            rb = pl.ds(bi * T + tb, 1)
            f = af_ref[rf, :] * hf[bi] + uf_ref[rf, :]
            b = ab_ref[rb, :] * hb[bi] + ub_ref[rb, :]
            hf_ref[rf, :] = f
            hb_ref[rb, :] = b
            nf.append(f)
            nbk.append(b)
        return tuple(nf), tuple(nbk)

    hf, hb = lax.fori_loop(0, T, step, (tuple(h0f), tuple(h0b)))

    def out_chunk(c, carry):
        rows = pl.ds(pl.multiple_of(c * _RG_CHUNK, _RG_CHUNK), _RG_CHUNK)
        y = yg_ref[rows, :].astype(F32)
        ob_ref[rows, :] = ((hf_ref[rows, :] + hb_ref[rows, :]) * jax.nn.gelu(y)).astype(BF16)
        return carry

    lax.fori_loop(0, R // _RG_CHUNK, out_chunk, 0)
    return hf, hb


def _rglru_kernel(xr_ref, yg_ref, cw_ref, cb_ref, w4_ref, b4_ref, lam_ref, sf_ref, sb_ref,
                  ob_ref, hlf_ref, hlb_ref, *scratch):
    i = pl.program_id(0)
    shared = (xr_ref, yg_ref, cw_ref, cb_ref, w4_ref, b4_ref, lam_ref)
    nb_ctx = _RG_ROWS // SEQ

    @pl.when(i < _RG_CTX_STEPS)
    def _():
        zero = [jnp.zeros((1, D_RNN), F32)] * nb_ctx
        hf, hb = _rg_block(nb_ctx, SEQ, *shared, zero, zero, ob_ref, *scratch)
        for bi in range(nb_ctx):
            hlf_ref[0, bi:bi + 1, :] = hf[bi]
            hlb_ref[0, bi:bi + 1, :] = hb[bi]

    @pl.when(i >= _RG_CTX_STEPS)
    def _():
        b = i - _RG_CTX_STEPS
        _rg_block(1, DEC_SEQ, *shared, [sf_ref[pl.ds(b, 1), :]], [sb_ref[pl.ds(b, 1), :]],
                  ob_ref, *scratch)


def _rglru(xr, yg, conv_w, conv_b, w4, b4, lam, sf, sb):
    steps = N_TOK // _RG_ROWS
    nb_ctx = _RG_ROWS // SEQ
    const = lambda shape: pl.BlockSpec(shape, lambda i: (0,) * len(shape))
    hl_map = lambda i: (jnp.minimum(i, _RG_CTX_STEPS - 1), 0, 0)
    return pl.pallas_call(
        _rglru_kernel,
        grid=(steps,),
        in_specs=[pl.BlockSpec((_RG_ROWS, D_RNN), lambda i: (i, 0)),
                  pl.BlockSpec((_RG_ROWS, D_RNN), lambda i: (i, 0)),
                  const((CONV_W, D_RNN)), const((1, D_RNN)),
                  const((D_RNN, 4 * D_RNN)), const((1, 4 * D_RNN)), const((2, D_RNN)),
                  const((DEC_BATCH, D_RNN)), const((DEC_BATCH, D_RNN))],
        out_specs=[pl.BlockSpec((_RG_ROWS, D_RNN), lambda i: (i, 0)),
                   pl.BlockSpec((1, nb_ctx, D_RNN), hl_map),
                   pl.BlockSpec((1, nb_ctx, D_RNN), hl_map)],
        out_shape=[jax.ShapeDtypeStruct((N_TOK, D_RNN), BF16),
                   jax.ShapeDtypeStruct((_RG_CTX_STEPS, nb_ctx, D_RNN), F32),
                   jax.ShapeDtypeStruct((_RG_CTX_STEPS, nb_ctx, D_RNN), F32)],
        scratch_shapes=[pltpu.VMEM((_RG_ROWS, D_RNN), F32)] * 7,
        compiler_params=_params(("arbitrary",)),
        name="rglru",
    )(xr, yg, conv_w, conv_b, w4, b4, lam, sf, sb)


def _block_diag(w):
    n, d, _ = w.shape
    eye = jnp.eye(n, dtype=w.dtype)
    return jnp.einsum('nij,nm->nimj', w, eye).reshape(n * d, n * d)


def _nt_dot(a, b):
    return lax.dot_general(a, b, (((1,), (1,)), ((), ())), preferred_element_type=F32)


def _route(s, sb):
    n = s.shape[1]
    low = -3.0e38
    sb3 = sb.reshape(N_GROUPS, GROUP_SIZE, n)
    iw = lax.broadcasted_iota(jnp.int32, sb3.shape, 1)
    m1 = jnp.max(sb3, axis=1, keepdims=True)
    i1 = jnp.min(jnp.where(sb3 == m1, iw, GROUP_SIZE), axis=1, keepdims=True)
    m2 = jnp.max(jnp.where(iw == i1, low, sb3), axis=1, keepdims=True)
    gscore = m1 + m2
    ig = lax.broadcasted_iota(jnp.int32, gscore.shape, 0)
    beaten = jnp.zeros(gscore.shape, jnp.int32)
    for g in range(N_GROUPS):
        row = gscore[g:g + 1]
        beaten = beaten + ((row > gscore) | ((row == gscore) & (g < ig))).astype(jnp.int32)
    gsel = beaten < TOPK_GROUP
    work = jnp.where(gsel, sb3, NEG).reshape(N_EXPERTS, n)
    ie = lax.broadcasted_iota(jnp.int32, work.shape, 0)
    chosen = jnp.zeros(work.shape, jnp.bool_)
    for _ in range(TOP_K):
        m = jnp.max(work, axis=0, keepdims=True)
        pick = ie == jnp.min(jnp.where(work == m, ie, N_EXPERTS), axis=0, keepdims=True)
        chosen = chosen | pick
        work = jnp.where(pick, low, work)
    wsel = jnp.where(chosen, s, 0.0)
    return wsel / jnp.sum(wsel, axis=0, keepdims=True) * ROUTED_SCALE


def _merge_kernel(oac_ref, oal_ref, ob_ref, ga_ref, gb_ref, xp_ref, xs_ref, mod_ref, g_ref,
                  wpa_ref, wpb_ref, wo_ref, wrh_ref, wrl_ref, br_ref,
                  x1_ref, h2_ref, gates_ref):
    i = pl.program_id(0)
    is_ctx = i < _CTX_TILES
    oa = jnp.where(is_ctx, oac_ref[...], oal_ref[...])
    x = jnp.where(is_ctx, xp_ref[...], xs_ref[...])
    m = mod_ref[0]
    sl = lambda k: m[:, k * D_MODEL:(k + 1) * D_MODEL]
    mix = (jax.nn.sigmoid(ga_ref[...].astype(F32)) * jnp.dot(oa, wpa_ref[...], preferred_element_type=F32)
           + jax.nn.sigmoid(gb_ref[...].astype(F32)) * jnp.dot(ob_ref[...], wpb_ref[...], preferred_element_type=F32))
    o = jnp.dot(mix.astype(BF16), wo_ref[...], preferred_element_type=F32)
    x1 = x + sl(2) * o
    x1_ref[...] = x1
    h2 = _norm_mod(x1, g_ref[...], sl(3), sl(4))
    h_hi = h2.astype(BF16)
    h2_ref[...] = h_hi
    h_lo = (h2 - h_hi.astype(F32)).astype(BF16)
    logits = _nt_dot(wrh_ref[...], h_hi) + _nt_dot(wrh_ref[...], h_lo) + _nt_dot(wrl_ref[...], h_hi)
    s = jax.nn.sigmoid(logits)
    gates_t = _route(s, s + br_ref[...])
    padded = jnp.concatenate([gates_t, jnp.zeros((LANES - N_EXPERTS, _TM), F32)], axis=0)
    gates_ref[...] = padded.T


def _merge(oac, oal, ob, ga, gb, xp, xs, mod3, g2, wpa, wpb, wo, wrh, wrl, br):
    row = lambda i: (i, 0)
    const = lambda shape: pl.BlockSpec(shape, lambda i: (0,) * len(shape))
    return pl.pallas_call(
        _merge_kernel,
        grid=(N_TOK // _TM,),
        in_specs=[pl.BlockSpec((_TM, D_ATT), _ctx_map),
                  pl.BlockSpec((_TM, D_ATT), _lat_map),
                  pl.BlockSpec((_TM, D_RNN), row),
                  pl.BlockSpec((_TM, D_MODEL), row),
                  pl.BlockSpec((_TM, D_MODEL), row),
                  pl.BlockSpec((_TM, D_MODEL), _ctx_map),
                  pl.BlockSpec((_TM, D_MODEL), _lat_map),
                  pl.BlockSpec((1, 1, N_MOD * D_MODEL), lambda i: (_mod_row(i, _TM), 0, 0)),
                  const((1, D_MODEL)),
                  const((D_ATT, D_MODEL)), const((D_RNN, D_MODEL)), const((D_MODEL, D_MODEL)),
                  const((N_EXPERTS, D_MODEL)), const((N_EXPERTS, D_MODEL)), const((N_EXPERTS, 1))],
        out_specs=[pl.BlockSpec((_TM, D_MODEL), row),
                   pl.BlockSpec((_TM, D_MODEL), row),
                   pl.BlockSpec((_TM, LANES), row)],
        out_shape=[jax.ShapeDtypeStruct((N_TOK, D_MODEL), F32),
                   jax.ShapeDtypeStruct((N_TOK, D_MODEL), BF16),
                   jax.ShapeDtypeStruct((N_TOK, LANES), F32)],
        compiler_params=_params(("parallel",)),
        name="merge",
    )(oac, oal, ob, ga, gb, xp, xs, mod3, g2, wpa, wpb, wo, wrh, wrl, br)


_MOE_TM = 1024
_MOE_EPS = 2
_MOE_CTX_TILES = N_CTX // _MOE_TM


def _moe_kernel(h_ref, gates_ref, x1_ref, mod_ref, fg_ref, wg_ref, wu_ref, wd_ref,
                wgs_ref, wus_ref, wds_ref, yp_ref, ys_ref, acc_ref):
    i = pl.program_id(0)
    g = pl.program_id(1)
    h = h_ref[...]

    @pl.when(g == 0)
    def _():
        a = _silu(jnp.dot(h, wgs_ref[...], preferred_element_type=F32)) * jnp.dot(
            h, wus_ref[...], preferred_element_type=F32)
        acc_ref[...] = jnp.dot(a.astype(BF16), wds_ref[...], preferred_element_type=F32)

    gv = gates_ref[...]
    g_hi = gv.astype(BF16)
    g_lo = (gv - g_hi.astype(F32)).astype(BF16)
    sel_row = lax.broadcasted_iota(jnp.int32, (LANES, D_EXPERT), 0)
    for j in range(_MOE_EPS):
        e = g * _MOE_EPS + j
        spread = (sel_row == e).astype(BF16)
        gcol = (jnp.dot(g_hi, spread, preferred_element_type=F32)
                + jnp.dot(g_lo, spread, preferred_element_type=F32))
        gt = jnp.dot(h, wg_ref[j].astype(BF16), preferred_element_type=F32)
        up = jnp.dot(h, wu_ref[j].astype(BF16), preferred_element_type=F32)
        act = (_silu(gt) * up * gcol).astype(BF16)
        acc_ref[...] += jnp.dot(act, wd_ref[j].astype(BF16), preferred_element_type=F32)

    @pl.when(g == pl.num_programs(1) - 1)
    def _():
        m = mod_ref[0]
        x2 = x1_ref[...] + m[:, 5 * D_MODEL:6 * D_MODEL] * acc_ref[...]
        y = x2 * lax.rsqrt(jnp.mean(x2 * x2, axis=-1, keepdims=True) + EPS) * fg_ref[...]

        @pl.when(i < _MOE_CTX_TILES)
        def _():
            yp_ref[...] = y

        @pl.when(i >= _MOE_CTX_TILES)
        def _():
            ys_ref[...] = y


def _moe(h2, gates, x1, mod3, fg, wg, wu, wd, wgs, wus, wds):
    tm = _MOE_TM
    row = lambda i, g: (i, 0)
    const = lambda shape: pl.BlockSpec(shape, lambda i, g: (0,) * len(shape))
    return pl.pallas_call(
        _moe_kernel,
        grid=(N_TOK // tm, N_EXPERTS // _MOE_EPS),
        in_specs=[pl.BlockSpec((tm, D_MODEL), row),
                  pl.BlockSpec((tm, LANES), row),
                  pl.BlockSpec((tm, D_MODEL), row),
                  pl.BlockSpec((1, 1, N_MOD * D_MODEL), lambda i, g: (_mod_row(i, tm), 0, 0)),
                  const((1, D_MODEL)),
                  pl.BlockSpec((_MOE_EPS, D_MODEL, D_EXPERT), lambda i, g: (g, 0, 0)),
                  pl.BlockSpec((_MOE_EPS, D_MODEL, D_EXPERT), lambda i, g: (g, 0, 0)),
                  pl.BlockSpec((_MOE_EPS, D_EXPERT, D_MODEL), lambda i, g: (g, 0, 0)),
                  const((D_MODEL, D_EXPERT)), const((D_MODEL, D_EXPERT)), const((D_EXPERT, D_MODEL))],
        out_specs=[pl.BlockSpec((tm, D_MODEL), lambda i, g: (jnp.minimum(i, _MOE_CTX_TILES - 1), 0)),
                   pl.BlockSpec((tm, D_MODEL), lambda i, g: (jnp.maximum(i - _MOE_CTX_TILES, 0), 0))],
        out_shape=[jax.ShapeDtypeStruct((N_CTX, D_MODEL), F32),
                   jax.ShapeDtypeStruct((N_LAT, D_MODEL), F32)],
        scratch_shapes=[pltpu.VMEM((tm, D_MODEL), F32)],
        compiler_params=_params(("arbitrary", "arbitrary")),
        name="moe",
    )(h2, gates, x1, mod3, fg, wg, wu, wd, wgs, wus, wds)


def kernel(x_prompt, x_sample, cache_k, cache_v, state_h_fwd, state_h_bwd, c, c_ctx, norm1_g, norm2_g, final_g, w_ada, b_ada, w_in, rpb, conv_w, conv_b, rg_wa, rg_ba, rg_wx, rg_bx, rg_lam, w_pa, w_pb, w_out, w_router, b_router, w_gate_e, w_up_e, w_down_e, w_gate_s, w_up_s, w_down_s):
    l = 0
    xp = x_prompt.reshape(N_CTX, D_MODEL)
    xs = x_sample.reshape(N_LAT, D_MODEL)

    cvecs = jnp.concatenate([c_ctx[None, :], c], axis=0)[:, :, None]
    mod = _adaln(cvecs, w_ada[l], b_ada[l][None, :])
    mod3 = mod.reshape(8, 1, N_MOD * D_MODEL)

    q, kc, vc, kl, vl, xr, yg, ga, gb = _inproj(xp, xs, mod3, norm1_g[l][None, :], w_in[l].astype(BF16))

    oac = _attn_ctx(q, kc, vc)
    ck = cache_k[:, l].reshape(DEC_BATCH, PAST_LEN, D_ATT)
    cv = cache_v[:, l].reshape(DEC_BATCH, PAST_LEN, D_ATT)
    oal = _attn_lat(q, kl, vl, ck, cv, _bias_tables(rpb[l]))

    w4 = jnp.concatenate([_block_diag(rg_wa[l, 0]), _block_diag(rg_wx[l, 0]),
                          _block_diag(rg_wa[l, 1]), _block_diag(rg_wx[l, 1])], axis=1).astype(BF16)
    b4 = jnp.concatenate([rg_ba[l, 0], rg_bx[l, 0], rg_ba[l, 1], rg_bx[l, 1]])[None, :]
    ob, hlf, hlb = _rglru(xr, yg, conv_w[l], conv_b[l][None, :], w4, b4, rg_lam[l],
                          state_h_fwd[:, l], state_h_bwd[:, l])

    wr_t = w_router[l].T
    wr_hi = wr_t.astype(BF16)
    wr_lo = (wr_t - wr_hi.astype(F32)).astype(BF16)
    x1, h2, gates = _merge(oac, oal, ob, ga, gb, xp, xs, mod3, norm2_g[l][None, :],
                           w_pa[l].astype(BF16), w_pb[l].astype(BF16), w_out[l].astype(BF16),
                           wr_hi, wr_lo, b_router[l][:, None])

    yp, ys = _moe(h2, gates, x1, mod3, final_g[None, :], w_gate_e[l], w_up_e[l], w_down_e[l],
                  w_gate_s[l].astype(BF16), w_up_s[l].astype(BF16), w_down_s[l].astype(BF16))

    return (yp.reshape(BATCH, SEQ, D_MODEL),
            ys.reshape(DEC_BATCH, DEC_SEQ, D_MODEL),
            kc.reshape(BATCH, 1, SEQ, N_HEADS_A, HEAD_DIM_A),
            vc.reshape(BATCH, 1, SEQ, N_HEADS_A, HEAD_DIM_A),
            hlf.reshape(BATCH, 1, D_RNN),
            hlb.reshape(BATCH, 1, D_RNN))
```

```python
import functools

import jax
import jax.numpy as jnp
from jax import lax
from jax.experimental import pallas as pl
from jax.experimental.pallas import tpu as pltpu

F32 = jnp.float32
BF16 = jnp.bfloat16

D_MODEL = 1024
BATCH = 16
SEQ = 256
DEC_BATCH = 2
DEC_SEQ = 1024
PAST_LEN = 512
GRID_W = 64
N_HEADS_A = 8
HEAD_DIM_A = 64
D_ATT = N_HEADS_A * HEAD_DIM_A
KH = 8
KW = 16
D_RNN = 512
N_RG_BLOCKS = 8
CONV_W = 4
RG_C = 8.0
N_EXPERTS = 64
TOP_K = 8
N_GROUPS = 8
GROUP_SIZE = N_EXPERTS // N_GROUPS
TOPK_GROUP = 4
D_EXPERT = 256
ROUTED_SCALE = 2.5
N_MOD = 6
EPS = 1e-6
NEG = -1e30
D_IN = 3 * D_ATT + 2 * D_RNN + 2 * D_MODEL

N_CTX = BATCH * SEQ
N_LAT = DEC_BATCH * DEC_SEQ
N_TOK = N_CTX + N_LAT
GRID_ROWS = DEC_SEQ // GRID_W

LANES = 128
VMEM_LIMIT = 56 * 1024 * 1024


def _params(sem, vmem=VMEM_LIMIT):
    return pltpu.CompilerParams(dimension_semantics=sem, vmem_limit_bytes=vmem)


def _mod_row(i, tile_rows):
    n_ctx_tiles = N_CTX // tile_rows
    return jnp.where(i < n_ctx_tiles, 0, 1 + (i - n_ctx_tiles) // (DEC_SEQ // tile_rows))


def _norm_mod(x, g, shift, scale):
    y = x * lax.rsqrt(jnp.mean(x * x, axis=-1, keepdims=True) + EPS)
    return (y * g) * (1.0 + scale) + shift


def _silu(x):
    return x * jax.nn.sigmoid(x)


def _adaln_kernel(c_ref, w_ref, b_ref, o_ref):
    w = w_ref[...]
    rows = []
    for r in range(3):
        s = _silu(c_ref[r])
        rows.append(jnp.sum(w * s, axis=0, keepdims=True))
    rows.append(jnp.zeros((5, w.shape[1]), F32))
    o_ref[...] = jnp.concatenate(rows, axis=0) + b_ref[...]


def _adaln(cvecs, w_ada, b_ada):
    tn = 512
    n = N_MOD * D_MODEL
    return pl.pallas_call(
        _adaln_kernel,
        grid=(n // tn,),
        in_specs=[pl.BlockSpec((3, D_MODEL, 1), lambda j: (0, 0, 0)),
                  pl.BlockSpec((D_MODEL, tn), lambda j: (0, j)),
                  pl.BlockSpec((1, tn), lambda j: (0, j))],
        out_specs=pl.BlockSpec((8, tn), lambda j: (0, j)),
        out_shape=jax.ShapeDtypeStruct((8, n), F32),
        compiler_params=_params(("parallel",)),
        name="adaln",
    )(cvecs, w_ada, b_ada)


_TM = 512
_CTX_TILES = N_CTX // _TM
_LAT_TILES = N_LAT // _TM


def _ctx_map(i):
    return (jnp.minimum(i, _CTX_TILES - 1), 0)


def _lat_map(i):
    return (jnp.maximum(i - _CTX_TILES, 0), 0)


def _inproj_kernel(xp_ref, xs_ref, mod_ref, g_ref, w_ref,
                   q_ref, kc_ref, vc_ref, kl_ref, vl_ref, xr_ref, yg_ref, ga_ref, gb_ref):
    i = pl.program_id(0)
    is_ctx = i < _CTX_TILES
    x = jnp.where(is_ctx, xp_ref[...], xs_ref[...])
    m = mod_ref[0]
    h = _norm_mod(x, g_ref[...], m[:, 0:D_MODEL], m[:, D_MODEL:2 * D_MODEL]).astype(BF16)

    def proj(a, b):
        return jnp.dot(h, w_ref[:, a:b], preferred_element_type=F32)

    q_ref[...] = proj(0, D_ATT).astype(BF16)
    k = proj(D_ATT, 2 * D_ATT)
    v = proj(2 * D_ATT, 3 * D_ATT)

    @pl.when(is_ctx)
    def _():
        kc_ref[...] = k
        vc_ref[...] = v

    @pl.when(jnp.logical_not(is_ctx))
    def _():
        kl_ref[...] = k
        vl_ref[...] = v

    o = 3 * D_ATT
    xr_ref[...] = proj(o, o + D_RNN)
    yg_ref[...] = proj(o + D_RNN, o + 2 * D_RNN).astype(BF16)
    o += 2 * D_RNN
    ga_ref[...] = proj(o, o + D_MODEL).astype(BF16)
    gb_ref[...] = proj(o + D_MODEL, o + 2 * D_MODEL).astype(BF16)


def _inproj(xp, xs, mod3, g1, w_in):
    row = lambda i: (i, 0)
    return pl.pallas_call(
        _inproj_kernel,
        grid=(N_TOK // _TM,),
        in_specs=[pl.BlockSpec((_TM, D_MODEL), _ctx_map),
                  pl.BlockSpec((_TM, D_MODEL), _lat_map),
                  pl.BlockSpec((1, 1, N_MOD * D_MODEL), lambda i: (_mod_row(i, _TM), 0, 0)),
                  pl.BlockSpec((1, D_MODEL), lambda i: (0, 0)),
                  pl.BlockSpec((D_MODEL, D_IN), lambda i: (0, 0))],
        out_specs=[pl.BlockSpec((_TM, D_ATT), row),
                   pl.BlockSpec((_TM, D_ATT), _ctx_map),
                   pl.BlockSpec((_TM, D_ATT), _ctx_map),
                   pl.BlockSpec((_TM, D_ATT), _lat_map),
                   pl.BlockSpec((_TM, D_ATT), _lat_map),
                   pl.BlockSpec((_TM, D_RNN), row),
                   pl.BlockSpec((_TM, D_RNN), row),
                   pl.BlockSpec((_TM, D_MODEL), row),
                   pl.BlockSpec((_TM, D_MODEL), row)],
        out_shape=[jax.ShapeDtypeStruct((N_TOK, D_ATT), BF16),
                   jax.ShapeDtypeStruct((N_CTX, D_ATT), F32),
                   jax.ShapeDtypeStruct((N_CTX, D_ATT), F32),
                   jax.ShapeDtypeStruct((N_LAT, D_ATT), F32),
                   jax.ShapeDtypeStruct((N_LAT, D_ATT), F32),
                   jax.ShapeDtypeStruct((N_TOK, D_RNN), F32),
                   jax.ShapeDtypeStruct((N_TOK, D_RNN), BF16),
                   jax.ShapeDtypeStruct((N_TOK, D_MODEL), BF16),
                   jax.ShapeDtypeStruct((N_TOK, D_MODEL), BF16)],
        compiler_params=_params(("arbitrary",)),
        name="inproj",
    )(xp, xs, mod3, g1, w_in)


def _pair_attention(qp, segs):
    scale = HEAD_DIM_A ** -0.5
    lane = lax.broadcasted_iota(jnp.int32, (1, LANES), 1)
    outs = []
    for a in range(2):
        sel = (lane >= HEAD_DIM_A) if a else (lane < HEAD_DIM_A)
        qm = jnp.where(sel, qp, jnp.zeros_like(qp))
        ss = []
        for kp, _, bias_fn in segs:
            s = lax.dot_general(qm, kp, (((1,), (1,)), ((), ())), preferred_element_type=F32) * scale
            if bias_fn is not None:
                s = s + bias_fn(a)
            ss.append(s)
        m = functools.reduce(jnp.maximum, [jnp.max(s, axis=-1, keepdims=True) for s in ss])
        es = [jnp.exp(s - m) for s in ss]
        inv = 1.0 / functools.reduce(jnp.add, [jnp.sum(e, axis=-1, keepdims=True) for e in es])
        o = functools.reduce(jnp.add, [
            jnp.dot((e * inv).astype(BF16), vp, preferred_element_type=F32)
            for e, (_, vp, _) in zip(es, segs)])
        outs.append(o)
    return jnp.where(lane < HEAD_DIM_A, outs[0], outs[1])


def _attn_ctx_kernel(q_ref, k_ref, v_ref, o_ref):
    for j in range(D_ATT // LANES):
        c = slice(j * LANES, (j + 1) * LANES)
        segs = [(k_ref[:, c].astype(BF16), v_ref[:, c].astype(BF16), None)]
        o_ref[:, c] = _pair_attention(q_ref[:, c], segs).astype(BF16)


def _attn_ctx(q_all, kc, vc):
    blk = pl.BlockSpec((SEQ, D_ATT), lambda b: (b, 0))
    return pl.pallas_call(
        _attn_ctx_kernel,
        grid=(BATCH,),
        in_specs=[blk, blk, blk],
        out_specs=blk,
        out_shape=jax.ShapeDtypeStruct((N_CTX, D_ATT), BF16),
        compiler_params=_params(("parallel",)),
        name="attn_ctx",
    )(q_all, kc, vc)


_QROWS = 4
_QB = _QROWS * GRID_W
_KROWS = 12
_KB = _KROWS * GRID_W


def _attn_lat_kernel(q_ref, k_ref, v_ref, ck_ref, cv_ref, t2_ref, o_ref):
    i = pl.program_id(1)
    ks = jnp.where(i < 2, 0, GRID_ROWS - _KROWS)
    kstart = pl.multiple_of(ks * GRID_W, 256)
    lane = lax.broadcasted_iota(jnp.int32, (1, LANES), 1)

    def bias_for_head(h):
        rows = []
        for rq in range(_QROWS):
            r = _QROWS * i + rq
            rs = jnp.clip(r - KH // 2, 0, GRID_ROWS - KH)
            tiles = []
            for t in range(_KROWS // 2):
                kr = ks + 2 * t
                d = jnp.clip(kr - r + (KH - 1), -1, 2 * KH - 2)
                tile = t2_ref[h, d + 1]
                v0 = ((kr >= rs) & (kr < rs + KH)).astype(jnp.int32)
                v1 = ((kr + 1 >= rs) & (kr + 1 < rs + KH)).astype(jnp.int32)
                valid = jnp.where(lane < GRID_W, v0, v1) > 0
                tiles.append(jnp.where(valid, tile, NEG))
            rows.append(jnp.concatenate(tiles, axis=1))
        return jnp.concatenate(rows, axis=0)

    for j in range(D_ATT // LANES):
        c = slice(j * LANES, (j + 1) * LANES)
        k_loc = k_ref[pl.ds(kstart, _KB), c].astype(BF16)
        v_loc = v_ref[pl.ds(kstart, _KB), c].astype(BF16)
        segs = [(k_loc, v_loc, lambda a, j=j: bias_for_head(2 * j + a)),
                (ck_ref[:, c].astype(BF16), cv_ref[:, c].astype(BF16), None)]
        o_ref[:, c] = _pair_attention(q_ref[:, c], segs).astype(BF16)


def _attn_lat(q_all, kl, vl, ck, cv, t2):
    qblocks = DEC_SEQ // _QB
    return pl.pallas_call(
        _attn_lat_kernel,
        grid=(DEC_BATCH, qblocks),
        in_specs=[pl.BlockSpec((_QB, D_ATT), lambda b, i: (N_CTX // _QB + b * qblocks + i, 0)),
                  pl.BlockSpec((DEC_SEQ, D_ATT), lambda b, i: (b, 0)),
                  pl.BlockSpec((DEC_SEQ, D_ATT), lambda b, i: (b, 0)),
                  pl.BlockSpec((None, PAST_LEN, D_ATT), lambda b, i: (b, 0, 0)),
                  pl.BlockSpec((None, PAST_LEN, D_ATT), lambda b, i: (b, 0, 0)),
                  pl.BlockSpec((N_HEADS_A, 2 * KH, GRID_W, LANES), lambda b, i: (0, 0, 0, 0))],
        out_specs=pl.BlockSpec((_QB, D_ATT), lambda b, i: (b * qblocks + i, 0)),
        out_shape=jax.ShapeDtypeStruct((N_LAT, D_ATT), BF16),
        compiler_params=_params(("parallel", "arbitrary")),
        name="attn_lat",
    )(q_all, kl, vl, ck, cv, t2)


def _bias_tables(rpb):
    w = jnp.arange(GRID_W)
    dcol = jnp.clip(w[None, :] - w[:, None], -(KW - 1), KW - 1) + (KW - 1)
    col_start = jnp.clip(w - KW // 2, 0, GRID_W - KW)
    col_in = (w[None, :] >= col_start[:, None]) & (w[None, :] < col_start[:, None] + KW)
    t = jnp.where(col_in[None, None], rpb[:, :, dcol], NEG)
    neg = jnp.full((N_HEADS_A, 1, GRID_W, GRID_W), NEG, F32)
    left = jnp.concatenate([neg, t], axis=1)
    right = jnp.concatenate([t, neg], axis=1)
    return jnp.concatenate([left, right], axis=-1)


_RG_ROWS = DEC_SEQ
_RG_CHUNK = 256
_RG_CTX_STEPS = N_CTX // _RG_ROWS


def _rg_block(nb, T, xr_ref, yg_ref, cw_ref, cb_ref, w4_ref, b4_ref, lam_ref, h0f, h0b,
              ob_ref, xc_ref, af_ref, uf_ref, ab_ref, ub_ref, hf_ref, hb_ref):
    R = nb * T
    xr = xr_ref[...]
    t_idx = lax.broadcasted_iota(jnp.int32, (R, 1), 0) & (T - 1)
    cw = cw_ref[...]
    xc = jnp.zeros((R, D_RNN), F32) + cb_ref[...]
    for j in range(CONV_W):
        off = j - 2
        if off == 0:
            tap = xr
        else:
            tap = pltpu.roll(xr, shift=(-off) % R, axis=0)
            tap = jnp.where((t_idx + off >= 0) & (t_idx + off < T), tap, 0.0)
        xc = xc + cw[j:j + 1, :] * tap
    xc_ref[...] = xc

    lam = lam_ref[...]
    z = -lam
    sp = jnp.maximum(z, 0.0) + jnp.log1p(jnp.exp(-jnp.abs(z)))

    def gate_chunk(c, carry):
        rows = pl.ds(pl.multiple_of(c * _RG_CHUNK, _RG_CHUNK), _RG_CHUNK)
        x = xc_ref[rows, :]
        g4 = jnp.dot(x.astype(BF16), w4_ref[...], preferred_element_type=F32) + b4_ref[...]
        for d, (a_ref, u_ref) in enumerate(((af_ref, uf_ref), (ab_ref, ub_ref))):
            r = jax.nn.sigmoid(g4[:, (2 * d) * D_RNN:(2 * d + 1) * D_RNN])
            ig = jax.nn.sigmoid(g4[:, (2 * d + 1) * D_RNN:(2 * d + 2) * D_RNN])
            log_a = -RG_C * r * sp[d:d + 1, :]
            a_ref[rows, :] = jnp.exp(log_a)
            u_ref[rows, :] = jnp.sqrt(jnp.maximum(1.0 - jnp.exp(2.0 * log_a), 0.0)) * (ig * x)
        return carry

    lax.fori_loop(0, R // _RG_CHUNK, gate_chunk, 0)

    def step(t, carry):
        hf, hb = carry
        tb = T - 1 - t
        nf, nbk = [], []
        for bi in range(nb):
            rf = pl.ds(bi * T + t, 1)
            rb = pl.ds(bi * T + tb, 1)
            f = af_ref[rf, :] * hf[bi] + uf_ref[rf, :]
            b = ab_ref[rb, :] * hb[bi] + ub_ref[rb, :]
            hf_ref[rf, :] = f
            hb_ref[rb, :] = b
            nf.append(f)
            nbk.append(b)
        return tuple(nf), tuple(nbk)

    hf, hb = lax.fori_loop(0, T, step, (tuple(h0f), tuple(h0b)))

    def out_chunk(c, carry):
        rows = pl.ds(pl.multiple_of(c * _RG_CHUNK, _RG_CHUNK), _RG_CHUNK)
        y = yg_ref[rows, :].astype(F32)
        ob_ref[rows, :] = ((hf_ref[rows, :] + hb_ref[rows, :]) * jax.nn.gelu(y)).astype(BF16)
        return carry

    lax.fori_loop(0, R // _RG_CHUNK, out_chunk, 0)
    return hf, hb


def _rglru_kernel(xr_ref, yg_ref, cw_ref, cb_ref, w4_ref, b4_ref, lam_ref, sf_ref, sb_ref,
                  ob_ref, hlf_ref, hlb_ref, *scratch):
    i = pl.program_id(0)
    shared = (xr_ref, yg_ref, cw_ref, cb_ref, w4_ref, b4_ref, lam_ref)
    nb_ctx = _RG_ROWS // SEQ

    @pl.when(i < _RG_CTX_STEPS)
    def _():
        zero = [jnp.zeros((1, D_RNN), F32)] * nb_ctx
        hf, hb = _rg_block(nb_ctx, SEQ, *shared, zero, zero, ob_ref, *scratch)
        for bi in range(nb_ctx):
            hlf_ref[0, bi:bi + 1, :] = hf[bi]
            hlb_ref[0, bi:bi + 1, :] = hb[bi]

    @pl.when(i >= _RG_CTX_STEPS)
    def _():
        b = i - _RG_CTX_STEPS
        _rg_block(1, DEC_SEQ, *shared, [sf_ref[pl.ds(b, 1), :]], [sb_ref[pl.ds(b, 1), :]],
                  ob_ref, *scratch)


def _rglru(xr, yg, conv_w, conv_b, w4, b4, lam, sf, sb):
    steps = N_TOK // _RG_ROWS
    nb_ctx = _RG_ROWS // SEQ
    const = lambda shape: pl.BlockSpec(shape, lambda i: (0,) * len(shape))
    hl_map = lambda i: (jnp.minimum(i, _RG_CTX_STEPS - 1), 0, 0)
    return pl.pallas_call(
        _rglru_kernel,
        grid=(steps,),
        in_specs=[pl.BlockSpec((_RG_ROWS, D_RNN), lambda i: (i, 0)),
                  pl.BlockSpec((_RG_ROWS, D_RNN), lambda i: (i, 0)),
                  const((CONV_W, D_RNN)), const((1, D_RNN)),
                  const((D_RNN, 4 * D_RNN)), const((1, 4 * D_RNN)), const((2, D_RNN)),
                  const((DEC_BATCH, D_RNN)), const((DEC_BATCH, D_RNN))],
        out_specs=[pl.BlockSpec((_RG_ROWS, D_RNN), lambda i: (i, 0)),
                   pl.BlockSpec((1, nb_ctx, D_RNN), hl_map),
                   pl.BlockSpec((1, nb_ctx, D_RNN), hl_map)],
        out_shape=[jax.ShapeDtypeStruct((N_TOK, D_RNN), BF16),
                   jax.ShapeDtypeStruct((_RG_CTX_STEPS, nb_ctx, D_RNN), F32),
                   jax.ShapeDtypeStruct((_RG_CTX_STEPS, nb_ctx, D_RNN), F32)],
        scratch_shapes=[pltpu.VMEM((_RG_ROWS, D_RNN), F32)] * 7,
        compiler_params=_params(("arbitrary",)),
        name="rglru",
    )(xr, yg, conv_w, conv_b, w4, b4, lam, sf, sb)


def _block_diag(w):
    n, d, _ = w.shape
    eye = jnp.eye(n, dtype=w.dtype)
    return jnp.einsum('nij,nm->nimj', w, eye).reshape(n * d, n * d)


def _nt_dot(a, b):
    return lax.dot_general(a, b, (((1,), (1,)), ((), ())), preferred_element_type=F32)


def _route(s, sb):
    n = s.shape[1]
    low = -3.0e38
    sb3 = sb.reshape(N_GROUPS, GROUP_SIZE, n)
    iw = lax.broadcasted_iota(jnp.int32, sb3.shape, 1)
    m1 = jnp.max(sb3, axis=1, keepdims=True)
    i1 = jnp.min(jnp.where(sb3 == m1, iw, GROUP_SIZE), axis=1, keepdims=True)
    m2 = jnp.max(jnp.where(iw == i1, low, sb3), axis=1, keepdims=True)
    gscore = m1 + m2
    ig = lax.broadcasted_iota(jnp.int32, gscore.shape, 0)
    beaten = jnp.zeros(gscore.shape, jnp.int32)
    for g in range(N_GROUPS):
        row = gscore[g:g + 1]
        beaten = beaten + ((row > gscore) | ((row == gscore) & (g < ig))).astype(jnp.int32)
    gsel = beaten < TOPK_GROUP
    work = jnp.where(gsel, sb3, NEG).reshape(N_EXPERTS, n)
    ie = lax.broadcasted_iota(jnp.int32, work.shape, 0)
    chosen = jnp.zeros(work.shape, jnp.bool_)
    for _ in range(TOP_K):
        m = jnp.max(work, axis=0, keepdims=True)
        pick = ie == jnp.min(jnp.where(work == m, ie, N_EXPERTS), axis=0, keepdims=True)
        chosen = chosen | pick
        work = jnp.where(pick, low, work)
    wsel = jnp.where(chosen, s, 0.0)
    return wsel / jnp.sum(wsel, axis=0, keepdims=True) * ROUTED_SCALE


def _merge_kernel(oac_ref, oal_ref, ob_ref, ga_ref, gb_ref, xp_ref, xs_ref, mod_ref, g_ref,
                  wpa_ref, wpb_ref, wo_ref, wrh_ref, wrl_ref, br_ref, wgs_ref, wus_ref, wds_ref,
                  x1s_ref, hf_ref, gates_ref):
    i = pl.program_id(0)
    is_ctx = i < _CTX_TILES
    oa = jnp.where(is_ctx, oac_ref[...], oal_ref[...])
    x = jnp.where(is_ctx, xp_ref[...], xs_ref[...])
    m = mod_ref[0]
    sl = lambda k: m[:, k * D_MODEL:(k + 1) * D_MODEL]
    mix = (jax.nn.sigmoid(ga_ref[...].astype(F32)) * jnp.dot(oa, wpa_ref[...], preferred_element_type=F32)
           + jax.nn.sigmoid(gb_ref[...].astype(F32)) * jnp.dot(ob_ref[...], wpb_ref[...], preferred_element_type=F32))
    o = jnp.dot(mix.astype(BF16), wo_ref[...], preferred_element_type=F32)
    x1 = x + sl(2) * o
    h2 = _norm_mod(x1, g_ref[...], sl(3), sl(4))
    for c in range(_FOLD):
        hf_ref[pl.ds(c, _TM, stride=_FOLD), :] = h2[:, c * LANES:(c + 1) * LANES]
    h_hi = h2.astype(BF16)
    a = _silu(jnp.dot(h_hi, wgs_ref[...], preferred_element_type=F32)) * jnp.dot(
        h_hi, wus_ref[...], preferred_element_type=F32)
    x1s_ref[...] = x1 + sl(5) * jnp.dot(a.astype(BF16), wds_ref[...], preferred_element_type=F32)
    h_lo = (h2 - h_hi.astype(F32)).astype(BF16)
    logits = _nt_dot(wrh_ref[...], h_hi) + _nt_dot(wrh_ref[...], h_lo) + _nt_dot(wrl_ref[...], h_hi)
    s = jax.nn.sigmoid(logits)
    gates_ref[...] = _route(s, s + br_ref[...])


def _merge(oac, oal, ob, ga, gb, xp, xs, mod3, g2, wpa, wpb, wo, wrh, wrl, br, wgs, wus, wds):
    row = lambda i: (i, 0)
    const = lambda shape: pl.BlockSpec(shape, lambda i: (0,) * len(shape))
    return pl.pallas_call(
        _merge_kernel,
        grid=(N_TOK // _TM,),
        in_specs=[pl.BlockSpec((_TM, D_ATT), _ctx_map),
                  pl.BlockSpec((_TM, D_ATT), _lat_map),
                  pl.BlockSpec((_TM, D_RNN), row),
                  pl.BlockSpec((_TM, D_MODEL), row),
                  pl.BlockSpec((_TM, D_MODEL), row),
                  pl.BlockSpec((_TM, D_MODEL), _ctx_map),
                  pl.BlockSpec((_TM, D_MODEL), _lat_map),
                  pl.BlockSpec((1, 1, N_MOD * D_MODEL), lambda i: (_mod_row(i, _TM), 0, 0)),
                  const((1, D_MODEL)),
                  const((D_ATT, D_MODEL)), const((D_RNN, D_MODEL)), const((D_MODEL, D_MODEL)),
                  const((N_EXPERTS, D_MODEL)), const((N_EXPERTS, D_MODEL)), const((N_EXPERTS, 1)),
                  const((D_MODEL, D_EXPERT)), const((D_MODEL, D_EXPERT)), const((D_EXPERT, D_MODEL))],
        out_specs=[pl.BlockSpec((_TM, D_MODEL), row),
                   pl.BlockSpec((_TM * _FOLD, LANES), row),
                   pl.BlockSpec((N_EXPERTS, _TM), lambda i: (0, i))],
        out_shape=[jax.ShapeDtypeStruct((N_TOK, D_MODEL), F32),
                   jax.ShapeDtypeStruct((N_TOK * _FOLD, LANES), F32),
                   jax.ShapeDtypeStruct((N_EXPERTS, N_TOK), F32)],
        compiler_params=_params(("parallel",)),
        name="merge",
    )(oac, oal, ob, ga, gb, xp, xs, mod3, g2, wpa, wpb, wo, wrh, wrl, br, wgs, wus, wds)


_FOLD = D_MODEL // LANES
_NH = N_TOK // 2
_ID_BITS = 13
_DISP_BITS = 12
_VALID_BIT = _ID_BITS + _DISP_BITS + 1


def _dispatch_kernel(g_ref, src_ref, dst_ref, gl_ref, cnt_ref):
    g = g_ref[...]
    n = g.shape[1]
    lane = lax.broadcasted_iota(jnp.int32, g.shape, 1)
    sel = g > 0.0
    m = sel.astype(jnp.int32)
    csum = m
    s = 1
    while s < n:
        csum = csum + jnp.where(lane >= s, pltpu.roll(csum, s, 1), 0)
        s *= 2
    cnt_ref[0] = jnp.broadcast_to(csum[:, n - 1:n], (N_EXPERTS, LANES))
    disp = lane - (csum - m)
    packed = jnp.where(sel, (1 << _VALID_BIT) | (disp << _ID_BITS) | lane, 0)
    g = jnp.where(sel, g, 0.0)
    for b in range(_DISP_BITS):
        shift = n - (1 << b)
        inc = pltpu.roll(packed, shift, 1)
        inc_moves = ((inc >> (_ID_BITS + b)) & 1) == 1
        own_moves = ((packed >> (_ID_BITS + b)) & 1) == 1
        packed = jnp.where(inc_moves, inc, jnp.where(own_moves, 0, packed))
        g = jnp.where(inc_moves, pltpu.roll(g, shift, 1), jnp.where(own_moves, 0.0, g))
    valid = (packed >> _VALID_BIT) == 1
    tok = packed & ((1 << _ID_BITS) - 1)
    src_ref[0] = jnp.where(valid, tok, 0) * _FOLD
    dst_ref[0] = jnp.where(valid, tok, n) * _FOLD
    gl_ref[0] = g


def _dispatch(gates_t):
    halves = N_TOK // _NH
    blk = pl.BlockSpec((1, N_EXPERTS, _NH), lambda h: (h, 0, 0))
    lists = jax.ShapeDtypeStruct((halves, N_EXPERTS, _NH), jnp.int32)
    return pl.pallas_call(
        _dispatch_kernel,
        grid=(halves,),
        in_specs=[pl.BlockSpec((N_EXPERTS, _NH), lambda h: (0, h))],
        out_specs=[blk, blk, blk, pl.BlockSpec((1, N_EXPERTS, LANES), lambda h: (h, 0, 0))],
        out_shape=[lists, lists, jax.ShapeDtypeStruct((halves, N_EXPERTS, _NH), F32),
                   jax.ShapeDtypeStruct((halves, N_EXPERTS, LANES), jnp.int32)],
        compiler_params=_params(("parallel",)),
        name="dispatch",
    )(gates_t)


_R = 256
_GRP = 8


def _moe_kernel(src_ref, dst_ref, gate_ref, cnt_ref, hf_ref, wg_ref, wu_ref, wd_ref, out_hbm,
                acc_ref, xst_ref, yst_ref, sem):
    h = pl.program_id(0)
    e = pl.program_id(1)

    @pl.when(e == 0)
    def _():
        acc_ref[...] = jnp.zeros_like(acc_ref)

    @pl.when((h == 0) & (e == 0))
    def _():
        xst_ref[...] = jnp.zeros_like(xst_ref)

    cnt = cnt_ref[h * N_EXPERTS + e]
    wg = wg_ref[0].astype(BF16)
    wu = wu_ref[0].astype(BF16)
    wd = wd_ref[0].astype(BF16)

    def chunk(c, carry):
        base = c * _R
        groups = (jnp.minimum(cnt - base, _R) + _GRP - 1) // _GRP

        def row(ref, r):
            return ref.at[pl.ds(pl.multiple_of(r, _FOLD), _FOLD), :]

        def gather(gi, carry2):
            j0 = gi * _GRP
            p0 = base + j0
            for u in range(_GRP):
                row(xst_ref, (j0 + u) * _FOLD)[...] = row(hf_ref, src_ref[p0 + u])[...]
            return carry2

        lax.fori_loop(0, groups, gather, 0)
        x = jnp.concatenate([xst_ref[pl.ds(k, _R, stride=_FOLD), :].astype(BF16) for k in range(_FOLD)], axis=1)
        act = _silu(jnp.dot(x, wg, preferred_element_type=F32)) * jnp.dot(x, wu, preferred_element_type=F32)
        y = jnp.dot(act.astype(BF16), wd, preferred_element_type=F32)
        for k in range(_FOLD):
            yst_ref[pl.ds(k, _R, stride=_FOLD), :] = y[:, k * LANES:(k + 1) * LANES]

        def scatter(gi, carry2):
            j0 = gi * _GRP
            p0 = base + j0
            dsts = [row(acc_ref, dst_ref[p0 + u]) for u in range(_GRP)]
            news = [dsts[u][...] + gate_ref[p0 + u] * row(yst_ref, (j0 + u) * _FOLD)[...] for u in range(_GRP)]
            for u in range(_GRP):
                dsts[u][...] = news[u]
            return carry2

        lax.fori_loop(0, groups, scatter, 0)
        return carry

    lax.fori_loop(0, (cnt + _R - 1) // _R, chunk, 0)

    @pl.when(e == N_EXPERTS - 1)
    def _():
        rows = _NH * _FOLD
        cp = pltpu.make_async_copy(acc_ref.at[pl.ds(0, rows)],
                                   out_hbm.at[pl.ds(pl.multiple_of(h * rows, rows), rows)], sem)
        cp.start()
        cp.wait()


def _moe(src, dst, gl, counts, hfold, wg, wu, wd):
    halves = N_TOK // _NH
    smem = pl.BlockSpec((_NH,), lambda h, e: (h * N_EXPERTS + e,), memory_space=pltpu.SMEM)
    return pl.pallas_call(
        _moe_kernel,
        grid=(halves, N_EXPERTS),
        in_specs=[smem, smem, smem,
                  pl.BlockSpec(memory_space=pltpu.SMEM),
                  pl.BlockSpec((_NH * _FOLD, LANES), lambda h, e: (h, 0)),
                  pl.BlockSpec((1, D_MODEL, D_EXPERT), lambda h, e: (e, 0, 0)),
                  pl.BlockSpec((1, D_MODEL, D_EXPERT), lambda h, e: (e, 0, 0)),
                  pl.BlockSpec((1, D_EXPERT, D_MODEL), lambda h, e: (e, 0, 0))],
        out_specs=pl.BlockSpec(memory_space=pl.ANY),
        out_shape=jax.ShapeDtypeStruct((N_TOK * _FOLD, LANES), F32),
        scratch_shapes=[pltpu.VMEM(((_NH + 1) * _FOLD, LANES), F32),
                        pltpu.VMEM((_R * _FOLD, LANES), F32),
                        pltpu.VMEM((_R * _FOLD, LANES), F32),
                        pltpu.SemaphoreType.DMA],
        compiler_params=_params(("arbitrary", "arbitrary")),
        name="moe",
    )(src, dst, gl, counts, hfold, wg, wu, wd)


def _final_kernel(x_ref, r_ref, mod_ref, fg_ref, yp_ref, ys_ref):
    i = pl.program_id(0)
    routed = jnp.concatenate([r_ref[pl.ds(k, _TM, stride=_FOLD), :] for k in range(_FOLD)], axis=1)
    m = mod_ref[0]
    x2 = x_ref[...] + m[:, 5 * D_MODEL:6 * D_MODEL] * routed
    y = x2 * lax.rsqrt(jnp.mean(x2 * x2, axis=-1, keepdims=True) + EPS) * fg_ref[...]

    @pl.when(i < _CTX_TILES)
    def _():
        yp_ref[...] = y

    @pl.when(i >= _CTX_TILES)
    def _():
        ys_ref[...] = y


def _final(x1s, routed, mod3, fg):
    return pl.pallas_call(
        _final_kernel,
        grid=(N_TOK // _TM,),
        in_specs=[pl.BlockSpec((_TM, D_MODEL), lambda i: (i, 0)),
                  pl.BlockSpec((_TM * _FOLD, LANES), lambda i: (i, 0)),
                  pl.BlockSpec((1, 1, N_MOD * D_MODEL), lambda i: (_mod_row(i, _TM), 0, 0)),
                  pl.BlockSpec((1, D_MODEL), lambda i: (0, 0))],
        out_specs=[pl.BlockSpec((_TM, D_MODEL), _ctx_map),
                   pl.BlockSpec((_TM, D_MODEL), _lat_map)],
        out_shape=[jax.ShapeDtypeStruct((N_CTX, D_MODEL), F32),
                   jax.ShapeDtypeStruct((N_LAT, D_MODEL), F32)],
        compiler_params=_params(("arbitrary",)),
        name="final",
    )(x1s, routed, mod3, fg)


def kernel(x_prompt, x_sample, cache_k, cache_v, state_h_fwd, state_h_bwd, c, c_ctx, norm1_g, norm2_g, final_g, w_ada, b_ada, w_in, rpb, conv_w, conv_b, rg_wa, rg_ba, rg_wx, rg_bx, rg_lam, w_pa, w_pb, w_out, w_router, b_router, w_gate_e, w_up_e, w_down_e, w_gate_s, w_up_s, w_down_s):
    l = 0
    xp = x_prompt.reshape(N_CTX, D_MODEL)
    xs = x_sample.reshape(N_LAT, D_MODEL)

    cvecs = jnp.concatenate([c_ctx[None, :], c], axis=0)[:, :, None]
    mod = _adaln(cvecs, w_ada[l], b_ada[l][None, :])
    mod3 = mod.reshape(8, 1, N_MOD * D_MODEL)

    q, kc, vc, kl, vl, xr, yg, ga, gb = _inproj(xp, xs, mod3, norm1_g[l][None, :], w_in[l].astype(BF16))

    oac = _attn_ctx(q, kc, vc)
    ck = cache_k[:, l].reshape(DEC_BATCH, PAST_LEN, D_ATT)
    cv = cache_v[:, l].reshape(DEC_BATCH, PAST_LEN, D_ATT)
    oal = _attn_lat(q, kl, vl, ck, cv, _bias_tables(rpb[l]))

    w4 = jnp.concatenate([_block_diag(rg_wa[l, 0]), _block_diag(rg_wx[l, 0]),
                          _block_diag(rg_wa[l, 1]), _block_diag(rg_wx[l, 1])], axis=1).astype(BF16)
    b4 = jnp.concatenate([rg_ba[l, 0], rg_bx[l, 0], rg_ba[l, 1], rg_bx[l, 1]])[None, :]
    ob, hlf, hlb = _rglru(xr, yg, conv_w[l], conv_b[l][None, :], w4, b4, rg_lam[l],
                          state_h_fwd[:, l], state_h_bwd[:, l])

    wr_t = w_router[l].T
    wr_hi = wr_t.astype(BF16)
    wr_lo = (wr_t - wr_hi.astype(F32)).astype(BF16)
    x1s, hfold, gates_t = _merge(oac, oal, ob, ga, gb, xp, xs, mod3, norm2_g[l][None, :],
                                 w_pa[l].astype(BF16), w_pb[l].astype(BF16), w_out[l].astype(BF16),
                                 wr_hi, wr_lo, b_router[l][:, None],
                                 w_gate_s[l].astype(BF16), w_up_s[l].astype(BF16), w_down_s[l].astype(BF16))

    src, dst, gl, counts = _dispatch(gates_t)
    routed = _moe(src.reshape(-1), dst.reshape(-1), gl.reshape(-1), counts[:, :, 0].reshape(-1), hfold,
                  w_gate_e[l], w_up_e[l], w_down_e[l])
    yp, ys = _final(x1s, routed, mod3, final_g[None, :])

    return (yp.reshape(BATCH, SEQ, D_MODEL),
            ys.reshape(DEC_BATCH, DEC_SEQ, D_MODEL),
            kc.reshape(BATCH, 1, SEQ, N_HEADS_A, HEAD_DIM_A),
            vc.reshape(BATCH, 1, SEQ, N_HEADS_A, HEAD_DIM_A),
            hlf.reshape(BATCH, 1, D_RNN),
            hlb.reshape(BATCH, 1, D_RNN))
```

```python
import functools

import jax
import jax.numpy as jnp
from jax import lax
from jax.experimental import pallas as pl
from jax.experimental.pallas import tpu as pltpu

F32 = jnp.float32
BF16 = jnp.bfloat16

D_MODEL = 1024
BATCH = 16
SEQ = 256
DEC_BATCH = 2
DEC_SEQ = 1024
PAST_LEN = 512
GRID_W = 64
N_HEADS_A = 8
HEAD_DIM_A = 64
D_ATT = N_HEADS_A * HEAD_DIM_A
KH = 8
KW = 16
D_RNN = 512
N_RG_BLOCKS = 8
CONV_W = 4
RG_C = 8.0
N_EXPERTS = 64
TOP_K = 8
N_GROUPS = 8
GROUP_SIZE = N_EXPERTS // N_GROUPS
TOPK_GROUP = 4
D_EXPERT = 256
ROUTED_SCALE = 2.5
N_MOD = 6
EPS = 1e-6
NEG = -1e30
D_IN = 3 * D_ATT + 2 * D_RNN + 2 * D_MODEL

N_CTX = BATCH * SEQ
N_LAT = DEC_BATCH * DEC_SEQ
N_TOK = N_CTX + N_LAT
GRID_ROWS = DEC_SEQ // GRID_W

LANES = 128
VMEM_LIMIT = 56 * 1024 * 1024


def _params(sem, vmem=VMEM_LIMIT):
    return pltpu.CompilerParams(dimension_semantics=sem, vmem_limit_bytes=vmem)


def _mod_row(i, tile_rows):
    n_ctx_tiles = N_CTX // tile_rows
    return jnp.where(i < n_ctx_tiles, 0, 1 + (i - n_ctx_tiles) // (DEC_SEQ // tile_rows))


def _norm_mod(x, g, shift, scale):
    y = x * lax.rsqrt(jnp.mean(x * x, axis=-1, keepdims=True) + EPS)
    return (y * g) * (1.0 + scale) + shift


def _silu(x):
    return x * jax.nn.sigmoid(x)


def _adaln_kernel(c_ref, w_ref, b_ref, o_ref):
    w = w_ref[...]
    rows = []
    for r in range(3):
        s = _silu(c_ref[r])
        rows.append(jnp.sum(w * s, axis=0, keepdims=True))
    rows.append(jnp.zeros((5, w.shape[1]), F32))
    o_ref[...] = jnp.concatenate(rows, axis=0) + b_ref[...]


def _adaln(cvecs, w_ada, b_ada):
    tn = 512
    n = N_MOD * D_MODEL
    return pl.pallas_call(
        _adaln_kernel,
        grid=(n // tn,),
        in_specs=[pl.BlockSpec((3, D_MODEL, 1), lambda j: (0, 0, 0)),
                  pl.BlockSpec((D_MODEL, tn), lambda j: (0, j)),
                  pl.BlockSpec((1, tn), lambda j: (0, j))],
        out_specs=pl.BlockSpec((8, tn), lambda j: (0, j)),
        out_shape=jax.ShapeDtypeStruct((8, n), F32),
        compiler_params=_params(("parallel",)),
        name="adaln",
    )(cvecs, w_ada, b_ada)


_TM = 512
_CTX_TILES = N_CTX // _TM
_LAT_TILES = N_LAT // _TM


def _ctx_map(i):
    return (jnp.minimum(i, _CTX_TILES - 1), 0)


def _lat_map(i):
    return (jnp.maximum(i - _CTX_TILES, 0), 0)


def _inproj_kernel(xp_ref, xs_ref, mod_ref, g_ref, w_ref,
                   q_ref, kc_ref, vc_ref, kl_ref, vl_ref, xr_ref, yg_ref, ga_ref, gb_ref):
    i = pl.program_id(0)
    is_ctx = i < _CTX_TILES
    x = jnp.where(is_ctx, xp_ref[...], xs_ref[...])
    m = mod_ref[0]
    h = _norm_mod(x, g_ref[...], m[:, 0:D_MODEL], m[:, D_MODEL:2 * D_MODEL]).astype(BF16)

    def proj(a, b):
        return jnp.dot(h, w_ref[:, a:b], preferred_element_type=F32)

    q_ref[...] = proj(0, D_ATT).astype(BF16)
    k = proj(D_ATT, 2 * D_ATT)
    v = proj(2 * D_ATT, 3 * D_ATT)

    @pl.when(is_ctx)
    def _():
        kc_ref[...] = k
        vc_ref[...] = v

    @pl.when(jnp.logical_not(is_ctx))
    def _():
        kl_ref[...] = k
        vl_ref[...] = v

    o = 3 * D_ATT
    xr_ref[...] = proj(o, o + D_RNN)
    yg_ref[...] = proj(o + D_RNN, o + 2 * D_RNN).astype(BF16)
    o += 2 * D_RNN
    ga_ref[...] = proj(o, o + D_MODEL).astype(BF16)
    gb_ref[...] = proj(o + D_MODEL, o + 2 * D_MODEL).astype(BF16)


def _inproj(xp, xs, mod3, g1, w_in):
    row = lambda i: (i, 0)
    return pl.pallas_call(
        _inproj_kernel,
        grid=(N_TOK // _TM,),
        in_specs=[pl.BlockSpec((_TM, D_MODEL), _ctx_map),
                  pl.BlockSpec((_TM, D_MODEL), _lat_map),
                  pl.BlockSpec((1, 1, N_MOD * D_MODEL), lambda i: (_mod_row(i, _TM), 0, 0)),
                  pl.BlockSpec((1, D_MODEL), lambda i: (0, 0)),
                  pl.BlockSpec((D_MODEL, D_IN), lambda i: (0, 0))],
        out_specs=[pl.BlockSpec((_TM, D_ATT), row),
                   pl.BlockSpec((_TM, D_ATT), _ctx_map),
                   pl.BlockSpec((_TM, D_ATT), _ctx_map),
                   pl.BlockSpec((_TM, D_ATT), _lat_map),
                   pl.BlockSpec((_TM, D_ATT), _lat_map),
                   pl.BlockSpec((_TM, D_RNN), row),
                   pl.BlockSpec((_TM, D_RNN), row),
                   pl.BlockSpec((_TM, D_MODEL), row),
                   pl.BlockSpec((_TM, D_MODEL), row)],
        out_shape=[jax.ShapeDtypeStruct((N_TOK, D_ATT), BF16),
                   jax.ShapeDtypeStruct((N_CTX, D_ATT), F32),
                   jax.ShapeDtypeStruct((N_CTX, D_ATT), F32),
                   jax.ShapeDtypeStruct((N_LAT, D_ATT), F32),
                   jax.ShapeDtypeStruct((N_LAT, D_ATT), F32),
                   jax.ShapeDtypeStruct((N_TOK, D_RNN), F32),
                   jax.ShapeDtypeStruct((N_TOK, D_RNN), BF16),
                   jax.ShapeDtypeStruct((N_TOK, D_MODEL), BF16),
                   jax.ShapeDtypeStruct((N_TOK, D_MODEL), BF16)],
        compiler_params=_params(("arbitrary",)),
        name="inproj",
    )(xp, xs, mod3, g1, w_in)


def _pair_attention(qp, segs):
    scale = HEAD_DIM_A ** -0.5
    lane = lax.broadcasted_iota(jnp.int32, (1, LANES), 1)
    outs = []
    for a in range(2):
        sel = (lane >= HEAD_DIM_A) if a else (lane < HEAD_DIM_A)
        qm = jnp.where(sel, qp, jnp.zeros_like(qp))
        ss = []
        for kp, _, bias_fn in segs:
            s = lax.dot_general(qm, kp, (((1,), (1,)), ((), ())), preferred_element_type=F32) * scale
            if bias_fn is not None:
                s = s + bias_fn(a)
            ss.append(s)
        m = functools.reduce(jnp.maximum, [jnp.max(s, axis=-1, keepdims=True) for s in ss])
        es = [jnp.exp(s - m) for s in ss]
        inv = 1.0 / functools.reduce(jnp.add, [jnp.sum(e, axis=-1, keepdims=True) for e in es])
        o = functools.reduce(jnp.add, [
            jnp.dot((e * inv).astype(BF16), vp, preferred_element_type=F32)
            for e, (_, vp, _) in zip(es, segs)])
        outs.append(o)
    return jnp.where(lane < HEAD_DIM_A, outs[0], outs[1])


def _attn_ctx_kernel(q_ref, k_ref, v_ref, o_ref):
    for j in range(D_ATT // LANES):
        c = slice(j * LANES, (j + 1) * LANES)
        segs = [(k_ref[:, c].astype(BF16), v_ref[:, c].astype(BF16), None)]
        o_ref[:, c] = _pair_attention(q_ref[:, c], segs).astype(BF16)


def _attn_ctx(q_all, kc, vc):
    blk = pl.BlockSpec((SEQ, D_ATT), lambda b: (b, 0))
    return pl.pallas_call(
        _attn_ctx_kernel,
        grid=(BATCH,),
        in_specs=[blk, blk, blk],
        out_specs=blk,
        out_shape=jax.ShapeDtypeStruct((N_CTX, D_ATT), BF16),
        compiler_params=_params(("parallel",)),
        name="attn_ctx",
    )(q_all, kc, vc)


_QROWS = 4
_QB = _QROWS * GRID_W
_KROWS = 12
_KB = _KROWS * GRID_W


def _attn_lat_kernel(q_ref, k_ref, v_ref, ck_ref, cv_ref, t2_ref, o_ref):
    i = pl.program_id(1)
    ks = jnp.where(i < 2, 0, GRID_ROWS - _KROWS)
    kstart = pl.multiple_of(ks * GRID_W, 256)
    lane = lax.broadcasted_iota(jnp.int32, (1, LANES), 1)

    def bias_for_head(h):
        rows = []
        for rq in range(_QROWS):
            r = _QROWS * i + rq
            rs = jnp.clip(r - KH // 2, 0, GRID_ROWS - KH)
            tiles = []
            for t in range(_KROWS // 2):
                kr = ks + 2 * t
                d = jnp.clip(kr - r + (KH - 1), -1, 2 * KH - 2)
                tile = t2_ref[h, d + 1]
                v0 = ((kr >= rs) & (kr < rs + KH)).astype(jnp.int32)
                v1 = ((kr + 1 >= rs) & (kr + 1 < rs + KH)).astype(jnp.int32)
                valid = jnp.where(lane < GRID_W, v0, v1) > 0
                tiles.append(jnp.where(valid, tile, NEG))
            rows.append(jnp.concatenate(tiles, axis=1))
        return jnp.concatenate(rows, axis=0)

    for j in range(D_ATT // LANES):
        c = slice(j * LANES, (j + 1) * LANES)
        k_loc = k_ref[pl.ds(kstart, _KB), c].astype(BF16)
        v_loc = v_ref[pl.ds(kstart, _KB), c].astype(BF16)
        segs = [(k_loc, v_loc, lambda a, j=j: bias_for_head(2 * j + a)),
                (ck_ref[:, c].astype(BF16), cv_ref[:, c].astype(BF16), None)]
        o_ref[:, c] = _pair_attention(q_ref[:, c], segs).astype(BF16)


def _attn_lat(q_all, kl, vl, ck, cv, t2):
    qblocks = DEC_SEQ // _QB
    return pl.pallas_call(
        _attn_lat_kernel,
        grid=(DEC_BATCH, qblocks),
        in_specs=[pl.BlockSpec((_QB, D_ATT), lambda b, i: (N_CTX // _QB + b * qblocks + i, 0)),
                  pl.BlockSpec((DEC_SEQ, D_ATT), lambda b, i: (b, 0)),
                  pl.BlockSpec((DEC_SEQ, D_ATT), lambda b, i: (b, 0)),
                  pl.BlockSpec((None, PAST_LEN, D_ATT), lambda b, i: (b, 0, 0)),
                  pl.BlockSpec((None, PAST_LEN, D_ATT), lambda b, i: (b, 0, 0)),
                  pl.BlockSpec((N_HEADS_A, 2 * KH, GRID_W, LANES), lambda b, i: (0, 0, 0, 0))],
        out_specs=pl.BlockSpec((_QB, D_ATT), lambda b, i: (b * qblocks + i, 0)),
        out_shape=jax.ShapeDtypeStruct((N_LAT, D_ATT), BF16),
        compiler_params=_params(("parallel", "arbitrary")),
        name="attn_lat",
    )(q_all, kl, vl, ck, cv, t2)


def _bias_tables(rpb):
    w = jnp.arange(GRID_W)
    dcol = jnp.clip(w[None, :] - w[:, None], -(KW - 1), KW - 1) + (KW - 1)
    col_start = jnp.clip(w - KW // 2, 0, GRID_W - KW)
    col_in = (w[None, :] >= col_start[:, None]) & (w[None, :] < col_start[:, None] + KW)
    t = jnp.where(col_in[None, None], rpb[:, :, dcol], NEG)
    neg = jnp.full((N_HEADS_A, 1, GRID_W, GRID_W), NEG, F32)
    left = jnp.concatenate([neg, t], axis=1)
    right = jnp.concatenate([t, neg], axis=1)
    return jnp.concatenate([left, right], axis=-1)


_RG_ROWS = DEC_SEQ
_RG_CHUNK = 256
_RG_CTX_STEPS = N_CTX // _RG_ROWS


def _rg_block(nb, T, xr_ref, yg_ref, cw_ref, cb_ref, w4_ref, b4_ref, lam_ref, h0f, h0b,
              ob_ref, xc_ref, af_ref, uf_ref, ab_ref, ub_ref, hf_ref, hb_ref):
    R = nb * T
    xr = xr_ref[...]
    t_idx = lax.broadcasted_iota(jnp.int32, (R, 1), 0) & (T - 1)
    cw = cw_ref[...]
    xc = jnp.zeros((R, D_RNN), F32) + cb_ref[...]
    for j in range(CONV_W):
        off = j - 2
        if off == 0:
            tap = xr
        else:
            tap = pltpu.roll(xr, shift=(-off) % R, axis=0)
            tap = jnp.where((t_idx + off >= 0) & (t_idx + off < T), tap, 0.0)
        xc = xc + cw[j:j + 1, :] * tap
    xc_ref[...] = xc

    lam = lam_ref[...]
    z = -lam
    sp = jnp.maximum(z, 0.0) + jnp.log1p(jnp.exp(-jnp.abs(z)))

    def gate_chunk(c, carry):
        rows = pl.ds(pl.multiple_of(c * _RG_CHUNK, _RG_CHUNK), _RG_CHUNK)
        x = xc_ref[rows, :]
        g4 = jnp.dot(x.astype(BF16), w4_ref[...], preferred_element_type=F32) + b4_ref[...]
        for d, (a_ref, u_ref) in enumerate(((af_ref, uf_ref), (ab_ref, ub_ref))):
            r = jax.nn.sigmoid(g4[:, (2 * d) * D_RNN:(2 * d + 1) * D_RNN])
            ig = jax.nn.sigmoid(g4[:, (2 * d + 1) * D_RNN:(2 * d + 2) * D_RNN])
            log_a = -RG_C * r * sp[d:d + 1, :]
            a_ref[rows, :] = jnp.exp(log_a)
            u_ref[rows, :] = jnp.sqrt(jnp.maximum(1.0 - jnp.exp(2.0 * log_a), 0.0)) * (ig * x)
        return carry

    lax.fori_loop(0, R // _RG_CHUNK, gate_chunk, 0)

    def step(t, carry):
        hf, hb = carry
        tb = T - 1 - t
        nf, nbk = [], []
        for bi in range(nb):
            rf = pl.ds(bi * T + t, 1)
            rb = pl.ds(bi * T + tb, 1)
            f = af_ref[rf, :] * hf[bi] + uf_ref[rf, :]
            b = ab_ref[rb, :] * hb[bi] + ub_ref[rb, :]
            hf_ref[rf, :] = f
            hb_ref[rb, :] = b
            nf.append(f)
            nbk.append(b)
        return tuple(nf), tuple(nbk)

    hf, hb = lax.fori_loop(0, T, step, (tuple(h0f), tuple(h0b)))

    def out_chunk(c, carry):
        rows = pl.ds(pl.multiple_of(c * _RG_CHUNK, _RG_CHUNK), _RG_CHUNK)
        y = yg_ref[rows, :].astype(F32)
        ob_ref[rows, :] = ((hf_ref[rows, :] + hb_ref[rows, :]) * jax.nn.gelu(y)).astype(BF16)
        return carry

    lax.fori_loop(0, R // _RG_CHUNK, out_chunk, 0)
    return hf, hb


def _rglru_kernel(xr_ref, yg_ref, cw_ref, cb_ref, w4_ref, b4_ref, lam_ref, sf_ref, sb_ref,
                  ob_ref, hlf_ref, hlb_ref, *scratch):
    i = pl.program_id(0)
    shared = (xr_ref, yg_ref, cw_ref, cb_ref, w4_ref, b4_ref, lam_ref)
    nb_ctx = _RG_ROWS // SEQ

    @pl.when(i < _RG_CTX_STEPS)
    def _():
        zero = [jnp.zeros((1, D_RNN), F32)] * nb_ctx
        hf, hb = _rg_block(nb_ctx, SEQ, *shared, zero, zero, ob_ref, *scratch)
        for bi in range(nb_ctx):
            hlf_ref[0, bi:bi + 1, :] = hf[bi]
            hlb_ref[0, bi:bi + 1, :] = hb[bi]

    @pl.when(i >= _RG_CTX_STEPS)
    def _():
        b = i - _RG_CTX_STEPS
        _rg_block(1, DEC_SEQ, *shared, [sf_ref[pl.ds(b, 1), :]], [sb_ref[pl.ds(b, 1), :]],
                  ob_ref, *scratch)


def _rglru(xr, yg, conv_w, conv_b, w4, b4, lam, sf, sb):
    steps = N_TOK // _RG_ROWS
    nb_ctx = _RG_ROWS // SEQ
    const = lambda shape: pl.BlockSpec(shape, lambda i: (0,) * len(shape))
    hl_map = lambda i: (jnp.minimum(i, _RG_CTX_STEPS - 1), 0, 0)
    return pl.pallas_call(
        _rglru_kernel,
        grid=(steps,),
        in_specs=[pl.BlockSpec((_RG_ROWS, D_RNN), lambda i: (i, 0)),
                  pl.BlockSpec((_RG_ROWS, D_RNN), lambda i: (i, 0)),
                  const((CONV_W, D_RNN)), const((1, D_RNN)),
                  const((D_RNN, 4 * D_RNN)), const((1, 4 * D_RNN)), const((2, D_RNN)),
                  const((DEC_BATCH, D_RNN)), const((DEC_BATCH, D_RNN))],
        out_specs=[pl.BlockSpec((_RG_ROWS, D_RNN), lambda i: (i, 0)),
                   pl.BlockSpec((1, nb_ctx, D_RNN), hl_map),
                   pl.BlockSpec((1, nb_ctx, D_RNN), hl_map)],
        out_shape=[jax.ShapeDtypeStruct((N_TOK, D_RNN), BF16),
                   jax.ShapeDtypeStruct((_RG_CTX_STEPS, nb_ctx, D_RNN), F32),
                   jax.ShapeDtypeStruct((_RG_CTX_STEPS, nb_ctx, D_RNN), F32)],
        scratch_shapes=[pltpu.VMEM((_RG_ROWS, D_RNN), F32)] * 7,
        compiler_params=_params(("arbitrary",)),
        name="rglru",
    )(xr, yg, conv_w, conv_b, w4, b4, lam, sf, sb)


def _block_diag(w):
    n, d, _ = w.shape
    eye = jnp.eye(n, dtype=w.dtype)
    return jnp.einsum('nij,nm->nimj', w, eye).reshape(n * d, n * d)


def _nt_dot(a, b):
    return lax.dot_general(a, b, (((1,), (1,)), ((), ())), preferred_element_type=F32)


def _route(s, sb):
    n = s.shape[1]
    low = -3.0e38
    sb3 = sb.reshape(N_GROUPS, GROUP_SIZE, n)
    iw = lax.broadcasted_iota(jnp.int32, sb3.shape, 1)
    m1 = jnp.max(sb3, axis=1, keepdims=True)
    i1 = jnp.min(jnp.where(sb3 == m1, iw, GROUP_SIZE), axis=1, keepdims=True)
    m2 = jnp.max(jnp.where(iw == i1, low, sb3), axis=1, keepdims=True)
    gscore = m1 + m2
    ig = lax.broadcasted_iota(jnp.int32, gscore.shape, 0)
    beaten = jnp.zeros(gscore.shape, jnp.int32)
    for g in range(N_GROUPS):
        row = gscore[g:g + 1]
        beaten = beaten + ((row > gscore) | ((row == gscore) & (g < ig))).astype(jnp.int32)
    gsel = beaten < TOPK_GROUP
    work = jnp.where(gsel, sb3, NEG).reshape(N_EXPERTS, n)
    ie = lax.broadcasted_iota(jnp.int32, work.shape, 0)
    chosen = jnp.zeros(work.shape, jnp.bool_)
    for _ in range(TOP_K):
        m = jnp.max(work, axis=0, keepdims=True)
        pick = ie == jnp.min(jnp.where(work == m, ie, N_EXPERTS), axis=0, keepdims=True)
        chosen = chosen | pick
        work = jnp.where(pick, low, work)
    wsel = jnp.where(chosen, s, 0.0)
    return wsel / jnp.sum(wsel, axis=0, keepdims=True) * ROUTED_SCALE


def _merge_kernel(oac_ref, oal_ref, ob_ref, ga_ref, gb_ref, xp_ref, xs_ref, mod_ref, g_ref,
                  wpa_ref, wpb_ref, wo_ref, wrh_ref, wrl_ref, br_ref, wgs_ref, wus_ref, wds_ref,
                  x1s_ref, hf_ref, gates_ref):
    i = pl.program_id(0)
    is_ctx = i < _CTX_TILES
    oa = jnp.where(is_ctx, oac_ref[...], oal_ref[...])
    x = jnp.where(is_ctx, xp_ref[...], xs_ref[...])
    m = mod_ref[0]
    sl = lambda k: m[:, k * D_MODEL:(k + 1) * D_MODEL]
    mix = (jax.nn.sigmoid(ga_ref[...].astype(F32)) * jnp.dot(oa, wpa_ref[...], preferred_element_type=F32)
           + jax.nn.sigmoid(gb_ref[...].astype(F32)) * jnp.dot(ob_ref[...], wpb_ref[...], preferred_element_type=F32))
    o = jnp.dot(mix.astype(BF16), wo_ref[...], preferred_element_type=F32)
    x1 = x + sl(2) * o
    h2 = _norm_mod(x1, g_ref[...], sl(3), sl(4))
    for c in range(_FOLD):
        hf_ref[pl.ds(c, _TM, stride=_FOLD), :] = h2[:, c * LANES:(c + 1) * LANES]
    h_hi = h2.astype(BF16)
    a = _silu(jnp.dot(h_hi, wgs_ref[...], preferred_element_type=F32)) * jnp.dot(
        h_hi, wus_ref[...], preferred_element_type=F32)
    x1s_ref[...] = x1 + sl(5) * jnp.dot(a.astype(BF16), wds_ref[...], preferred_element_type=F32)
    h_lo = (h2 - h_hi.astype(F32)).astype(BF16)
    logits = _nt_dot(wrh_ref[...], h_hi) + _nt_dot(wrh_ref[...], h_lo) + _nt_dot(wrl_ref[...], h_hi)
    s = jax.nn.sigmoid(logits)
    gates_ref[...] = _route(s, s + br_ref[...])


def _merge(oac, oal, ob, ga, gb, xp, xs, mod3, g2, wpa, wpb, wo, wrh, wrl, br, wgs, wus, wds):
    row = lambda i: (i, 0)
    const = lambda shape: pl.BlockSpec(shape, lambda i: (0,) * len(shape))
    return pl.pallas_call(
        _merge_kernel,
        grid=(N_TOK // _TM,),
        in_specs=[pl.BlockSpec((_TM, D_ATT), _ctx_map),
                  pl.BlockSpec((_TM, D_ATT), _lat_map),
                  pl.BlockSpec((_TM, D_RNN), row),
                  pl.BlockSpec((_TM, D_MODEL), row),
                  pl.BlockSpec((_TM, D_MODEL), row),
                  pl.BlockSpec((_TM, D_MODEL), _ctx_map),
                  pl.BlockSpec((_TM, D_MODEL), _lat_map),
                  pl.BlockSpec((1, 1, N_MOD * D_MODEL), lambda i: (_mod_row(i, _TM), 0, 0)),
                  const((1, D_MODEL)),
                  const((D_ATT, D_MODEL)), const((D_RNN, D_MODEL)), const((D_MODEL, D_MODEL)),
                  const((N_EXPERTS, D_MODEL)), const((N_EXPERTS, D_MODEL)), const((N_EXPERTS, 1)),
                  const((D_MODEL, D_EXPERT)), const((D_MODEL, D_EXPERT)), const((D_EXPERT, D_MODEL))],
        out_specs=[pl.BlockSpec((_TM, D_MODEL), row),
                   pl.BlockSpec((_TM * _FOLD, LANES), row),
                   pl.BlockSpec((N_EXPERTS, _TM), lambda i: (0, i))],
        out_shape=[jax.ShapeDtypeStruct((N_TOK, D_MODEL), F32),
                   jax.ShapeDtypeStruct((N_TOK * _FOLD, LANES), F32),
                   jax.ShapeDtypeStruct((N_EXPERTS, N_TOK), F32)],
        compiler_params=_params(("parallel",)),
        name="merge",
    )(oac, oal, ob, ga, gb, xp, xs, mod3, g2, wpa, wpb, wo, wrh, wrl, br, wgs, wus, wds)


_FOLD = D_MODEL // LANES
_NH = N_TOK // 2
_R = 256
_SLOTS = 256
_ID_BITS = 13
_DISP_BITS = 12
_VALID_BIT = _ID_BITS + _DISP_BITS + 1


def _chunk_table(cnt):
    shape = (N_EXPERTS, _SLOTS)
    nch = jnp.broadcast_to((cnt + (_R - 1)) // _R, shape)
    erow = lax.broadcasted_iota(jnp.int32, shape, 0)
    q = lax.broadcasted_iota(jnp.int32, shape, 1)
    cum = nch
    s = 1
    while s < N_EXPERTS:
        cum = cum + jnp.where(erow >= s, pltpu.roll(cum, s, 0), 0)
        s *= 2
    total = cum[N_EXPERTS - 1:N_EXPERTS, :]
    used = q[:1] < total
    expert = jnp.minimum(jnp.sum((cum <= q).astype(jnp.int32), axis=0, keepdims=True), N_EXPERTS - 1)
    first = cum - nch
    start = jnp.max(jnp.where(first <= q, first, 0), axis=0, keepdims=True)
    chunk = jnp.where(used, q[:1] - start, 0)
    return jnp.concatenate([expert, chunk, used.astype(jnp.int32), total,
                            jnp.zeros((4, _SLOTS), jnp.int32)], axis=0)


def _dispatch_kernel(g_ref, src_ref, dst_ref, gl_ref, tbl_ref):
    g = g_ref[...]
    n = g.shape[1]
    lane = lax.broadcasted_iota(jnp.int32, g.shape, 1)
    sel = g > 0.0
    m = sel.astype(jnp.int32)
    csum = m
    s = 1
    while s < n:
        csum = csum + jnp.where(lane >= s, pltpu.roll(csum, s, 1), 0)
        s *= 2
    tbl_ref[0] = _chunk_table(csum[:, n - 1:n])
    disp = lane - (csum - m)
    packed = jnp.where(sel, (1 << _VALID_BIT) | (disp << _ID_BITS) | lane, 0)
    g = jnp.where(sel, g, 0.0)
    for b in range(_DISP_BITS):
        shift = n - (1 << b)
        inc = pltpu.roll(packed, shift, 1)
        inc_moves = ((inc >> (_ID_BITS + b)) & 1) == 1
        own_moves = ((packed >> (_ID_BITS + b)) & 1) == 1
        packed = jnp.where(inc_moves, inc, jnp.where(own_moves, 0, packed))
        g = jnp.where(inc_moves, pltpu.roll(g, shift, 1), jnp.where(own_moves, 0.0, g))
    valid = (packed >> _VALID_BIT) == 1
    tok = packed & ((1 << _ID_BITS) - 1)
    src_ref[0] = jnp.where(valid, tok, 0) * _FOLD
    dst_ref[0] = jnp.where(valid, tok, n) * _FOLD
    gl_ref[0] = g


def _dispatch(gates_t):
    halves = N_TOK // _NH
    blk = pl.BlockSpec((1, N_EXPERTS, _NH), lambda h: (h, 0, 0))
    lists = jax.ShapeDtypeStruct((halves, N_EXPERTS, _NH), jnp.int32)
    return pl.pallas_call(
        _dispatch_kernel,
        grid=(halves,),
        in_specs=[pl.BlockSpec((N_EXPERTS, _NH), lambda h: (0, h))],
        out_specs=[blk, blk, blk, pl.BlockSpec((1, 8, _SLOTS), lambda h: (h, 0, 0))],
        out_shape=[lists, lists, jax.ShapeDtypeStruct((halves, N_EXPERTS, _NH), F32),
                   jax.ShapeDtypeStruct((halves, 8, _SLOTS), jnp.int32)],
        compiler_params=_params(("parallel",)),
        name="dispatch",
    )(gates_t)


_GRP = 8
_MXU = 256
_KT = D_MODEL // _MXU
_UNF = 64
_TBL = 8 * _SLOTS
_STEPS = _NH * TOP_K // _R + N_EXPERTS + 2


def _tbl(tbl_ref, h, row, k):
    return tbl_ref[h * _TBL + row * _SLOTS + jnp.clip(k, 0, _SLOTS - 1)]


def _list_block(h, k, tbl_ref):
    per_list = _NH // _R
    return ((h * N_EXPERTS + _tbl(tbl_ref, h, 0, k)) * per_list + _tbl(tbl_ref, h, 1, k),)


def _moe_kernel(tbl_ref, src_ref, dst_ref, gate_ref, hf_ref, wg_ref, wu_ref, wd_ref, out_hbm,
                acc_ref, wgb_ref, wub_ref, wdb_ref, xg_ref, xa_ref, xb_ref, ya_ref, yb_ref, sem):
    h = pl.program_id(0)
    t = pl.program_id(1)
    total = _tbl(tbl_ref, h, 3, 0)

    @pl.when(t == 0)
    def _():
        for ref in (acc_ref, ya_ref, yb_ref):
            ref[...] = jnp.zeros_like(ref)

    @pl.when((h == 0) & (t == 0))
    def _():
        for ref in (xa_ref, xb_ref, wgb_ref, wub_ref, wdb_ref):
            ref[...] = jnp.zeros_like(ref)

    mm_used = (t >= 1) & (_tbl(tbl_ref, h, 2, t - 1) == 1)

    @pl.when(mm_used & (_tbl(tbl_ref, h, 1, t - 1) == 0))
    def _():
        wgb_ref[...] = wg_ref[0].astype(BF16)
        wub_ref[...] = wu_ref[0].astype(BF16)
        wdb_ref[...] = wd_ref[0].astype(BF16)

    def row(ref, r):
        return ref.at[pl.ds(pl.multiple_of(r, _FOLD), _FOLD), :]

    def stages(x_next, x_mm, y_mm, y_scatter):
        def gather_group(j0):
            for j in range(j0, j0 + _GRP):
                row(xg_ref, j * _FOLD)[...] = row(hf_ref, src_ref[j])[...]

        def scatter_group(j0):
            dsts = [row(acc_ref, dst_ref[j0 + u]) for u in range(_GRP)]
            news = [dsts[u][...] + row(y_scatter, (j0 + u) * _FOLD)[...] for u in range(_GRP)]
            for u in range(_GRP):
                dsts[u][...] = news[u]

        def unfold_block(r0):
            for k in range(_FOLD):
                x_next[r0:r0 + _UNF, k * LANES:(k + 1) * LANES] = xg_ref[
                    pl.ds(r0 * _FOLD + k, _UNF, stride=_FOLD), :].astype(BF16)

        moves = []
        for j0 in range(0, _R, _GRP):
            moves += [functools.partial(scatter_group, j0), functools.partial(gather_group, j0)]
            if (j0 + _GRP) % _UNF == 0:
                moves.append(functools.partial(unfold_block, j0 + _GRP - _UNF))
        n_pieces = 3 * _KT
        done = [0]

        def some_moves():
            done[0] += 1
            for f in moves[(done[0] - 1) * len(moves) // n_pieces:done[0] * len(moves) // n_pieces]:
                f()

        diag = (lax.broadcasted_iota(jnp.int32, (_R, _R), 0) == lax.broadcasted_iota(jnp.int32, (_R, _R), 1))
        gates = gate_ref[0] * mm_used.astype(F32)
        gcol = jnp.sum(jnp.where(diag, gates, 0.0), axis=1, keepdims=True)
        gt = up = None
        for kt in range(_KT):
            rows_k = slice(kt * _MXU, (kt + 1) * _MXU)
            xk = x_mm[:, rows_k]
            d = jnp.dot(xk, wgb_ref[rows_k, :], preferred_element_type=F32)
            gt = d if gt is None else gt + d
            some_moves()
            d = jnp.dot(xk, wub_ref[rows_k, :], preferred_element_type=F32)
            up = d if up is None else up + d
            some_moves()
        act = (_silu(gt) * up * gcol).astype(BF16)
        for nt in range(D_MODEL // _MXU):
            y = jnp.dot(act, wdb_ref[:, nt * _MXU:(nt + 1) * _MXU], preferred_element_type=F32)
            some_moves()
            for k in range(_MXU // LANES):
                y_mm[pl.ds(nt * _MXU // LANES + k, _R, stride=_FOLD), :] = y[:, k * LANES:(k + 1) * LANES]

    busy = t < total + 2

    @pl.when(busy & (t % 2 == 0))
    def _():
        stages(xa_ref, xb_ref, yb_ref, ya_ref)

    @pl.when(busy & (t % 2 == 1))
    def _():
        stages(xb_ref, xa_ref, ya_ref, yb_ref)

    @pl.when(t == _STEPS - 1)
    def _():
        rows = _NH * _FOLD
        cp = pltpu.make_async_copy(acc_ref.at[pl.ds(0, rows)],
                                   out_hbm.at[pl.ds(pl.multiple_of(h * rows, rows), rows)], sem)
        cp.start()
        cp.wait()


def _moe(tbl, src, dst, gl, hfold, wg, wu, wd):
    halves = N_TOK // _NH
    smem = lambda lag: pl.BlockSpec((_R,), lambda h, t, tbl_ref: _list_block(h, t - lag, tbl_ref),
                                    memory_space=pltpu.SMEM)
    wmap = lambda h, t, tbl_ref: (_tbl(tbl_ref, h, 0, t - 1), 0, 0)
    stage = pltpu.VMEM((_R * _FOLD, LANES), F32)
    packed = pltpu.VMEM((_R, D_MODEL), BF16)
    return pl.pallas_call(
        _moe_kernel,
        grid_spec=pltpu.PrefetchScalarGridSpec(
            num_scalar_prefetch=1,
            grid=(halves, _STEPS),
            in_specs=[smem(0), smem(2),
                      pl.BlockSpec((1, 1, _R), lambda h, t, tbl_ref: _list_block(h, t - 1, tbl_ref) + (0, 0)),
                      pl.BlockSpec((_NH * _FOLD, LANES), lambda h, t, tbl_ref: (h, 0)),
                      pl.BlockSpec((1, D_MODEL, D_EXPERT), wmap),
                      pl.BlockSpec((1, D_MODEL, D_EXPERT), wmap),
                      pl.BlockSpec((1, D_EXPERT, D_MODEL), wmap)],
            out_specs=pl.BlockSpec(memory_space=pl.ANY),
            scratch_shapes=[pltpu.VMEM(((_NH + 1) * _FOLD, LANES), F32),
                            pltpu.VMEM((D_MODEL, D_EXPERT), BF16),
                            pltpu.VMEM((D_MODEL, D_EXPERT), BF16),
                            pltpu.VMEM((D_EXPERT, D_MODEL), BF16),
                            stage, packed, packed, stage, stage,
                            pltpu.SemaphoreType.DMA]),
        out_shape=jax.ShapeDtypeStruct((N_TOK * _FOLD, LANES), F32),
        compiler_params=_params(("arbitrary", "arbitrary")),
        name="moe",
    )(tbl, src, dst, gl, hfold, wg, wu, wd)


def _final_kernel(x_ref, r_ref, mod_ref, fg_ref, yp_ref, ys_ref):
    i = pl.program_id(0)
    routed = jnp.concatenate([r_ref[pl.ds(k, _TM, stride=_FOLD), :] for k in range(_FOLD)], axis=1)
    m = mod_ref[0]
    x2 = x_ref[...] + m[:, 5 * D_MODEL:6 * D_MODEL] * routed
    y = x2 * lax.rsqrt(jnp.mean(x2 * x2, axis=-1, keepdims=True) + EPS) * fg_ref[...]

    @pl.when(i < _CTX_TILES)
    def _():
        yp_ref[...] = y

    @pl.when(i >= _CTX_TILES)
    def _():
        ys_ref[...] = y


def _final(x1s, routed, mod3, fg):
    return pl.pallas_call(
        _final_kernel,
        grid=(N_TOK // _TM,),
        in_specs=[pl.BlockSpec((_TM, D_MODEL), lambda i: (i, 0)),
                  pl.BlockSpec((_TM * _FOLD, LANES), lambda i: (i, 0)),
                  pl.BlockSpec((1, 1, N_MOD * D_MODEL), lambda i: (_mod_row(i, _TM), 0, 0)),
                  pl.BlockSpec((1, D_MODEL), lambda i: (0, 0))],
        out_specs=[pl.BlockSpec((_TM, D_MODEL), _ctx_map),
                   pl.BlockSpec((_TM, D_MODEL), _lat_map)],
        out_shape=[jax.ShapeDtypeStruct((N_CTX, D_MODEL), F32),
                   jax.ShapeDtypeStruct((N_LAT, D_MODEL), F32)],
        compiler_params=_params(("arbitrary",)),
        name="final",
    )(x1s, routed, mod3, fg)


def kernel(x_prompt, x_sample, cache_k, cache_v, state_h_fwd, state_h_bwd, c, c_ctx, norm1_g, norm2_g, final_g, w_ada, b_ada, w_in, rpb, conv_w, conv_b, rg_wa, rg_ba, rg_wx, rg_bx, rg_lam, w_pa, w_pb, w_out, w_router, b_router, w_gate_e, w_up_e, w_down_e, w_gate_s, w_up_s, w_down_s):
    l = 0
    xp = x_prompt.reshape(N_CTX, D_MODEL)
    xs = x_sample.reshape(N_LAT, D_MODEL)

    cvecs = jnp.concatenate([c_ctx[None, :], c], axis=0)[:, :, None]
    mod = _adaln(cvecs, w_ada[l], b_ada[l][None, :])
    mod3 = mod.reshape(8, 1, N_MOD * D_MODEL)

    q, kc, vc, kl, vl, xr, yg, ga, gb = _inproj(xp, xs, mod3, norm1_g[l][None, :], w_in[l].astype(BF16))

    oac = _attn_ctx(q, kc, vc)
    ck = cache_k[:, l].reshape(DEC_BATCH, PAST_LEN, D_ATT)
    cv = cache_v[:, l].reshape(DEC_BATCH, PAST_LEN, D_ATT)
    oal = _attn_lat(q, kl, vl, ck, cv, _bias_tables(rpb[l]))

    w4 = jnp.concatenate([_block_diag(rg_wa[l, 0]), _block_diag(rg_wx[l, 0]),
                          _block_diag(rg_wa[l, 1]), _block_diag(rg_wx[l, 1])], axis=1).astype(BF16)
    b4 = jnp.concatenate([rg_ba[l, 0], rg_bx[l, 0], rg_ba[l, 1], rg_bx[l, 1]])[None, :]
    ob, hlf, hlb = _rglru(xr, yg, conv_w[l], conv_b[l][None, :], w4, b4, rg_lam[l],
                          state_h_fwd[:, l], state_h_bwd[:, l])

    wr_t = w_router[l].T
    wr_hi = wr_t.astype(BF16)
    wr_lo = (wr_t - wr_hi.astype(F32)).astype(BF16)
    x1s, hfold, gates_t = _merge(oac, oal, ob, ga, gb, xp, xs, mod3, norm2_g[l][None, :],
                                 w_pa[l].astype(BF16), w_pb[l].astype(BF16), w_out[l].astype(BF16),
                                 wr_hi, wr_lo, b_router[l][:, None],
                                 w_gate_s[l].astype(BF16), w_up_s[l].astype(BF16), w_down_s[l].astype(BF16))

    src, dst, gl, tbl = _dispatch(gates_t)
    routed = _moe(tbl.reshape(-1), src.reshape(-1), dst.reshape(-1), gl.reshape(-1, 1, _R), hfold,
                  w_gate_e[l], w_up_e[l], w_down_e[l])
    yp, ys = _final(x1s, routed, mod3, final_g[None, :])

    return (yp.reshape(BATCH, SEQ, D_MODEL),
            ys.reshape(DEC_BATCH, DEC_SEQ, D_MODEL),
            kc.reshape(BATCH, 1, SEQ, N_HEADS_A, HEAD_DIM_A),
            vc.reshape(BATCH, 1, SEQ, N_HEADS_A, HEAD_DIM_A),
            hlf.reshape(BATCH, 1, D_RNN),
            hlb.reshape(BATCH, 1, D_RNN))
```

```python
import functools

import jax
import jax.numpy as jnp
from jax import lax
from jax.experimental import pallas as pl
from jax.experimental.pallas import tpu as pltpu

F32 = jnp.float32
BF16 = jnp.bfloat16

D_MODEL = 1024
BATCH = 16
SEQ = 256
DEC_BATCH = 2
DEC_SEQ = 1024
PAST_LEN = 512
GRID_W = 64
N_HEADS_A = 8
HEAD_DIM_A = 64
D_ATT = N_HEADS_A * HEAD_DIM_A
KH = 8
KW = 16
D_RNN = 512
N_RG_BLOCKS = 8
CONV_W = 4
RG_C = 8.0
N_EXPERTS = 64
TOP_K = 8
N_GROUPS = 8
GROUP_SIZE = N_EXPERTS // N_GROUPS
TOPK_GROUP = 4
D_EXPERT = 256
ROUTED_SCALE = 2.5
N_MOD = 6
EPS = 1e-6
NEG = -1e30
D_IN = 3 * D_ATT + 2 * D_RNN + 2 * D_MODEL

N_CTX = BATCH * SEQ
N_LAT = DEC_BATCH * DEC_SEQ
N_TOK = N_CTX + N_LAT
GRID_ROWS = DEC_SEQ // GRID_W

LANES = 128
VMEM_LIMIT = 56 * 1024 * 1024


def _params(sem, vmem=VMEM_LIMIT):
    return pltpu.CompilerParams(dimension_semantics=sem, vmem_limit_bytes=vmem)


def _mod_row(i, tile_rows):
    n_ctx_tiles = N_CTX // tile_rows
    return jnp.where(i < n_ctx_tiles, 0, 1 + (i - n_ctx_tiles) // (DEC_SEQ // tile_rows))


def _norm_mod(x, g, shift, scale):
    y = x * lax.rsqrt(jnp.mean(x * x, axis=-1, keepdims=True) + EPS)
    return (y * g) * (1.0 + scale) + shift


def _silu(x):
    return x * jax.nn.sigmoid(x)


def _adaln_kernel(c_ref, w_ref, b_ref, o_ref):
    w = w_ref[...]
    rows = []
    for r in range(3):
        s = _silu(c_ref[r])
        rows.append(jnp.sum(w * s, axis=0, keepdims=True))
    rows.append(jnp.zeros((5, w.shape[1]), F32))
    o_ref[...] = jnp.concatenate(rows, axis=0) + b_ref[...]


def _adaln(cvecs, w_ada, b_ada):
    tn = 512
    n = N_MOD * D_MODEL
    return pl.pallas_call(
        _adaln_kernel,
        grid=(n // tn,),
        in_specs=[pl.BlockSpec((3, D_MODEL, 1), lambda j: (0, 0, 0)),
                  pl.BlockSpec((D_MODEL, tn), lambda j: (0, j)),
                  pl.BlockSpec((1, tn), lambda j: (0, j))],
        out_specs=pl.BlockSpec((8, tn), lambda j: (0, j)),
        out_shape=jax.ShapeDtypeStruct((8, n), F32),
        compiler_params=_params(("parallel",)),
        name="adaln",
    )(cvecs, w_ada, b_ada)


_TM = 512
_CTX_TILES = N_CTX // _TM
_LAT_TILES = N_LAT // _TM


def _ctx_map(i):
    return (jnp.minimum(i, _CTX_TILES - 1), 0)


def _lat_map(i):
    return (jnp.maximum(i - _CTX_TILES, 0), 0)


def _inproj_kernel(xp_ref, xs_ref, mod_ref, g_ref, w_ref,
                   q_ref, kc_ref, vc_ref, kl_ref, vl_ref, xr_ref, yg_ref, ga_ref, gb_ref):
    i = pl.program_id(0)
    is_ctx = i < _CTX_TILES
    x = jnp.where(is_ctx, xp_ref[...], xs_ref[...])
    m = mod_ref[0]
    h = _norm_mod(x, g_ref[...], m[:, 0:D_MODEL], m[:, D_MODEL:2 * D_MODEL]).astype(BF16)

    def proj(a, b):
        return jnp.dot(h, w_ref[:, a:b], preferred_element_type=F32)

    q_ref[...] = proj(0, D_ATT).astype(BF16)
    k = proj(D_ATT, 2 * D_ATT)
    v = proj(2 * D_ATT, 3 * D_ATT)

    @pl.when(is_ctx)
    def _():
        kc_ref[...] = k
        vc_ref[...] = v

    @pl.when(jnp.logical_not(is_ctx))
    def _():
        kl_ref[...] = k
        vl_ref[...] = v

    o = 3 * D_ATT
    xr_ref[...] = proj(o, o + D_RNN)
    yg_ref[...] = proj(o + D_RNN, o + 2 * D_RNN).astype(BF16)
    o += 2 * D_RNN
    ga_ref[...] = proj(o, o + D_MODEL).astype(BF16)
    gb_ref[...] = proj(o + D_MODEL, o + 2 * D_MODEL).astype(BF16)


def _inproj(xp, xs, mod3, g1, w_in):
    row = lambda i: (i, 0)
    return pl.pallas_call(
        _inproj_kernel,
        grid=(N_TOK // _TM,),
        in_specs=[pl.BlockSpec((_TM, D_MODEL), _ctx_map),
                  pl.BlockSpec((_TM, D_MODEL), _lat_map),
                  pl.BlockSpec((1, 1, N_MOD * D_MODEL), lambda i: (_mod_row(i, _TM), 0, 0)),
                  pl.BlockSpec((1, D_MODEL), lambda i: (0, 0)),
                  pl.BlockSpec((D_MODEL, D_IN), lambda i: (0, 0))],
        out_specs=[pl.BlockSpec((_TM, D_ATT), row),
                   pl.BlockSpec((_TM, D_ATT), _ctx_map),
                   pl.BlockSpec((_TM, D_ATT), _ctx_map),
                   pl.BlockSpec((_TM, D_ATT), _lat_map),
                   pl.BlockSpec((_TM, D_ATT), _lat_map),
                   pl.BlockSpec((_TM, D_RNN), row),
                   pl.BlockSpec((_TM, D_RNN), row),
                   pl.BlockSpec((_TM, D_MODEL), row),
                   pl.BlockSpec((_TM, D_MODEL), row)],
        out_shape=[jax.ShapeDtypeStruct((N_TOK, D_ATT), BF16),
                   jax.ShapeDtypeStruct((N_CTX, D_ATT), F32),
                   jax.ShapeDtypeStruct((N_CTX, D_ATT), F32),
                   jax.ShapeDtypeStruct((N_LAT, D_ATT), F32),
                   jax.ShapeDtypeStruct((N_LAT, D_ATT), F32),
                   jax.ShapeDtypeStruct((N_TOK, D_RNN), F32),
                   jax.ShapeDtypeStruct((N_TOK, D_RNN), BF16),
                   jax.ShapeDtypeStruct((N_TOK, D_MODEL), BF16),
                   jax.ShapeDtypeStruct((N_TOK, D_MODEL), BF16)],
        compiler_params=_params(("arbitrary",)),
        name="inproj",
    )(xp, xs, mod3, g1, w_in)


def _pair_attention(qp, segs):
    scale = HEAD_DIM_A ** -0.5
    lane = lax.broadcasted_iota(jnp.int32, (1, LANES), 1)
    outs = []
    for a in range(2):
        sel = (lane >= HEAD_DIM_A) if a else (lane < HEAD_DIM_A)
        qm = jnp.where(sel, qp, jnp.zeros_like(qp))
        ss = []
        for kp, _, bias_fn in segs:
            s = lax.dot_general(qm, kp, (((1,), (1,)), ((), ())), preferred_element_type=F32) * scale
            if bias_fn is not None:
                s = s + bias_fn(a)
            ss.append(s)
        m = functools.reduce(jnp.maximum, [jnp.max(s, axis=-1, keepdims=True) for s in ss])
        es = [jnp.exp(s - m) for s in ss]
        inv = 1.0 / functools.reduce(jnp.add, [jnp.sum(e, axis=-1, keepdims=True) for e in es])
        o = functools.reduce(jnp.add, [
            jnp.dot((e * inv).astype(BF16), vp, preferred_element_type=F32)
            for e, (_, vp, _) in zip(es, segs)])
        outs.append(o)
    return jnp.where(lane < HEAD_DIM_A, outs[0], outs[1])


def _attn_ctx_kernel(q_ref, k_ref, v_ref, o_ref):
    for j in range(D_ATT // LANES):
        c = slice(j * LANES, (j + 1) * LANES)
        segs = [(k_ref[:, c].astype(BF16), v_ref[:, c].astype(BF16), None)]
        o_ref[:, c] = _pair_attention(q_ref[:, c], segs).astype(BF16)


def _attn_ctx(q_all, kc, vc):
    blk = pl.BlockSpec((SEQ, D_ATT), lambda b: (b, 0))
    return pl.pallas_call(
        _attn_ctx_kernel,
        grid=(BATCH,),
        in_specs=[blk, blk, blk],
        out_specs=blk,
        out_shape=jax.ShapeDtypeStruct((N_CTX, D_ATT), BF16),
        compiler_params=_params(("parallel",)),
        name="attn_ctx",
    )(q_all, kc, vc)


_QROWS = 4
_QB = _QROWS * GRID_W
_KROWS = 12
_KB = _KROWS * GRID_W


def _attn_lat_kernel(q_ref, k_ref, v_ref, ck_ref, cv_ref, t2_ref, o_ref):
    i = pl.program_id(1)
    ks = jnp.where(i < 2, 0, GRID_ROWS - _KROWS)
    kstart = pl.multiple_of(ks * GRID_W, 256)
    lane = lax.broadcasted_iota(jnp.int32, (1, LANES), 1)

    def bias_for_head(h):
        rows = []
        for rq in range(_QROWS):
            r = _QROWS * i + rq
            rs = jnp.clip(r - KH // 2, 0, GRID_ROWS - KH)
            tiles = []
            for t in range(_KROWS // 2):
                kr = ks + 2 * t
                d = jnp.clip(kr - r + (KH - 1), -1, 2 * KH - 2)
                tile = t2_ref[h, d + 1]
                v0 = ((kr >= rs) & (kr < rs + KH)).astype(jnp.int32)
                v1 = ((kr + 1 >= rs) & (kr + 1 < rs + KH)).astype(jnp.int32)
                valid = jnp.where(lane < GRID_W, v0, v1) > 0
                tiles.append(jnp.where(valid, tile, NEG))
            rows.append(jnp.concatenate(tiles, axis=1))
        return jnp.concatenate(rows, axis=0)

    for j in range(D_ATT // LANES):
        c = slice(j * LANES, (j + 1) * LANES)
        k_loc = k_ref[pl.ds(kstart, _KB), c].astype(BF16)
        v_loc = v_ref[pl.ds(kstart, _KB), c].astype(BF16)
        segs = [(k_loc, v_loc, lambda a, j=j: bias_for_head(2 * j + a)),
                (ck_ref[:, c].astype(BF16), cv_ref[:, c].astype(BF16), None)]
        o_ref[:, c] = _pair_attention(q_ref[:, c], segs).astype(BF16)


def _attn_lat(q_all, kl, vl, ck, cv, t2):
    qblocks = DEC_SEQ // _QB
    return pl.pallas_call(
        _attn_lat_kernel,
        grid=(DEC_BATCH, qblocks),
        in_specs=[pl.BlockSpec((_QB, D_ATT), lambda b, i: (N_CTX // _QB + b * qblocks + i, 0)),
                  pl.BlockSpec((DEC_SEQ, D_ATT), lambda b, i: (b, 0)),
                  pl.BlockSpec((DEC_SEQ, D_ATT), lambda b, i: (b, 0)),
                  pl.BlockSpec((None, PAST_LEN, D_ATT), lambda b, i: (b, 0, 0)),
                  pl.BlockSpec((None, PAST_LEN, D_ATT), lambda b, i: (b, 0, 0)),
                  pl.BlockSpec((N_HEADS_A, 2 * KH, GRID_W, LANES), lambda b, i: (0, 0, 0, 0))],
        out_specs=pl.BlockSpec((_QB, D_ATT), lambda b, i: (b * qblocks + i, 0)),
        out_shape=jax.ShapeDtypeStruct((N_LAT, D_ATT), BF16),
        compiler_params=_params(("parallel", "arbitrary")),
        name="attn_lat",
    )(q_all, kl, vl, ck, cv, t2)


def _bias_tables(rpb):
    w = jnp.arange(GRID_W)
    dcol = jnp.clip(w[None, :] - w[:, None], -(KW - 1), KW - 1) + (KW - 1)
    col_start = jnp.clip(w - KW // 2, 0, GRID_W - KW)
    col_in = (w[None, :] >= col_start[:, None]) & (w[None, :] < col_start[:, None] + KW)
    t = jnp.where(col_in[None, None], rpb[:, :, dcol], NEG)
    neg = jnp.full((N_HEADS_A, 1, GRID_W, GRID_W), NEG, F32)
    left = jnp.concatenate([neg, t], axis=1)
    right = jnp.concatenate([t, neg], axis=1)
    return jnp.concatenate([left, right], axis=-1)


_RG_ROWS = DEC_SEQ
_RG_CHUNK = 256
_RG_CTX_STEPS = N_CTX // _RG_ROWS


def _rg_block(nb, T, xr_ref, yg_ref, cw_ref, cb_ref, w4_ref, b4_ref, lam_ref, h0f, h0b,
              ob_ref, xc_ref, af_ref, uf_ref, ab_ref, ub_ref, hf_ref, hb_ref):
    R = nb * T
    xr = xr_ref[...]
    t_idx = lax.broadcasted_iota(jnp.int32, (R, 1), 0) & (T - 1)
    cw = cw_ref[...]
    xc = jnp.zeros((R, D_RNN), F32) + cb_ref[...]
    for j in range(CONV_W):
        off = j - 2
        if off == 0:
            tap = xr
        else:
            tap = pltpu.roll(xr, shift=(-off) % R, axis=0)
            tap = jnp.where((t_idx + off >= 0) & (t_idx + off < T), tap, 0.0)
        xc = xc + cw[j:j + 1, :] * tap
    xc_ref[...] = xc

    lam = lam_ref[...]
    z = -lam
    sp = jnp.maximum(z, 0.0) + jnp.log1p(jnp.exp(-jnp.abs(z)))

    def gate_chunk(c, carry):
        rows = pl.ds(pl.multiple_of(c * _RG_CHUNK, _RG_CHUNK), _RG_CHUNK)
        x = xc_ref[rows, :]
        g4 = jnp.dot(x.astype(BF16), w4_ref[...], preferred_element_type=F32) + b4_ref[...]
        for d, (a_ref, u_ref) in enumerate(((af_ref, uf_ref), (ab_ref, ub_ref))):
            r = jax.nn.sigmoid(g4[:, (2 * d) * D_RNN:(2 * d + 1) * D_RNN])
            ig = jax.nn.sigmoid(g4[:, (2 * d + 1) * D_RNN:(2 * d + 2) * D_RNN])
            log_a = -RG_C * r * sp[d:d + 1, :]
            a_ref[rows, :] = jnp.exp(log_a)
            u_ref[rows, :] = jnp.sqrt(jnp.maximum(1.0 - jnp.exp(2.0 * log_a), 0.0)) * (ig * x)
        return carry

    lax.fori_loop(0, R // _RG_CHUNK, gate_chunk, 0)

    def step(t, carry):
        hf, hb = carry
        tb = T - 1 - t
        nf, nbk = [], []
        for bi in range(nb):
            rf = pl.ds(bi * T + t, 1)
            rb = pl.ds(bi * T + tb, 1)
            f = af_ref[rf, :] * hf[bi] + uf_ref[rf, :]
            b = ab_ref[rb, :] * hb[bi] + ub_ref[rb, :]
            hf_ref[rf, :] = f
            hb_ref[rb, :] = b
            nf.append(f)
            nbk.append(b)
        return tuple(nf), tuple(nbk)

    hf, hb = lax.fori_loop(0, T, step, (tuple(h0f), tuple(h0b)))

    def out_chunk(c, carry):
        rows = pl.ds(pl.multiple_of(c * _RG_CHUNK, _RG_CHUNK), _RG_CHUNK)
        y = yg_ref[rows, :].astype(F32)
        ob_ref[rows, :] = ((hf_ref[rows, :] + hb_ref[rows, :]) * jax.nn.gelu(y)).astype(BF16)
        return carry

    lax.fori_loop(0, R // _RG_CHUNK, out_chunk, 0)
    return hf, hb


def _rglru_kernel(xr_ref, yg_ref, cw_ref, cb_ref, w4_ref, b4_ref, lam_ref, sf_ref, sb_ref,
                  ob_ref, hlf_ref, hlb_ref, *scratch):
    i = pl.program_id(0)
    shared = (xr_ref, yg_ref, cw_ref, cb_ref, w4_ref, b4_ref, lam_ref)
    nb_ctx = _RG_ROWS // SEQ

    @pl.when(i < _RG_CTX_STEPS)
    def _():
        zero = [jnp.zeros((1, D_RNN), F32)] * nb_ctx
        hf, hb = _rg_block(nb_ctx, SEQ, *shared, zero, zero, ob_ref, *scratch)
        for bi in range(nb_ctx):
            hlf_ref[0, bi:bi + 1, :] = hf[bi]
            hlb_ref[0, bi:bi + 1, :] = hb[bi]

    @pl.when(i >= _RG_CTX_STEPS)
    def _():
        b = i - _RG_CTX_STEPS
        _rg_block(1, DEC_SEQ, *shared, [sf_ref[pl.ds(b, 1), :]], [sb_ref[pl.ds(b, 1), :]],
                  ob_ref, *scratch)


def _rglru(xr, yg, conv_w, conv_b, w4, b4, lam, sf, sb):
    steps = N_TOK // _RG_ROWS
    nb_ctx = _RG_ROWS // SEQ
    const = lambda shape: pl.BlockSpec(shape, lambda i: (0,) * len(shape))
    hl_map = lambda i: (jnp.minimum(i, _RG_CTX_STEPS - 1), 0, 0)
    return pl.pallas_call(
        _rglru_kernel,
        grid=(steps,),
        in_specs=[pl.BlockSpec((_RG_ROWS, D_RNN), lambda i: (i, 0)),
                  pl.BlockSpec((_RG_ROWS, D_RNN), lambda i: (i, 0)),
                  const((CONV_W, D_RNN)), const((1, D_RNN)),
                  const((D_RNN, 4 * D_RNN)), const((1, 4 * D_RNN)), const((2, D_RNN)),
                  const((DEC_BATCH, D_RNN)), const((DEC_BATCH, D_RNN))],
        out_specs=[pl.BlockSpec((_RG_ROWS, D_RNN), lambda i: (i, 0)),
                   pl.BlockSpec((1, nb_ctx, D_RNN), hl_map),
                   pl.BlockSpec((1, nb_ctx, D_RNN), hl_map)],
        out_shape=[jax.ShapeDtypeStruct((N_TOK, D_RNN), BF16),
                   jax.ShapeDtypeStruct((_RG_CTX_STEPS, nb_ctx, D_RNN), F32),
                   jax.ShapeDtypeStruct((_RG_CTX_STEPS, nb_ctx, D_RNN), F32)],
        scratch_shapes=[pltpu.VMEM((_RG_ROWS, D_RNN), F32)] * 7,
        compiler_params=_params(("arbitrary",)),
        name="rglru",
    )(xr, yg, conv_w, conv_b, w4, b4, lam, sf, sb)


def _block_diag(w):
    n, d, _ = w.shape
    eye = jnp.eye(n, dtype=w.dtype)
    return jnp.einsum('nij,nm->nimj', w, eye).reshape(n * d, n * d)


def _nt_dot(a, b):
    return lax.dot_general(a, b, (((1,), (1,)), ((), ())), preferred_element_type=F32)


def _route(s, sb):
    n = s.shape[1]
    low = -3.0e38
    sb3 = sb.reshape(N_GROUPS, GROUP_SIZE, n)
    iw = lax.broadcasted_iota(jnp.int32, sb3.shape, 1)
    m1 = jnp.max(sb3, axis=1, keepdims=True)
    i1 = jnp.min(jnp.where(sb3 == m1, iw, GROUP_SIZE), axis=1, keepdims=True)
    m2 = jnp.max(jnp.where(iw == i1, low, sb3), axis=1, keepdims=True)
    gscore = m1 + m2
    ig = lax.broadcasted_iota(jnp.int32, gscore.shape, 0)
    beaten = jnp.zeros(gscore.shape, jnp.int32)
    for g in range(N_GROUPS):
        row = gscore[g:g + 1]
        beaten = beaten + ((row > gscore) | ((row == gscore) & (g < ig))).astype(jnp.int32)
    gsel = beaten < TOPK_GROUP
    work = jnp.where(gsel, sb3, NEG).reshape(N_EXPERTS, n)
    ie = lax.broadcasted_iota(jnp.int32, work.shape, 0)
    chosen = jnp.zeros(work.shape, jnp.bool_)
    for _ in range(TOP_K):
        m = jnp.max(work, axis=0, keepdims=True)
        pick = ie == jnp.min(jnp.where(work == m, ie, N_EXPERTS), axis=0, keepdims=True)
        chosen = chosen | pick
        work = jnp.where(pick, low, work)
    wsel = jnp.where(chosen, s, 0.0)
    return wsel / jnp.sum(wsel, axis=0, keepdims=True) * ROUTED_SCALE


def _merge_kernel(oac_ref, oal_ref, ob_ref, ga_ref, gb_ref, xp_ref, xs_ref, mod_ref, g_ref,
                  wpa_ref, wpb_ref, wo_ref, wrh_ref, wrl_ref, br_ref, wgs_ref, wus_ref, wds_ref,
                  x1s_ref, hf_ref, gates_ref):
    i = pl.program_id(0)
    is_ctx = i < _CTX_TILES
    oa = jnp.where(is_ctx, oac_ref[...], oal_ref[...])
    x = jnp.where(is_ctx, xp_ref[...], xs_ref[...])
    m = mod_ref[0]
    sl = lambda k: m[:, k * D_MODEL:(k + 1) * D_MODEL]
    mix = (jax.nn.sigmoid(ga_ref[...].astype(F32)) * jnp.dot(oa, wpa_ref[...], preferred_element_type=F32)
           + jax.nn.sigmoid(gb_ref[...].astype(F32)) * jnp.dot(ob_ref[...], wpb_ref[...], preferred_element_type=F32))
    o = jnp.dot(mix.astype(BF16), wo_ref[...], preferred_element_type=F32)
    x1 = x + sl(2) * o
    h2 = _norm_mod(x1, g_ref[...], sl(3), sl(4))
    for c in range(_FOLD):
        hf_ref[pl.ds(c, _TM, stride=_FOLD), :] = h2[:, c * LANES:(c + 1) * LANES]
    h_hi = h2.astype(BF16)
    a = _silu(jnp.dot(h_hi, wgs_ref[...], preferred_element_type=F32)) * jnp.dot(
        h_hi, wus_ref[...], preferred_element_type=F32)
    x1s_ref[...] = x1 + sl(5) * jnp.dot(a.astype(BF16), wds_ref[...], preferred_element_type=F32)
    h_lo = (h2 - h_hi.astype(F32)).astype(BF16)
    logits = _nt_dot(wrh_ref[...], h_hi) + _nt_dot(wrh_ref[...], h_lo) + _nt_dot(wrl_ref[...], h_hi)
    s = jax.nn.sigmoid(logits)
    gates_ref[...] = _route(s, s + br_ref[...])


def _merge(oac, oal, ob, ga, gb, xp, xs, mod3, g2, wpa, wpb, wo, wrh, wrl, br, wgs, wus, wds):
    row = lambda i: (i, 0)
    const = lambda shape: pl.BlockSpec(shape, lambda i: (0,) * len(shape))
    return pl.pallas_call(
        _merge_kernel,
        grid=(N_TOK // _TM,),
        in_specs=[pl.BlockSpec((_TM, D_ATT), _ctx_map),
                  pl.BlockSpec((_TM, D_ATT), _lat_map),
                  pl.BlockSpec((_TM, D_RNN), row),
                  pl.BlockSpec((_TM, D_MODEL), row),
                  pl.BlockSpec((_TM, D_MODEL), row),
                  pl.BlockSpec((_TM, D_MODEL), _ctx_map),
                  pl.BlockSpec((_TM, D_MODEL), _lat_map),
                  pl.BlockSpec((1, 1, N_MOD * D_MODEL), lambda i: (_mod_row(i, _TM), 0, 0)),
                  const((1, D_MODEL)),
                  const((D_ATT, D_MODEL)), const((D_RNN, D_MODEL)), const((D_MODEL, D_MODEL)),
                  const((N_EXPERTS, D_MODEL)), const((N_EXPERTS, D_MODEL)), const((N_EXPERTS, 1)),
                  const((D_MODEL, D_EXPERT)), const((D_MODEL, D_EXPERT)), const((D_EXPERT, D_MODEL))],
        out_specs=[pl.BlockSpec((_TM, D_MODEL), row),
                   pl.BlockSpec((_TM * _FOLD, LANES), row),
                   pl.BlockSpec((N_EXPERTS, _TM), lambda i: (0, i))],
        out_shape=[jax.ShapeDtypeStruct((N_TOK, D_MODEL), F32),
                   jax.ShapeDtypeStruct((N_TOK * _FOLD, LANES), F32),
                   jax.ShapeDtypeStruct((N_EXPERTS, N_TOK), F32)],
        compiler_params=_params(("parallel",)),
        name="merge",
    )(oac, oal, ob, ga, gb, xp, xs, mod3, g2, wpa, wpb, wo, wrh, wrl, br, wgs, wus, wds)


_FOLD = D_MODEL // LANES
_NH = N_TOK // 2
_R = 512
_SLOTS = 256
_ID_BITS = 13
_DISP_BITS = 12
_VALID_BIT = _ID_BITS + _DISP_BITS + 1


def _chunk_table(cnt):
    shape = (N_EXPERTS, _SLOTS)
    nch = jnp.broadcast_to((cnt + (_R - 1)) // _R, shape)
    erow = lax.broadcasted_iota(jnp.int32, shape, 0)
    q = lax.broadcasted_iota(jnp.int32, shape, 1)
    cum = nch
    s = 1
    while s < N_EXPERTS:
        cum = cum + jnp.where(erow >= s, pltpu.roll(cum, s, 0), 0)
        s *= 2
    total = cum[N_EXPERTS - 1:N_EXPERTS, :]
    used = q[:1] < total
    expert = jnp.minimum(jnp.sum((cum <= q).astype(jnp.int32), axis=0, keepdims=True), N_EXPERTS - 1)
    first = cum - nch
    start = jnp.max(jnp.where(first <= q, first, 0), axis=0, keepdims=True)
    chunk = jnp.where(used, q[:1] - start, 0)
    return jnp.concatenate([expert, chunk, used.astype(jnp.int32), total,
                            jnp.zeros((4, _SLOTS), jnp.int32)], axis=0)


def _dispatch_kernel(g_ref, src_ref, dst_ref, gl_ref, tbl_ref):
    g = g_ref[...]
    n = g.shape[1]
    lane = lax.broadcasted_iota(jnp.int32, g.shape, 1)
    sel = g > 0.0
    m = sel.astype(jnp.int32)
    csum = m
    s = 1
    while s < n:
        csum = csum + jnp.where(lane >= s, pltpu.roll(csum, s, 1), 0)
        s *= 2
    tbl_ref[0] = _chunk_table(csum[:, n - 1:n])
    disp = lane - (csum - m)
    packed = jnp.where(sel, (1 << _VALID_BIT) | (disp << _ID_BITS) | lane, 0)
    g = jnp.where(sel, g, 0.0)
    for b in range(_DISP_BITS):
        shift = n - (1 << b)
        inc = pltpu.roll(packed, shift, 1)
        inc_moves = ((inc >> (_ID_BITS + b)) & 1) == 1
        own_moves = ((packed >> (_ID_BITS + b)) & 1) == 1
        packed = jnp.where(inc_moves, inc, jnp.where(own_moves, 0, packed))
        g = jnp.where(inc_moves, pltpu.roll(g, shift, 1), jnp.where(own_moves, 0.0, g))
    valid = (packed >> _VALID_BIT) == 1
    tok = packed & ((1 << _ID_BITS) - 1)
    src_ref[0] = jnp.where(valid, tok, 0) * _FOLD
    dst_ref[0] = jnp.where(valid, tok, n) * _FOLD
    gl_ref[0] = g


def _dispatch(gates_t):
    halves = N_TOK // _NH
    blk = pl.BlockSpec((1, N_EXPERTS, _NH), lambda h: (h, 0, 0))
    lists = jax.ShapeDtypeStruct((halves, N_EXPERTS, _NH), jnp.int32)
    return pl.pallas_call(
        _dispatch_kernel,
        grid=(halves,),
        in_specs=[pl.BlockSpec((N_EXPERTS, _NH), lambda h: (0, h))],
        out_specs=[blk, blk, blk, pl.BlockSpec((1, 8, _SLOTS), lambda h: (h, 0, 0))],
        out_shape=[lists, lists, jax.ShapeDtypeStruct((halves, N_EXPERTS, _NH), F32),
                   jax.ShapeDtypeStruct((halves, 8, _SLOTS), jnp.int32)],
        compiler_params=_params(("parallel",)),
        name="dispatch",
    )(gates_t)


_GRP = 8
_MXU = 256
_KT = D_MODEL // _MXU
_UNF = 64
_TBL = 8 * _SLOTS
_STEPS = _NH * TOP_K // _R + N_EXPERTS + 2


def _tbl(tbl_ref, h, row, k):
    return tbl_ref[h * _TBL + row * _SLOTS + jnp.clip(k, 0, _SLOTS - 1)]


def _list_block(h, k, tbl_ref):
    per_list = _NH // _R
    return ((h * N_EXPERTS + _tbl(tbl_ref, h, 0, k)) * per_list + _tbl(tbl_ref, h, 1, k),)


def _moe_kernel(tbl_ref, src_ref, dst_ref, gate_ref, hf_ref, wg_ref, wu_ref, wd_ref, out_hbm,
                acc_ref, wgb_ref, wub_ref, wdb_ref, xg_ref, xa_ref, xb_ref, ya_ref, yb_ref, sem):
    h = pl.program_id(0)
    t = pl.program_id(1)
    total = _tbl(tbl_ref, h, 3, 0)

    @pl.when(t == 0)
    def _():
        for ref in (acc_ref, ya_ref, yb_ref):
            ref[...] = jnp.zeros_like(ref)

    @pl.when((h == 0) & (t == 0))
    def _():
        for ref in (xa_ref, xb_ref, wgb_ref, wub_ref, wdb_ref):
            ref[...] = jnp.zeros_like(ref)

    mm_used = (t >= 1) & (_tbl(tbl_ref, h, 2, t - 1) == 1)

    @pl.when(mm_used & (_tbl(tbl_ref, h, 1, t - 1) == 0))
    def _():
        wgb_ref[...] = wg_ref[0].astype(BF16)
        wub_ref[...] = wu_ref[0].astype(BF16)
        wdb_ref[...] = wd_ref[0].astype(BF16)

    def row(ref, r):
        return ref.at[pl.ds(pl.multiple_of(r, _FOLD), _FOLD), :]

    def stages(x_next, x_mm, y_mm, y_scatter):
        def gather_group(j0):
            for j in range(j0, j0 + _GRP):
                row(xg_ref, j * _FOLD)[...] = row(hf_ref, src_ref[j])[...]

        def scatter_group(j0):
            dsts = [row(acc_ref, dst_ref[j0 + u]) for u in range(_GRP)]
            news = [dsts[u][...] + row(y_scatter, (j0 + u) * _FOLD)[...] for u in range(_GRP)]
            for u in range(_GRP):
                dsts[u][...] = news[u]

        def unfold_block(r0):
            for k in range(_FOLD):
                x_next[r0:r0 + _UNF, k * LANES:(k + 1) * LANES] = xg_ref[
                    pl.ds(r0 * _FOLD + k, _UNF, stride=_FOLD), :].astype(BF16)

        moves = []
        for j0 in range(0, _R, _GRP):
            moves += [functools.partial(scatter_group, j0), functools.partial(gather_group, j0)]
            if (j0 + _GRP) % _UNF == 0:
                moves.append(functools.partial(unfold_block, j0 + _GRP - _UNF))
        n_pieces = 3 * _KT
        done = [0]

        def some_moves():
            done[0] += 1
            for f in moves[(done[0] - 1) * len(moves) // n_pieces:done[0] * len(moves) // n_pieces]:
                f()

        diag = (lax.broadcasted_iota(jnp.int32, (_R, _R), 0) == lax.broadcasted_iota(jnp.int32, (_R, _R), 1))
        gates = gate_ref[0] * mm_used.astype(F32)
        gcol = jnp.sum(jnp.where(diag, gates, 0.0), axis=1, keepdims=True)
        gt = up = None
        for kt in range(_KT):
            rows_k = slice(kt * _MXU, (kt + 1) * _MXU)
            xk = x_mm[:, rows_k]
            d = jnp.dot(xk, wgb_ref[rows_k, :], preferred_element_type=F32)
            gt = d if gt is None else gt + d
            some_moves()
            d = jnp.dot(xk, wub_ref[rows_k, :], preferred_element_type=F32)
            up = d if up is None else up + d
            some_moves()
        act = (_silu(gt) * up * gcol).astype(BF16)
        for nt in range(D_MODEL // _MXU):
            y = jnp.dot(act, wdb_ref[:, nt * _MXU:(nt + 1) * _MXU], preferred_element_type=F32)
            some_moves()
            for k in range(_MXU // LANES):
                y_mm[pl.ds(nt * _MXU // LANES + k, _R, stride=_FOLD), :] = y[:, k * LANES:(k + 1) * LANES]

    busy = t < total + 2

    @pl.when(busy & (t % 2 == 0))
    def _():
        stages(xa_ref, xb_ref, yb_ref, ya_ref)

    @pl.when(busy & (t % 2 == 1))
    def _():
        stages(xb_ref, xa_ref, ya_ref, yb_ref)

    @pl.when(t == _STEPS - 1)
    def _():
        rows = _NH * _FOLD
        cp = pltpu.make_async_copy(acc_ref.at[pl.ds(0, rows)],
                                   out_hbm.at[pl.ds(pl.multiple_of(h * rows, rows), rows)], sem)
        cp.start()
        cp.wait()


def _moe(tbl, src, dst, gl, hfold, wg, wu, wd):
    halves = N_TOK // _NH
    smem = lambda lag: pl.BlockSpec((_R,), lambda h, t, tbl_ref: _list_block(h, t - lag, tbl_ref),
                                    memory_space=pltpu.SMEM)
    wmap = lambda h, t, tbl_ref: (_tbl(tbl_ref, h, 0, t - 1), 0, 0)
    stage = pltpu.VMEM((_R * _FOLD, LANES), F32)
    packed = pltpu.VMEM((_R, D_MODEL), BF16)
    return pl.pallas_call(
        _moe_kernel,
        grid_spec=pltpu.PrefetchScalarGridSpec(
            num_scalar_prefetch=1,
            grid=(halves, _STEPS),
            in_specs=[smem(0), smem(2),
                      pl.BlockSpec((1, 1, _R), lambda h, t, tbl_ref: _list_block(h, t - 1, tbl_ref) + (0, 0)),
                      pl.BlockSpec((_NH * _FOLD, LANES), lambda h, t, tbl_ref: (h, 0)),
                      pl.BlockSpec((1, D_MODEL, D_EXPERT), wmap),
                      pl.BlockSpec((1, D_MODEL, D_EXPERT), wmap),
                      pl.BlockSpec((1, D_EXPERT, D_MODEL), wmap)],
            out_specs=pl.BlockSpec(memory_space=pl.ANY),
            scratch_shapes=[pltpu.VMEM(((_NH + 1) * _FOLD, LANES), F32),
                            pltpu.VMEM((D_MODEL, D_EXPERT), BF16),
                            pltpu.VMEM((D_MODEL, D_EXPERT), BF16),
                            pltpu.VMEM((D_EXPERT, D_MODEL), BF16),
                            stage, packed, packed, stage, stage,
                            pltpu.SemaphoreType.DMA]),
        out_shape=jax.ShapeDtypeStruct((N_TOK * _FOLD, LANES), F32),
        compiler_params=_params(("arbitrary", "arbitrary")),
        name="moe",
    )(tbl, src, dst, gl, hfold, wg, wu, wd)


def _final_kernel(x_ref, r_ref, mod_ref, fg_ref, yp_ref, ys_ref):
    i = pl.program_id(0)
    routed = jnp.concatenate([r_ref[pl.ds(k, _TM, stride=_FOLD), :] for k in range(_FOLD)], axis=1)
    m = mod_ref[0]
    x2 = x_ref[...] + m[:, 5 * D_MODEL:6 * D_MODEL] * routed
    y = x2 * lax.rsqrt(jnp.mean(x2 * x2, axis=-1, keepdims=True) + EPS) * fg_ref[...]

    @pl.when(i < _CTX_TILES)
    def _():
        yp_ref[...] = y

    @pl.when(i >= _CTX_TILES)
    def _():
        ys_ref[...] = y


def _final(x1s, routed, mod3, fg):
    return pl.pallas_call(
        _final_kernel,
        grid=(N_TOK // _TM,),
        in_specs=[pl.BlockSpec((_TM, D_MODEL), lambda i: (i, 0)),
                  pl.BlockSpec((_TM * _FOLD, LANES), lambda i: (i, 0)),
                  pl.BlockSpec((1, 1, N_MOD * D_MODEL), lambda i: (_mod_row(i, _TM), 0, 0)),
                  pl.BlockSpec((1, D_MODEL), lambda i: (0, 0))],
        out_specs=[pl.BlockSpec((_TM, D_MODEL), _ctx_map),
                   pl.BlockSpec((_TM, D_MODEL), _lat_map)],
        out_shape=[jax.ShapeDtypeStruct((N_CTX, D_MODEL), F32),
                   jax.ShapeDtypeStruct((N_LAT, D_MODEL), F32)],
        compiler_params=_params(("arbitrary",)),
        name="final",
    )(x1s, routed, mod3, fg)


def kernel(x_prompt, x_sample, cache_k, cache_v, state_h_fwd, state_h_bwd, c, c_ctx, norm1_g, norm2_g, final_g, w_ada, b_ada, w_in, rpb, conv_w, conv_b, rg_wa, rg_ba, rg_wx, rg_bx, rg_lam, w_pa, w_pb, w_out, w_router, b_router, w_gate_e, w_up_e, w_down_e, w_gate_s, w_up_s, w_down_s):
    l = 0
    xp = x_prompt.reshape(N_CTX, D_MODEL)
    xs = x_sample.reshape(N_LAT, D_MODEL)

    cvecs = jnp.concatenate([c_ctx[None, :], c], axis=0)[:, :, None]
    mod = _adaln(cvecs, w_ada[l], b_ada[l][None, :])
    mod3 = mod.reshape(8, 1, N_MOD * D_MODEL)

    q, kc, vc, kl, vl, xr, yg, ga, gb = _inproj(xp, xs, mod3, norm1_g[l][None, :], w_in[l].astype(BF16))

    oac = _attn_ctx(q, kc, vc)
    ck = cache_k[:, l].reshape(DEC_BATCH, PAST_LEN, D_ATT)
    cv = cache_v[:, l].reshape(DEC_BATCH, PAST_LEN, D_ATT)
    oal = _attn_lat(q, kl, vl, ck, cv, _bias_tables(rpb[l]))

    w4 = jnp.concatenate([_block_diag(rg_wa[l, 0]), _block_diag(rg_wx[l, 0]),
                          _block_diag(rg_wa[l, 1]), _block_diag(rg_wx[l, 1])], axis=1).astype(BF16)
    b4 = jnp.concatenate([rg_ba[l, 0], rg_bx[l, 0], rg_ba[l, 1], rg_bx[l, 1]])[None, :]
    ob, hlf, hlb = _rglru(xr, yg, conv_w[l], conv_b[l][None, :], w4, b4, rg_lam[l],
                          state_h_fwd[:, l], state_h_bwd[:, l])

    wr_t = w_router[l].T
    wr_hi = wr_t.astype(BF16)
    wr_lo = (wr_t - wr_hi.astype(F32)).astype(BF16)
    x1s, hfold, gates_t = _merge(oac, oal, ob, ga, gb, xp, xs, mod3, norm2_g[l][None, :],
                                 w_pa[l].astype(BF16), w_pb[l].astype(BF16), w_out[l].astype(BF16),
                                 wr_hi, wr_lo, b_router[l][:, None],
                                 w_gate_s[l].astype(BF16), w_up_s[l].astype(BF16), w_down_s[l].astype(BF16))

    src, dst, gl, tbl = _dispatch(gates_t)
    routed = _moe(tbl.reshape(-1), src.reshape(-1), dst.reshape(-1), gl.reshape(-1, 1, _R), hfold,
                  w_gate_e[l], w_up_e[l], w_down_e[l])
    yp, ys = _final(x1s, routed, mod3, final_g[None, :])

    return (yp.reshape(BATCH, SEQ, D_MODEL),
            ys.reshape(DEC_BATCH, DEC_SEQ, D_MODEL),
            kc.reshape(BATCH, 1, SEQ, N_HEADS_A, HEAD_DIM_A),
            vc.reshape(BATCH, 1, SEQ, N_HEADS_A, HEAD_DIM_A),
            hlf.reshape(BATCH, 1, D_RNN),
            hlb.reshape(BATCH, 1, D_RNN))
```

```python
import functools

import jax
import jax.numpy as jnp
from jax import lax
from jax.experimental import pallas as pl
from jax.experimental.pallas import tpu as pltpu

F32 = jnp.float32
BF16 = jnp.bfloat16

D_MODEL = 1024
BATCH = 16
SEQ = 256
DEC_BATCH = 2
DEC_SEQ = 1024
PAST_LEN = 512
GRID_W = 64
N_HEADS_A = 8
HEAD_DIM_A = 64
D_ATT = N_HEADS_A * HEAD_DIM_A
KH = 8
KW = 16
D_RNN = 512
N_RG_BLOCKS = 8
CONV_W = 4
RG_C = 8.0
N_EXPERTS = 64
TOP_K = 8
N_GROUPS = 8
GROUP_SIZE = N_EXPERTS // N_GROUPS
TOPK_GROUP = 4
D_EXPERT = 256
ROUTED_SCALE = 2.5
N_MOD = 6
EPS = 1e-6
NEG = -1e30
D_IN = 3 * D_ATT + 2 * D_RNN + 2 * D_MODEL

N_CTX = BATCH * SEQ
N_LAT = DEC_BATCH * DEC_SEQ
N_TOK = N_CTX + N_LAT
GRID_ROWS = DEC_SEQ // GRID_W

LANES = 128
VMEM_LIMIT = 56 * 1024 * 1024


def _params(sem, vmem=VMEM_LIMIT):
    return pltpu.CompilerParams(dimension_semantics=sem, vmem_limit_bytes=vmem)


def _mod_row(i, tile_rows):
    n_ctx_tiles = N_CTX // tile_rows
    return jnp.where(i < n_ctx_tiles, 0, 1 + (i - n_ctx_tiles) // (DEC_SEQ // tile_rows))


def _norm_mod(x, g, shift, scale):
    y = x * lax.rsqrt(jnp.mean(x * x, axis=-1, keepdims=True) + EPS)
    return (y * g) * (1.0 + scale) + shift


def _sigmoid(x):
    return 0.5 * jnp.tanh(0.5 * x) + 0.5


def _silu(x):
    return x * _sigmoid(x)


def _adaln_kernel(c_ref, w_ref, b_ref, o_ref):
    w = w_ref[...]
    rows = []
    for r in range(3):
        s = _silu(c_ref[r])
        rows.append(jnp.sum(w * s, axis=0, keepdims=True))
    rows.append(jnp.zeros((5, w.shape[1]), F32))
    o_ref[...] = jnp.concatenate(rows, axis=0) + b_ref[...]


def _adaln(cvecs, w_ada, b_ada):
    tn = 512
    n = N_MOD * D_MODEL
    return pl.pallas_call(
        _adaln_kernel,
        grid=(n // tn,),
        in_specs=[pl.BlockSpec((3, D_MODEL, 1), lambda j: (0, 0, 0)),
                  pl.BlockSpec((D_MODEL, tn), lambda j: (0, j)),
                  pl.BlockSpec((1, tn), lambda j: (0, j))],
        out_specs=pl.BlockSpec((8, tn), lambda j: (0, j)),
        out_shape=jax.ShapeDtypeStruct((8, n), F32),
        compiler_params=_params(("parallel",)),
        name="adaln",
    )(cvecs, w_ada, b_ada)


_TM = 512
_CTX_TILES = N_CTX // _TM
_LAT_TILES = N_LAT // _TM


def _ctx_map(i):
    return (jnp.minimum(i, _CTX_TILES - 1), 0)


def _lat_map(i):
    return (jnp.maximum(i - _CTX_TILES, 0), 0)


def _inproj_kernel(xp_ref, xs_ref, mod_ref, g_ref, w_ref,
                   q_ref, kc_ref, vc_ref, kl_ref, vl_ref, xr_ref, yg_ref, ga_ref, gb_ref):
    i = pl.program_id(0)
    is_ctx = i < _CTX_TILES
    x = jnp.where(is_ctx, xp_ref[...], xs_ref[...])
    m = mod_ref[0]
    h = _norm_mod(x, g_ref[...], m[:, 0:D_MODEL], m[:, D_MODEL:2 * D_MODEL]).astype(BF16)

    def proj(a, b):
        return jnp.dot(h, w_ref[:, a:b], preferred_element_type=F32)

    q_ref[...] = proj(0, D_ATT).astype(BF16)
    k = proj(D_ATT, 2 * D_ATT)
    v = proj(2 * D_ATT, 3 * D_ATT)

    @pl.when(is_ctx)
    def _():
        kc_ref[...] = k
        vc_ref[...] = v

    @pl.when(jnp.logical_not(is_ctx))
    def _():
        kl_ref[...] = k
        vl_ref[...] = v

    o = 3 * D_ATT
    xr_ref[...] = proj(o, o + D_RNN)
    yg_ref[...] = proj(o + D_RNN, o + 2 * D_RNN).astype(BF16)
    o += 2 * D_RNN
    ga_ref[...] = proj(o, o + D_MODEL).astype(BF16)
    gb_ref[...] = proj(o + D_MODEL, o + 2 * D_MODEL).astype(BF16)


def _inproj(xp, xs, mod3, g1, w_in):
    row = lambda i: (i, 0)
    return pl.pallas_call(
        _inproj_kernel,
        grid=(N_TOK // _TM,),
        in_specs=[pl.BlockSpec((_TM, D_MODEL), _ctx_map),
                  pl.BlockSpec((_TM, D_MODEL), _lat_map),
                  pl.BlockSpec((1, 1, N_MOD * D_MODEL), lambda i: (_mod_row(i, _TM), 0, 0)),
                  pl.BlockSpec((1, D_MODEL), lambda i: (0, 0)),
                  pl.BlockSpec((D_MODEL, D_IN), lambda i: (0, 0))],
        out_specs=[pl.BlockSpec((_TM, D_ATT), row),
                   pl.BlockSpec((_TM, D_ATT), _ctx_map),
                   pl.BlockSpec((_TM, D_ATT), _ctx_map),
                   pl.BlockSpec((_TM, D_ATT), _lat_map),
                   pl.BlockSpec((_TM, D_ATT), _lat_map),
                   pl.BlockSpec((_TM, D_RNN), row),
                   pl.BlockSpec((_TM, D_RNN), row),
                   pl.BlockSpec((_TM, D_MODEL), row),
                   pl.BlockSpec((_TM, D_MODEL), row)],
        out_shape=[jax.ShapeDtypeStruct((N_TOK, D_ATT), BF16),
                   jax.ShapeDtypeStruct((N_CTX, D_ATT), F32),
                   jax.ShapeDtypeStruct((N_CTX, D_ATT), F32),
                   jax.ShapeDtypeStruct((N_LAT, D_ATT), F32),
                   jax.ShapeDtypeStruct((N_LAT, D_ATT), F32),
                   jax.ShapeDtypeStruct((N_TOK, D_RNN), F32),
                   jax.ShapeDtypeStruct((N_TOK, D_RNN), BF16),
                   jax.ShapeDtypeStruct((N_TOK, D_MODEL), BF16),
                   jax.ShapeDtypeStruct((N_TOK, D_MODEL), BF16)],
        compiler_params=_params(("arbitrary",)),
        name="inproj",
    )(xp, xs, mod3, g1, w_in)


def _pair_attention(qp, segs):
    scale = HEAD_DIM_A ** -0.5
    lane = lax.broadcasted_iota(jnp.int32, (1, LANES), 1)
    outs = []
    for a in range(2):
        sel = (lane >= HEAD_DIM_A) if a else (lane < HEAD_DIM_A)
        qm = jnp.where(sel, qp, jnp.zeros_like(qp))
        ss = []
        for kp, _, bias_fn in segs:
            s = lax.dot_general(qm, kp, (((1,), (1,)), ((), ())), preferred_element_type=F32) * scale
            if bias_fn is not None:
                s = s + bias_fn(a)
            ss.append(s)
        m = functools.reduce(jnp.maximum, [jnp.max(s, axis=-1, keepdims=True) for s in ss])
        es = [jnp.exp(s - m) for s in ss]
        inv = 1.0 / functools.reduce(jnp.add, [jnp.sum(e, axis=-1, keepdims=True) for e in es])
        o = functools.reduce(jnp.add, [
            jnp.dot((e * inv).astype(BF16), vp, preferred_element_type=F32)
            for e, (_, vp, _) in zip(es, segs)])
        outs.append(o)
    return jnp.where(lane < HEAD_DIM_A, outs[0], outs[1])


def _attn_ctx_kernel(q_ref, k_ref, v_ref, o_ref):
    for j in range(D_ATT // LANES):
        c = slice(j * LANES, (j + 1) * LANES)
        segs = [(k_ref[:, c].astype(BF16), v_ref[:, c].astype(BF16), None)]
        o_ref[:, c] = _pair_attention(q_ref[:, c], segs).astype(BF16)


def _attn_ctx(q_all, kc, vc):
    blk = pl.BlockSpec((SEQ, D_ATT), lambda b: (b, 0))
    return pl.pallas_call(
        _attn_ctx_kernel,
        grid=(BATCH,),
        in_specs=[blk, blk, blk],
        out_specs=blk,
        out_shape=jax.ShapeDtypeStruct((N_CTX, D_ATT), BF16),
        compiler_params=_params(("parallel",)),
        name="attn_ctx",
    )(q_all, kc, vc)


_QROWS = 4
_QB = _QROWS * GRID_W
_KROWS = 12
_KB = _KROWS * GRID_W


def _build_bias_tables(rows_ref, t2_ref):
    wq = lax.broadcasted_iota(jnp.int32, (GRID_W, LANES), 0)
    wk = lax.broadcasted_iota(jnp.int32, (GRID_W, LANES), 1) & (GRID_W - 1)
    col_start = jnp.clip(wq - KW // 2, 0, GRID_W - KW)
    col_in = (wk >= col_start) & (wk < col_start + KW)
    for r in range(N_HEADS_A * 2 * KH):
        x = jnp.broadcast_to(rows_ref[r:r + 1, :], (GRID_W, LANES))
        x = pltpu.roll(x, LANES - (KW - 1), 1, stride=1, stride_axis=0)
        t2_ref[r // (2 * KH), r % (2 * KH)] = jnp.where(col_in, x, NEG)


def _attn_lat_kernel(q_ref, k_ref, v_ref, ck_ref, cv_ref, rows_ref, o_ref, t2_ref):
    i = pl.program_id(1)

    @pl.when((pl.program_id(0) == 0) & (i == 0))
    def _():
        _build_bias_tables(rows_ref, t2_ref)

    ks = jnp.where(i < 2, 0, GRID_ROWS - _KROWS)
    kstart = pl.multiple_of(ks * GRID_W, 256)
    lane = lax.broadcasted_iota(jnp.int32, (1, LANES), 1)

    def bias_for_head(h):
        rows = []
        for rq in range(_QROWS):
            r = _QROWS * i + rq
            rs = jnp.clip(r - KH // 2, 0, GRID_ROWS - KH)
            tiles = []
            for t in range(_KROWS // 2):
                kr = ks + 2 * t
                d = jnp.clip(kr - r + (KH - 1), -1, 2 * KH - 2)
                tile = t2_ref[h, d + 1]
                v0 = ((kr >= rs) & (kr < rs + KH)).astype(jnp.int32)
                v1 = ((kr + 1 >= rs) & (kr + 1 < rs + KH)).astype(jnp.int32)
                valid = jnp.where(lane < GRID_W, v0, v1) > 0
                tiles.append(jnp.where(valid, tile, NEG))
            rows.append(jnp.concatenate(tiles, axis=1))
        return jnp.concatenate(rows, axis=0)

    for j in range(D_ATT // LANES):
        c = slice(j * LANES, (j + 1) * LANES)
        k_loc = k_ref[pl.ds(kstart, _KB), c].astype(BF16)
        v_loc = v_ref[pl.ds(kstart, _KB), c].astype(BF16)
        segs = [(k_loc, v_loc, lambda a, j=j: bias_for_head(2 * j + a)),
                (ck_ref[:, c].astype(BF16), cv_ref[:, c].astype(BF16), None)]
        o_ref[:, c] = _pair_attention(q_ref[:, c], segs).astype(BF16)


def _attn_lat(q_all, kl, vl, ck, cv, t2):
    qblocks = DEC_SEQ // _QB
    return pl.pallas_call(
        _attn_lat_kernel,
        grid=(DEC_BATCH, qblocks),
        in_specs=[pl.BlockSpec((_QB, D_ATT), lambda b, i: (N_CTX // _QB + b * qblocks + i, 0)),
                  pl.BlockSpec((DEC_SEQ, D_ATT), lambda b, i: (b, 0)),
                  pl.BlockSpec((DEC_SEQ, D_ATT), lambda b, i: (b, 0)),
                  pl.BlockSpec((None, PAST_LEN, D_ATT), lambda b, i: (b, 0, 0)),
                  pl.BlockSpec((None, PAST_LEN, D_ATT), lambda b, i: (b, 0, 0)),
                  pl.BlockSpec((N_HEADS_A * 2 * KH, LANES), lambda b, i: (0, 0))],
        out_specs=pl.BlockSpec((_QB, D_ATT), lambda b, i: (b * qblocks + i, 0)),
        out_shape=jax.ShapeDtypeStruct((N_LAT, D_ATT), BF16),
        scratch_shapes=[pltpu.VMEM((N_HEADS_A, 2 * KH, GRID_W, LANES), F32)],
        compiler_params=_params(("arbitrary", "arbitrary")),
        name="attn_lat",
    )(q_all, kl, vl, ck, cv, t2)


def _bias_tables(rpb):
    half = jnp.pad(rpb, ((0, 0), (0, 0), (0, GRID_W - rpb.shape[-1])))
    neg = jnp.full((N_HEADS_A, 1, GRID_W), NEG, F32)
    left = jnp.concatenate([neg, half], axis=1)
    right = jnp.concatenate([half, neg], axis=1)
    return jnp.concatenate([left, right], axis=-1).reshape(N_HEADS_A * 2 * KH, LANES)


_RG_ROWS = 2048
_RG_CHUNK = 256
_RG_SLABS = D_RNN // LANES
_RG_CTX_STEPS = N_CTX // _RG_ROWS


def _rg_block(nb, T, interleaved, xr_ref, yg_ref, cw_ref, cb_ref, w4_ref, b4_ref, lam_ref, h0f, h0b,
              ob_ref, xc_ref, af_ref, uf_ref, ab_ref, ub_ref, hf_ref, hb_ref):
    R = nb * T
    xr = xr_ref[...]
    t_idx = lax.broadcasted_iota(jnp.int32, (R, 1), 0) & (T - 1)
    cw = cw_ref[...]
    xc = jnp.zeros((R, D_RNN), F32) + cb_ref[...]
    for j in range(CONV_W):
        off = j - 2
        if off == 0:
            tap = xr
        else:
            tap = pltpu.roll(xr, shift=(-off) % R, axis=0)
            tap = jnp.where((t_idx + off >= 0) & (t_idx + off < T), tap, 0.0)
        xc = xc + cw[j:j + 1, :] * tap
    xc_ref[...] = xc

    lam = lam_ref[...]
    z = -lam
    sp = jnp.maximum(z, 0.0) + jnp.log1p(jnp.exp(-jnp.abs(z)))

    def chunk_rows(c):
        if interleaved:
            return pl.ds(c, _RG_CHUNK, stride=nb)
        return pl.ds(pl.multiple_of(c * _RG_CHUNK, _RG_CHUNK), _RG_CHUNK)

    def put(ref, c, val):
        for k in range(_RG_SLABS):
            ref[k, chunk_rows(c), :] = val[:, k * LANES:(k + 1) * LANES]

    def get(ref, c):
        return jnp.concatenate([ref[k, chunk_rows(c), :] for k in range(_RG_SLABS)], axis=1)

    def gate_chunk(c, carry):
        x = xc_ref[pl.ds(pl.multiple_of(c * _RG_CHUNK, _RG_CHUNK), _RG_CHUNK), :]
        g4 = jnp.dot(x.astype(BF16), w4_ref[...], preferred_element_type=F32) + b4_ref[...]
        for d, (a_ref, u_ref) in enumerate(((af_ref, uf_ref), (ab_ref, ub_ref))):
            r = _sigmoid(g4[:, (2 * d) * D_RNN:(2 * d + 1) * D_RNN])
            ig = _sigmoid(g4[:, (2 * d + 1) * D_RNN:(2 * d + 2) * D_RNN])
            a = jnp.exp(-RG_C * r * sp[d:d + 1, :])
            put(a_ref, c, a)
            put(u_ref, c, jnp.sqrt(jnp.maximum(1.0 - a * a, 0.0)) * (ig * x))
        return carry

    lax.fori_loop(0, R // _RG_CHUNK, gate_chunk, 0)

    if interleaved:
        groups = [lambda t: pl.ds(pl.multiple_of(t * nb, nb), nb)]
    else:
        groups = [lambda t, bi=bi: pl.ds(bi * T + t, 1) for bi in range(nb)]

    def step(t, carry):
        hf, hb = carry
        nf, nbk = [], []
        for gi, rows_at in enumerate(groups):
            rf, rb = rows_at(t), rows_at(T - 1 - t)
            for k in range(_RG_SLABS):
                n = gi * _RG_SLABS + k
                f = af_ref[k, rf, :] * hf[n] + uf_ref[k, rf, :]
                b = ab_ref[k, rb, :] * hb[n] + ub_ref[k, rb, :]
                hf_ref[k, rf, :] = f
                hb_ref[k, rb, :] = b
                nf.append(f)
                nbk.append(b)
        return tuple(nf), tuple(nbk)

    hf, hb = lax.fori_loop(0, T, step, (tuple(h0f), tuple(h0b)))

    def out_chunk(c, carry):
        rows = pl.ds(pl.multiple_of(c * _RG_CHUNK, _RG_CHUNK), _RG_CHUNK)
        y = yg_ref[rows, :].astype(F32)
        ob_ref[rows, :] = ((get(hf_ref, c) + get(hb_ref, c)) * jax.nn.gelu(y)).astype(BF16)
        return carry

    lax.fori_loop(0, R // _RG_CHUNK, out_chunk, 0)
    return hf, hb


def _rglru_kernel(xr_ref, yg_ref, cw_ref, cb_ref, w4_ref, b4_ref, lam_ref, sf_ref, sb_ref,
                  ob_ref, hlf_ref, hlb_ref, *scratch):
    i = pl.program_id(0)
    shared = (xr_ref, yg_ref, cw_ref, cb_ref, w4_ref, b4_ref, lam_ref)
    nb_ctx = _RG_ROWS // SEQ

    @pl.when(i < _RG_CTX_STEPS)
    def _():
        zero = [jnp.zeros((nb_ctx, LANES), F32)] * _RG_SLABS
        hf, hb = _rg_block(nb_ctx, SEQ, True, *shared, zero, zero, ob_ref, *scratch)
        hlf_ref[...] = jnp.concatenate(hf, axis=1)
        hlb_ref[...] = jnp.concatenate(hb, axis=1)

    @pl.when(i >= _RG_CTX_STEPS)
    def _():
        slabs = lambda ref: [ref[bi:bi + 1, k * LANES:(k + 1) * LANES]
                             for bi in range(DEC_BATCH) for k in range(_RG_SLABS)]
        _rg_block(DEC_BATCH, DEC_SEQ, False, *shared, slabs(sf_ref), slabs(sb_ref), ob_ref, *scratch)


def _rglru(xr, yg, conv_w, conv_b, w4, b4, lam, sf, sb):
    steps = N_TOK // _RG_ROWS
    nb_ctx = _RG_ROWS // SEQ
    const = lambda shape: pl.BlockSpec(shape, lambda i: (0,) * len(shape))
    hl_map = lambda i: (jnp.minimum(i, _RG_CTX_STEPS - 1), 0)
    scan_buf = pltpu.VMEM((_RG_SLABS, _RG_ROWS, LANES), F32)
    return pl.pallas_call(
        _rglru_kernel,
        grid=(steps,),
        in_specs=[pl.BlockSpec((_RG_ROWS, D_RNN), lambda i: (i, 0)),
                  pl.BlockSpec((_RG_ROWS, D_RNN), lambda i: (i, 0)),
                  const((CONV_W, D_RNN)), const((1, D_RNN)),
                  const((D_RNN, 4 * D_RNN)), const((1, 4 * D_RNN)), const((2, D_RNN)),
                  const((DEC_BATCH, D_RNN)), const((DEC_BATCH, D_RNN))],
        out_specs=[pl.BlockSpec((_RG_ROWS, D_RNN), lambda i: (i, 0)),
                   pl.BlockSpec((nb_ctx, D_RNN), hl_map),
                   pl.BlockSpec((nb_ctx, D_RNN), hl_map)],
        out_shape=[jax.ShapeDtypeStruct((N_TOK, D_RNN), BF16),
                   jax.ShapeDtypeStruct((BATCH, D_RNN), F32),
                   jax.ShapeDtypeStruct((BATCH, D_RNN), F32)],
        scratch_shapes=[pltpu.VMEM((_RG_ROWS, D_RNN), F32)] + [scan_buf] * 6,
        compiler_params=_params(("arbitrary",)),
        name="rglru",
    )(xr, yg, conv_w, conv_b, w4, b4, lam, sf, sb)


def _block_diag(w):
    n, d, _ = w.shape
    eye = jnp.eye(n, dtype=w.dtype)
    return jnp.einsum('nij,nm->nimj', w, eye).reshape(n * d, n * d)


def _nt_dot(a, b):
    return lax.dot_general(a, b, (((1,), (1,)), ((), ())), preferred_element_type=F32)


def _route(s, sb):
    n = s.shape[1]
    low = -3.0e38
    sb3 = sb.reshape(N_GROUPS, GROUP_SIZE, n)
    iw = lax.broadcasted_iota(jnp.int32, sb3.shape, 1)
    m1 = jnp.max(sb3, axis=1, keepdims=True)
    i1 = jnp.min(jnp.where(sb3 == m1, iw, GROUP_SIZE), axis=1, keepdims=True)
    m2 = jnp.max(jnp.where(iw == i1, low, sb3), axis=1, keepdims=True)
    gscore = m1 + m2
    ig = lax.broadcasted_iota(jnp.int32, gscore.shape, 0)
    beaten = jnp.zeros(gscore.shape, jnp.int32)
    for g in range(N_GROUPS):
        row = gscore[g:g + 1]
        beaten = beaten + ((row > gscore) | ((row == gscore) & (g < ig))).astype(jnp.int32)
    gsel = beaten < TOPK_GROUP
    work = jnp.where(gsel, sb3, NEG).reshape(N_EXPERTS, n)
    ie = lax.broadcasted_iota(jnp.int32, work.shape, 0)
    chosen = jnp.zeros(work.shape, jnp.bool_)
    for _ in range(TOP_K):
        m = jnp.max(work, axis=0, keepdims=True)
        pick = ie == jnp.min(jnp.where(work == m, ie, N_EXPERTS), axis=0, keepdims=True)
        chosen = chosen | pick
        work = jnp.where(pick, low, work)
    wsel = jnp.where(chosen, s, 0.0)
    return wsel / jnp.sum(wsel, axis=0, keepdims=True) * ROUTED_SCALE


def _merge_kernel(oac_ref, oal_ref, ob_ref, ga_ref, gb_ref, xp_ref, xs_ref, mod_ref, g_ref,
                  wpa_ref, wpb_ref, wo_ref, wrh_ref, wrl_ref, br_ref, wgs_ref, wus_ref, wds_ref,
                  x1s_ref, hf_ref, gates_ref):
    i = pl.program_id(0)
    is_ctx = i < _CTX_TILES
    oa = jnp.where(is_ctx, oac_ref[...], oal_ref[...])
    x = jnp.where(is_ctx, xp_ref[...], xs_ref[...])
    m = mod_ref[0]
    sl = lambda k: m[:, k * D_MODEL:(k + 1) * D_MODEL]
    mix = (_sigmoid(ga_ref[...].astype(F32)) * jnp.dot(oa, wpa_ref[...], preferred_element_type=F32)
           + _sigmoid(gb_ref[...].astype(F32)) * jnp.dot(ob_ref[...], wpb_ref[...], preferred_element_type=F32))
    o = jnp.dot(mix.astype(BF16), wo_ref[...], preferred_element_type=F32)
    x1 = x + sl(2) * o
    h2 = _norm_mod(x1, g_ref[...], sl(3), sl(4))
    for c in range(_FOLD):
        hf_ref[pl.ds(c, _TM, stride=_FOLD), :] = h2[:, c * LANES:(c + 1) * LANES]
    h_hi = h2.astype(BF16)
    a = _silu(jnp.dot(h_hi, wgs_ref[...], preferred_element_type=F32)) * jnp.dot(
        h_hi, wus_ref[...], preferred_element_type=F32)
    x1s_ref[...] = x1 + sl(5) * jnp.dot(a.astype(BF16), wds_ref[...], preferred_element_type=F32)
    h_lo = (h2 - h_hi.astype(F32)).astype(BF16)
    logits = _nt_dot(wrh_ref[...], h_hi) + _nt_dot(wrh_ref[...], h_lo) + _nt_dot(wrl_ref[...], h_hi)
    s = jax.nn.sigmoid(logits)
    gates_ref[...] = _route(s, s + br_ref[...])


def _merge(oac, oal, ob, ga, gb, xp, xs, mod3, g2, wpa, wpb, wo, wrh, wrl, br, wgs, wus, wds):
    row = lambda i: (i, 0)
    const = lambda shape: pl.BlockSpec(shape, lambda i: (0,) * len(shape))
    return pl.pallas_call(
        _merge_kernel,
        grid=(N_TOK // _TM,),
        in_specs=[pl.BlockSpec((_TM, D_ATT), _ctx_map),
                  pl.BlockSpec((_TM, D_ATT), _lat_map),
                  pl.BlockSpec((_TM, D_RNN), row),
                  pl.BlockSpec((_TM, D_MODEL), row),
                  pl.BlockSpec((_TM, D_MODEL), row),
                  pl.BlockSpec((_TM, D_MODEL), _ctx_map),
                  pl.BlockSpec((_TM, D_MODEL), _lat_map),
                  pl.BlockSpec((1, 1, N_MOD * D_MODEL), lambda i: (_mod_row(i, _TM), 0, 0)),
                  const((1, D_MODEL)),
                  const((D_ATT, D_MODEL)), const((D_RNN, D_MODEL)), const((D_MODEL, D_MODEL)),
                  const((N_EXPERTS, D_MODEL)), const((N_EXPERTS, D_MODEL)), const((N_EXPERTS, 1)),
                  const((D_MODEL, D_EXPERT)), const((D_MODEL, D_EXPERT)), const((D_EXPERT, D_MODEL))],
        out_specs=[pl.BlockSpec((_TM, D_MODEL), row),
                   pl.BlockSpec((_TM * _FOLD, LANES), row),
                   pl.BlockSpec((N_EXPERTS, _TM), lambda i: (0, i))],
        out_shape=[jax.ShapeDtypeStruct((N_TOK, D_MODEL), F32),
                   jax.ShapeDtypeStruct((N_TOK * _FOLD, LANES), F32),
                   jax.ShapeDtypeStruct((N_EXPERTS, N_TOK), F32)],
        compiler_params=_params(("parallel",)),
        name="merge",
    )(oac, oal, ob, ga, gb, xp, xs, mod3, g2, wpa, wpb, wo, wrh, wrl, br, wgs, wus, wds)


_FOLD = D_MODEL // LANES
_NH = N_TOK // 2
_R = 512
_SLOTS = 256
_ID_BITS = 13
_DISP_BITS = 12
_VALID_BIT = _ID_BITS + _DISP_BITS + 1


def _chunk_table(cnt):
    shape = (N_EXPERTS, _SLOTS)
    nch = jnp.broadcast_to((cnt + (_R - 1)) // _R, shape)
    erow = lax.broadcasted_iota(jnp.int32, shape, 0)
    q = lax.broadcasted_iota(jnp.int32, shape, 1)
    cum = nch
    s = 1
    while s < N_EXPERTS:
        cum = cum + jnp.where(erow >= s, pltpu.roll(cum, s, 0), 0)
        s *= 2
    total = cum[N_EXPERTS - 1:N_EXPERTS, :]
    used = q[:1] < total
    expert = jnp.minimum(jnp.sum((cum <= q).astype(jnp.int32), axis=0, keepdims=True), N_EXPERTS - 1)
    first = cum - nch
    start = jnp.max(jnp.where(first <= q, first, 0), axis=0, keepdims=True)
    chunk = jnp.where(used, q[:1] - start, 0)
    return jnp.concatenate([expert, chunk, used.astype(jnp.int32), total,
                            jnp.zeros((4, _SLOTS), jnp.int32)], axis=0)


def _dispatch_kernel(g_ref, src_ref, dst_ref, gl_ref, tbl_ref):
    g = g_ref[...]
    n = g.shape[1]
    lane = lax.broadcasted_iota(jnp.int32, g.shape, 1)
    sel = g > 0.0
    m = sel.astype(jnp.int32)
    csum = m
    s = 1
    while s < n:
        csum = csum + jnp.where(lane >= s, pltpu.roll(csum, s, 1), 0)
        s *= 2
    tbl_ref[0] = _chunk_table(csum[:, n - 1:n])
    disp = lane - (csum - m)
    packed = jnp.where(sel, (1 << _VALID_BIT) | (disp << _ID_BITS) | lane, 0)
    g = jnp.where(sel, g, 0.0)
    for b in range(_DISP_BITS):
        shift = n - (1 << b)
        inc = pltpu.roll(packed, shift, 1)
        inc_moves = ((inc >> (_ID_BITS + b)) & 1) == 1
        own_moves = ((packed >> (_ID_BITS + b)) & 1) == 1
        packed = jnp.where(inc_moves, inc, jnp.where(own_moves, 0, packed))
        g = jnp.where(inc_moves, pltpu.roll(g, shift, 1), jnp.where(own_moves, 0.0, g))
    valid = (packed >> _VALID_BIT) == 1
    tok = packed & ((1 << _ID_BITS) - 1)
    src_ref[0] = jnp.where(valid, tok, 0) * _FOLD
    dst_ref[0] = jnp.where(valid, tok, n) * _FOLD
    gl_ref[0] = g


def _dispatch(gates_t):
    halves = N_TOK // _NH
    blk = pl.BlockSpec((1, N_EXPERTS, _NH), lambda h: (h, 0, 0))
    lists = jax.ShapeDtypeStruct((halves, N_EXPERTS, _NH), jnp.int32)
    return pl.pallas_call(
        _dispatch_kernel,
        grid=(halves,),
        in_specs=[pl.BlockSpec((N_EXPERTS, _NH), lambda h: (0, h))],
        out_specs=[blk, blk, blk, pl.BlockSpec((1, 8, _SLOTS), lambda h: (h, 0, 0))],
        out_shape=[lists, lists, jax.ShapeDtypeStruct((halves, N_EXPERTS, _NH), F32),
                   jax.ShapeDtypeStruct((halves, 8, _SLOTS), jnp.int32)],
        compiler_params=_params(("parallel",)),
        name="dispatch",
    )(gates_t)


_GRP = 8
_MXU = 256
_KT = D_MODEL // _MXU
_UNF = 64
_TBL = 8 * _SLOTS
_STEPS = _NH * TOP_K // _R + N_EXPERTS + 2


def _tbl(tbl_ref, h, row, k):
    return tbl_ref[h * _TBL + row * _SLOTS + jnp.clip(k, 0, _SLOTS - 1)]


def _list_block(h, k, tbl_ref):
    per_list = _NH // _R
    return ((h * N_EXPERTS + _tbl(tbl_ref, h, 0, k)) * per_list + _tbl(tbl_ref, h, 1, k),)


def _moe_kernel(tbl_ref, src_ref, dst_ref, gate_ref, hf_ref, wg_ref, wu_ref, wd_ref, out_hbm,
                acc_ref, wgb_ref, wub_ref, wdb_ref, xg_ref, xa_ref, xb_ref, ya_ref, yb_ref, sem):
    h = pl.program_id(0)
    t = pl.program_id(1)
    total = _tbl(tbl_ref, h, 3, 0)

    @pl.when(t == 0)
    def _():
        for ref in (acc_ref, ya_ref, yb_ref):
            ref[...] = jnp.zeros_like(ref)

    @pl.when((h == 0) & (t == 0))
    def _():
        for ref in (xa_ref, xb_ref, wgb_ref, wub_ref, wdb_ref):
            ref[...] = jnp.zeros_like(ref)

    mm_used = (t >= 1) & (_tbl(tbl_ref, h, 2, t - 1) == 1)

    @pl.when(mm_used & (_tbl(tbl_ref, h, 1, t - 1) == 0))
    def _():
        wgb_ref[...] = wg_ref[0].astype(BF16)
        wub_ref[...] = wu_ref[0].astype(BF16)
        wdb_ref[...] = wd_ref[0].astype(BF16)

    def row(ref, r):
        return ref.at[pl.ds(pl.multiple_of(r, _FOLD), _FOLD), :]

    def stages(x_next, x_mm, y_mm, y_scatter):
        def gather_group(j0):
            for j in range(j0, j0 + _GRP):
                row(xg_ref, j * _FOLD)[...] = row(hf_ref, src_ref[j])[...]

        def scatter_group(j0):
            dsts = [row(acc_ref, dst_ref[j0 + u]) for u in range(_GRP)]
            news = [dsts[u][...] + row(y_scatter, (j0 + u) * _FOLD)[...] for u in range(_GRP)]
            for u in range(_GRP):
                dsts[u][...] = news[u]

        def unfold_block(r0):
            for k in range(_FOLD):
                x_next[r0:r0 + _UNF, k * LANES:(k + 1) * LANES] = xg_ref[
                    pl.ds(r0 * _FOLD + k, _UNF, stride=_FOLD), :].astype(BF16)

        moves = []
        for j0 in range(0, _R, _GRP):
            moves += [functools.partial(scatter_group, j0), functools.partial(gather_group, j0)]
            if (j0 + _GRP) % _UNF == 0:
                moves.append(functools.partial(unfold_block, j0 + _GRP - _UNF))
        n_pieces = 3 * _KT
        done = [0]

        def some_moves():
            done[0] += 1
            for f in moves[(done[0] - 1) * len(moves) // n_pieces:done[0] * len(moves) // n_pieces]:
                f()

        diag = (lax.broadcasted_iota(jnp.int32, (_R, _R), 0) == lax.broadcasted_iota(jnp.int32, (_R, _R), 1))
        gates = gate_ref[0] * mm_used.astype(F32)
        gcol = jnp.sum(jnp.where(diag, gates, 0.0), axis=1, keepdims=True)
        gt = up = None
        for kt in range(_KT):
            rows_k = slice(kt * _MXU, (kt + 1) * _MXU)
            xk = x_mm[:, rows_k]
            d = jnp.dot(xk, wgb_ref[rows_k, :], preferred_element_type=F32)
            gt = d if gt is None else gt + d
            some_moves()
            d = jnp.dot(xk, wub_ref[rows_k, :], preferred_element_type=F32)
            up = d if up is None else up + d
            some_moves()
        act = (_silu(gt) * up * gcol).astype(BF16)
        for nt in range(D_MODEL // _MXU):
            y = jnp.dot(act, wdb_ref[:, nt * _MXU:(nt + 1) * _MXU], preferred_element_type=F32)
            some_moves()
            for k in range(_MXU // LANES):
                y_mm[pl.ds(nt * _MXU // LANES + k, _R, stride=_FOLD), :] = y[:, k * LANES:(k + 1) * LANES]

    busy = t < total + 2

    @pl.when(busy & (t % 2 == 0))
    def _():
        stages(xa_ref, xb_ref, yb_ref, ya_ref)

    @pl.when(busy & (t % 2 == 1))
    def _():
        stages(xb_ref, xa_ref, ya_ref, yb_ref)

    @pl.when(t == _STEPS - 1)
    def _():
        rows = _NH * _FOLD
        cp = pltpu.make_async_copy(acc_ref.at[pl.ds(0, rows)],
                                   out_hbm.at[pl.ds(pl.multiple_of(h * rows, rows), rows)], sem)
        cp.start()
        cp.wait()


def _moe(tbl, src, dst, gl, hfold, wg, wu, wd):
    halves = N_TOK // _NH
    smem = lambda lag: pl.BlockSpec((_R,), lambda h, t, tbl_ref: _list_block(h, t - lag, tbl_ref),
                                    memory_space=pltpu.SMEM)
    wmap = lambda h, t, tbl_ref: (_tbl(tbl_ref, h, 0, t - 1), 0, 0)
    stage = pltpu.VMEM((_R * _FOLD, LANES), F32)
    packed = pltpu.VMEM((_R, D_MODEL), BF16)
    return pl.pallas_call(
        _moe_kernel,
        grid_spec=pltpu.PrefetchScalarGridSpec(
            num_scalar_prefetch=1,
            grid=(halves, _STEPS),
            in_specs=[smem(0), smem(2),
                      pl.BlockSpec((1, 1, _R), lambda h, t, tbl_ref: _list_block(h, t - 1, tbl_ref) + (0, 0)),
                      pl.BlockSpec((_NH * _FOLD, LANES), lambda h, t, tbl_ref: (h, 0)),
                      pl.BlockSpec((1, D_MODEL, D_EXPERT), wmap),
                      pl.BlockSpec((1, D_MODEL, D_EXPERT), wmap),
                      pl.BlockSpec((1, D_EXPERT, D_MODEL), wmap)],
            out_specs=pl.BlockSpec(memory_space=pl.ANY),
            scratch_shapes=[pltpu.VMEM(((_NH + 1) * _FOLD, LANES), F32),
                            pltpu.VMEM((D_MODEL, D_EXPERT), BF16),
                            pltpu.VMEM((D_MODEL, D_EXPERT), BF16),
                            pltpu.VMEM((D_EXPERT, D_MODEL), BF16),
                            stage, packed, packed, stage, stage,
                            pltpu.SemaphoreType.DMA]),
        out_shape=jax.ShapeDtypeStruct((N_TOK * _FOLD, LANES), F32),
        compiler_params=_params(("arbitrary", "arbitrary")),
        name="moe",
    )(tbl, src, dst, gl, hfold, wg, wu, wd)


def _final_kernel(x_ref, r_ref, mod_ref, fg_ref, yp_ref, ys_ref):
    i = pl.program_id(0)
    routed = jnp.concatenate([r_ref[pl.ds(k, _TM, stride=_FOLD), :] for k in range(_FOLD)], axis=1)
    m = mod_ref[0]
    x2 = x_ref[...] + m[:, 5 * D_MODEL:6 * D_MODEL] * routed
    y = x2 * lax.rsqrt(jnp.mean(x2 * x2, axis=-1, keepdims=True) + EPS) * fg_ref[...]

    @pl.when(i < _CTX_TILES)
    def _():
        yp_ref[...] = y

    @pl.when(i >= _CTX_TILES)
    def _():
        ys_ref[...] = y


def _final(x1s, routed, mod3, fg):
    return pl.pallas_call(
        _final_kernel,
        grid=(N_TOK // _TM,),
        in_specs=[pl.BlockSpec((_TM, D_MODEL), lambda i: (i, 0)),
                  pl.BlockSpec((_TM * _FOLD, LANES), lambda i: (i, 0)),
                  pl.BlockSpec((1, 1, N_MOD * D_MODEL), lambda i: (_mod_row(i, _TM), 0, 0)),
                  pl.BlockSpec((1, D_MODEL), lambda i: (0, 0))],
        out_specs=[pl.BlockSpec((_TM, D_MODEL), _ctx_map),
                   pl.BlockSpec((_TM, D_MODEL), _lat_map)],
        out_shape=[jax.ShapeDtypeStruct((N_CTX, D_MODEL), F32),
                   jax.ShapeDtypeStruct((N_LAT, D_MODEL), F32)],
        compiler_params=_params(("arbitrary",)),
        name="final",
    )(x1s, routed, mod3, fg)


def kernel(x_prompt, x_sample, cache_k, cache_v, state_h_fwd, state_h_bwd, c, c_ctx, norm1_g, norm2_g, final_g, w_ada, b_ada, w_in, rpb, conv_w, conv_b, rg_wa, rg_ba, rg_wx, rg_bx, rg_lam, w_pa, w_pb, w_out, w_router, b_router, w_gate_e, w_up_e, w_down_e, w_gate_s, w_up_s, w_down_s):
    l = 0
    xp = x_prompt.reshape(N_CTX, D_MODEL)
    xs = x_sample.reshape(N_LAT, D_MODEL)

    cvecs = jnp.concatenate([c_ctx[None, :], c], axis=0)[:, :, None]
    mod = _adaln(cvecs, w_ada[l], b_ada[l][None, :])
    mod3 = mod.reshape(8, 1, N_MOD * D_MODEL)

    q, kc, vc, kl, vl, xr, yg, ga, gb = _inproj(xp, xs, mod3, norm1_g[l][None, :], w_in[l].astype(BF16))

    oac = _attn_ctx(q, kc, vc)
    ck = cache_k[:, l].reshape(DEC_BATCH, PAST_LEN, D_ATT)
    cv = cache_v[:, l].reshape(DEC_BATCH, PAST_LEN, D_ATT)
    oal = _attn_lat(q, kl, vl, ck, cv, _bias_tables(rpb[l]))

    w4 = jnp.concatenate([_block_diag(rg_wa[l, 0]), _block_diag(rg_wx[l, 0]),
                          _block_diag(rg_wa[l, 1]), _block_diag(rg_wx[l, 1])], axis=1).astype(BF16)
    b4 = jnp.concatenate([rg_ba[l, 0], rg_bx[l, 0], rg_ba[l, 1], rg_bx[l, 1]])[None, :]
    ob, hlf, hlb = _rglru(xr, yg, conv_w[l], conv_b[l][None, :], w4, b4, rg_lam[l],
                          state_h_fwd[:, l], state_h_bwd[:, l])

    wr_t = w_router[l].T
    wr_hi = wr_t.astype(BF16)
    wr_lo = (wr_t - wr_hi.astype(F32)).astype(BF16)
    x1s, hfold, gates_t = _merge(oac, oal, ob, ga, gb, xp, xs, mod3, norm2_g[l][None, :],
                                 w_pa[l].astype(BF16), w_pb[l].astype(BF16), w_out[l].astype(BF16),
                                 wr_hi, wr_lo, b_router[l][:, None],
                                 w_gate_s[l].astype(BF16), w_up_s[l].astype(BF16), w_down_s[l].astype(BF16))

    src, dst, gl, tbl = _dispatch(gates_t)
    routed = _moe(tbl.reshape(-1), src.reshape(-1), dst.reshape(-1), gl.reshape(-1, 1, _R), hfold,
                  w_gate_e[l], w_up_e[l], w_down_e[l])
    yp, ys = _final(x1s, routed, mod3, final_g[None, :])

    return (yp.reshape(BATCH, SEQ, D_MODEL),
            ys.reshape(DEC_BATCH, DEC_SEQ, D_MODEL),
            kc.reshape(BATCH, 1, SEQ, N_HEADS_A, HEAD_DIM_A),
            vc.reshape(BATCH, 1, SEQ, N_HEADS_A, HEAD_DIM_A),
            hlf.reshape(BATCH, 1, D_RNN),
            hlb.reshape(BATCH, 1, D_RNN))
```

```python
import functools

import jax
import jax.numpy as jnp
from jax import lax
from jax.experimental import pallas as pl
from jax.experimental.pallas import tpu as pltpu

F32 = jnp.float32
BF16 = jnp.bfloat16

D_MODEL = 1024
BATCH = 16
SEQ = 256
DEC_BATCH = 2
DEC_SEQ = 1024
PAST_LEN = 512
GRID_W = 64
N_HEADS_A = 8
HEAD_DIM_A = 64
D_ATT = N_HEADS_A * HEAD_DIM_A
KH = 8
KW = 16
D_RNN = 512
N_RG_BLOCKS = 8
CONV_W = 4
RG_C = 8.0
N_EXPERTS = 64
TOP_K = 8
N_GROUPS = 8
GROUP_SIZE = N_EXPERTS // N_GROUPS
TOPK_GROUP = 4
D_EXPERT = 256
ROUTED_SCALE = 2.5
N_MOD = 6
EPS = 1e-6
NEG = -1e30
D_IN = 3 * D_ATT + 2 * D_RNN + 2 * D_MODEL

N_CTX = BATCH * SEQ
N_LAT = DEC_BATCH * DEC_SEQ
N_TOK = N_CTX + N_LAT
GRID_ROWS = DEC_SEQ // GRID_W

LANES = 128
VMEM_LIMIT = 56 * 1024 * 1024


def _params(sem, vmem=VMEM_LIMIT):
    return pltpu.CompilerParams(dimension_semantics=sem, vmem_limit_bytes=vmem)


def _mod_row(i, tile_rows):
    n_ctx_tiles = N_CTX // tile_rows
    return jnp.where(i < n_ctx_tiles, 0, 1 + (i - n_ctx_tiles) // (DEC_SEQ // tile_rows))


def _norm_mod(x, g, shift, scale):
    y = x * lax.rsqrt(jnp.mean(x * x, axis=-1, keepdims=True) + EPS)
    return (y * g) * (1.0 + scale) + shift


def _sigmoid(x):
    return 0.5 * jnp.tanh(0.5 * x) + 0.5


def _silu(x):
    return x * _sigmoid(x)


def _adaln_kernel(c_ref, w_ref, b_ref, o_ref):
    w = w_ref[...]
    rows = []
    for r in range(3):
        s = _silu(c_ref[r])
        rows.append(jnp.sum(w * s, axis=0, keepdims=True))
    rows.append(jnp.zeros((5, w.shape[1]), F32))
    o_ref[...] = jnp.concatenate(rows, axis=0) + b_ref[...]


def _adaln(cvecs, w_ada, b_ada):
    tn = 512
    n = N_MOD * D_MODEL
    return pl.pallas_call(
        _adaln_kernel,
        grid=(n // tn,),
        in_specs=[pl.BlockSpec((3, D_MODEL, 1), lambda j: (0, 0, 0)),
                  pl.BlockSpec((D_MODEL, tn), lambda j: (0, j)),
                  pl.BlockSpec((1, tn), lambda j: (0, j))],
        out_specs=pl.BlockSpec((8, tn), lambda j: (0, j)),
        out_shape=jax.ShapeDtypeStruct((8, n), F32),
        compiler_params=_params(("parallel",)),
        name="adaln",
    )(cvecs, w_ada, b_ada)


_TM = 512
_CTX_TILES = N_CTX // _TM
_LAT_TILES = N_LAT // _TM


def _ctx_map(i):
    return (jnp.minimum(i, _CTX_TILES - 1), 0)


def _lat_map(i):
    return (jnp.maximum(i - _CTX_TILES, 0), 0)


def _inproj_kernel(xp_ref, xs_ref, mod_ref, g_ref, w_ref,
                   q_ref, kc_ref, vc_ref, kl_ref, vl_ref, xr_ref, yg_ref, ga_ref, gb_ref):
    i = pl.program_id(0)
    is_ctx = i < _CTX_TILES
    x = jnp.where(is_ctx, xp_ref[...], xs_ref[...])
    m = mod_ref[0]
    h = _norm_mod(x, g_ref[...], m[:, 0:D_MODEL], m[:, D_MODEL:2 * D_MODEL]).astype(BF16)

    def proj(a, b):
        return jnp.dot(h, w_ref[:, a:b], preferred_element_type=F32)

    q_ref[...] = proj(0, D_ATT).astype(BF16)
    k = proj(D_ATT, 2 * D_ATT)
    v = proj(2 * D_ATT, 3 * D_ATT)

    @pl.when(is_ctx)
    def _():
        kc_ref[...] = k
        vc_ref[...] = v

    @pl.when(jnp.logical_not(is_ctx))
    def _():
        kl_ref[...] = k
        vl_ref[...] = v

    o = 3 * D_ATT
    xr_ref[...] = proj(o, o + D_RNN)
    yg_ref[...] = proj(o + D_RNN, o + 2 * D_RNN).astype(BF16)
    o += 2 * D_RNN
    ga_ref[...] = proj(o, o + D_MODEL).astype(BF16)
    gb_ref[...] = proj(o + D_MODEL, o + 2 * D_MODEL).astype(BF16)


def _inproj(xp, xs, mod3, g1, w_in):
    row = lambda i: (i, 0)
    return pl.pallas_call(
        _inproj_kernel,
        grid=(N_TOK // _TM,),
        in_specs=[pl.BlockSpec((_TM, D_MODEL), _ctx_map),
                  pl.BlockSpec((_TM, D_MODEL), _lat_map),
                  pl.BlockSpec((1, 1, N_MOD * D_MODEL), lambda i: (_mod_row(i, _TM), 0, 0)),
                  pl.BlockSpec((1, D_MODEL), lambda i: (0, 0)),
                  pl.BlockSpec((D_MODEL, D_IN), lambda i: (0, 0))],
        out_specs=[pl.BlockSpec((_TM, D_ATT), row),
                   pl.BlockSpec((_TM, D_ATT), _ctx_map),
                   pl.BlockSpec((_TM, D_ATT), _ctx_map),
                   pl.BlockSpec((_TM, D_ATT), _lat_map),
                   pl.BlockSpec((_TM, D_ATT), _lat_map),
                   pl.BlockSpec((_TM, D_RNN), row),
                   pl.BlockSpec((_TM, D_RNN), row),
                   pl.BlockSpec((_TM, D_MODEL), row),
                   pl.BlockSpec((_TM, D_MODEL), row)],
        out_shape=[jax.ShapeDtypeStruct((N_TOK, D_ATT), BF16),
                   jax.ShapeDtypeStruct((N_CTX, D_ATT), F32),
                   jax.ShapeDtypeStruct((N_CTX, D_ATT), F32),
                   jax.ShapeDtypeStruct((N_LAT, D_ATT), F32),
                   jax.ShapeDtypeStruct((N_LAT, D_ATT), F32),
                   jax.ShapeDtypeStruct((N_TOK, D_RNN), F32),
                   jax.ShapeDtypeStruct((N_TOK, D_RNN), BF16),
                   jax.ShapeDtypeStruct((N_TOK, D_MODEL), BF16),
                   jax.ShapeDtypeStruct((N_TOK, D_MODEL), BF16)],
        compiler_params=_params(("arbitrary",)),
        name="inproj",
    )(xp, xs, mod3, g1, w_in)


def _pair_attention(qp, segs):
    scale = HEAD_DIM_A ** -0.5
    lane = lax.broadcasted_iota(jnp.int32, (1, LANES), 1)
    outs = []
    for a in range(2):
        sel = (lane >= HEAD_DIM_A) if a else (lane < HEAD_DIM_A)
        qm = jnp.where(sel, qp, jnp.zeros_like(qp))
        ss = []
        for kp, _, bias_fn in segs:
            s = lax.dot_general(qm, kp, (((1,), (1,)), ((), ())), preferred_element_type=F32) * scale
            if bias_fn is not None:
                s = s + bias_fn(a)
            ss.append(s)
        m = functools.reduce(jnp.maximum, [jnp.max(s, axis=-1, keepdims=True) for s in ss])
        es = [jnp.exp(s - m) for s in ss]
        inv = 1.0 / functools.reduce(jnp.add, [jnp.sum(e, axis=-1, keepdims=True) for e in es])
        o = functools.reduce(jnp.add, [
            jnp.dot((e * inv).astype(BF16), vp, preferred_element_type=F32)
            for e, (_, vp, _) in zip(es, segs)])
        outs.append(o)
    return jnp.where(lane < HEAD_DIM_A, outs[0], outs[1])


def _attn_ctx_kernel(q_ref, k_ref, v_ref, o_ref):
    for j in range(D_ATT // LANES):
        c = slice(j * LANES, (j + 1) * LANES)
        segs = [(k_ref[:, c].astype(BF16), v_ref[:, c].astype(BF16), None)]
        o_ref[:, c] = _pair_attention(q_ref[:, c], segs).astype(BF16)


def _attn_ctx(q_all, kc, vc):
    blk = pl.BlockSpec((SEQ, D_ATT), lambda b: (b, 0))
    return pl.pallas_call(
        _attn_ctx_kernel,
        grid=(BATCH,),
        in_specs=[blk, blk, blk],
        out_specs=blk,
        out_shape=jax.ShapeDtypeStruct((N_CTX, D_ATT), BF16),
        compiler_params=_params(("parallel",)),
        name="attn_ctx",
    )(q_all, kc, vc)


_QROWS = 4
_QB = _QROWS * GRID_W
_KROWS = 12
_KB = _KROWS * GRID_W


def _build_bias_tables(rows_ref, t2_ref):
    wq = lax.broadcasted_iota(jnp.int32, (GRID_W, LANES), 0)
    wk = lax.broadcasted_iota(jnp.int32, (GRID_W, LANES), 1) & (GRID_W - 1)
    col_start = jnp.clip(wq - KW // 2, 0, GRID_W - KW)
    col_in = (wk >= col_start) & (wk < col_start + KW)
    for r in range(N_HEADS_A * 2 * KH):
        x = jnp.broadcast_to(rows_ref[r:r + 1, :], (GRID_W, LANES))
        x = pltpu.roll(x, LANES - (KW - 1), 1, stride=1, stride_axis=0)
        t2_ref[r // (2 * KH), r % (2 * KH)] = jnp.where(col_in, x, NEG)


def _attn_lat_kernel(q_ref, k_ref, v_ref, ck_ref, cv_ref, rows_ref, o_ref, t2_ref):
    i = pl.program_id(1)

    @pl.when((pl.program_id(0) == 0) & (i == 0))
    def _():
        _build_bias_tables(rows_ref, t2_ref)

    ks = jnp.where(i < 2, 0, GRID_ROWS - _KROWS)
    kstart = pl.multiple_of(ks * GRID_W, 256)
    lane = lax.broadcasted_iota(jnp.int32, (1, LANES), 1)

    def bias_for_head(h):
        rows = []
        for rq in range(_QROWS):
            r = _QROWS * i + rq
            rs = jnp.clip(r - KH // 2, 0, GRID_ROWS - KH)
            tiles = []
            for t in range(_KROWS // 2):
                kr = ks + 2 * t
                d = jnp.clip(kr - r + (KH - 1), -1, 2 * KH - 2)
                tile = t2_ref[h, d + 1]
                v0 = ((kr >= rs) & (kr < rs + KH)).astype(jnp.int32)
                v1 = ((kr + 1 >= rs) & (kr + 1 < rs + KH)).astype(jnp.int32)
                valid = jnp.where(lane < GRID_W, v0, v1) > 0
                tiles.append(jnp.where(valid, tile, NEG))
            rows.append(jnp.concatenate(tiles, axis=1))
        return jnp.concatenate(rows, axis=0)

    for j in range(D_ATT // LANES):
        c = slice(j * LANES, (j + 1) * LANES)
        k_loc = k_ref[pl.ds(kstart, _KB), c].astype(BF16)
        v_loc = v_ref[pl.ds(kstart, _KB), c].astype(BF16)
        segs = [(k_loc, v_loc, lambda a, j=j: bias_for_head(2 * j + a)),
                (ck_ref[:, c].astype(BF16), cv_ref[:, c].astype(BF16), None)]
        o_ref[:, c] = _pair_attention(q_ref[:, c], segs).astype(BF16)


def _attn_lat(q_all, kl, vl, ck, cv, t2):
    qblocks = DEC_SEQ // _QB
    return pl.pallas_call(
        _attn_lat_kernel,
        grid=(DEC_BATCH, qblocks),
        in_specs=[pl.BlockSpec((_QB, D_ATT), lambda b, i: (N_CTX // _QB + b * qblocks + i, 0)),
                  pl.BlockSpec((DEC_SEQ, D_ATT), lambda b, i: (b, 0)),
                  pl.BlockSpec((DEC_SEQ, D_ATT), lambda b, i: (b, 0)),
                  pl.BlockSpec((None, PAST_LEN, D_ATT), lambda b, i: (b, 0, 0)),
                  pl.BlockSpec((None, PAST_LEN, D_ATT), lambda b, i: (b, 0, 0)),
                  pl.BlockSpec((N_HEADS_A * 2 * KH, LANES), lambda b, i: (0, 0))],
        out_specs=pl.BlockSpec((_QB, D_ATT), lambda b, i: (b * qblocks + i, 0)),
        out_shape=jax.ShapeDtypeStruct((N_LAT, D_ATT), BF16),
        scratch_shapes=[pltpu.VMEM((N_HEADS_A, 2 * KH, GRID_W, LANES), F32)],
        compiler_params=_params(("arbitrary", "arbitrary")),
        name="attn_lat",
    )(q_all, kl, vl, ck, cv, t2)


def _bias_tables(rpb):
    half = jnp.pad(rpb, ((0, 0), (0, 0), (0, GRID_W - rpb.shape[-1])))
    neg = jnp.full((N_HEADS_A, 1, GRID_W), NEG, F32)
    left = jnp.concatenate([neg, half], axis=1)
    right = jnp.concatenate([half, neg], axis=1)
    return jnp.concatenate([left, right], axis=-1).reshape(N_HEADS_A * 2 * KH, LANES)


_RG_ROWS = 2048
_RG_CHUNK = 256
_RG_SLABS = D_RNN // LANES
_RG_CTX_STEPS = N_CTX // _RG_ROWS


def _rg_block(nb, T, interleaved, xr_ref, yg_ref, cw_ref, cb_ref, w4_ref, b4_ref, lam_ref, h0f, h0b,
              ob_ref, xc_ref, af_ref, uf_ref, ab_ref, ub_ref, hf_ref, hb_ref):
    R = nb * T
    xr = xr_ref[...]
    t_idx = lax.broadcasted_iota(jnp.int32, (R, 1), 0) & (T - 1)
    cw = cw_ref[...]
    xc = jnp.zeros((R, D_RNN), F32) + cb_ref[...]
    for j in range(CONV_W):
        off = j - 2
        if off == 0:
            tap = xr
        else:
            tap = pltpu.roll(xr, shift=(-off) % R, axis=0)
            tap = jnp.where((t_idx + off >= 0) & (t_idx + off < T), tap, 0.0)
        xc = xc + cw[j:j + 1, :] * tap
    xc_ref[...] = xc

    lam = lam_ref[...]
    z = -lam
    sp = jnp.maximum(z, 0.0) + jnp.log1p(jnp.exp(-jnp.abs(z)))

    def chunk_rows(c):
        if interleaved:
            return pl.ds(c, _RG_CHUNK, stride=nb)
        return pl.ds(pl.multiple_of(c * _RG_CHUNK, _RG_CHUNK), _RG_CHUNK)

    def put(ref, c, val):
        for k in range(_RG_SLABS):
            ref[k, chunk_rows(c), :] = val[:, k * LANES:(k + 1) * LANES]

    def get(ref, c):
        return jnp.concatenate([ref[k, chunk_rows(c), :] for k in range(_RG_SLABS)], axis=1)

    def gate_chunk(c, carry):
        x = xc_ref[pl.ds(pl.multiple_of(c * _RG_CHUNK, _RG_CHUNK), _RG_CHUNK), :]
        g4 = jnp.dot(x.astype(BF16), w4_ref[...], preferred_element_type=F32) + b4_ref[...]
        for d, (a_ref, u_ref) in enumerate(((af_ref, uf_ref), (ab_ref, ub_ref))):
            r = _sigmoid(g4[:, (2 * d) * D_RNN:(2 * d + 1) * D_RNN])
            ig = _sigmoid(g4[:, (2 * d + 1) * D_RNN:(2 * d + 2) * D_RNN])
            a = jnp.exp(-RG_C * r * sp[d:d + 1, :])
            put(a_ref, c, a)
            put(u_ref, c, jnp.sqrt(jnp.maximum(1.0 - a * a, 0.0)) * (ig * x))
        return carry

    lax.fori_loop(0, R // _RG_CHUNK, gate_chunk, 0)

    if interleaved:
        groups = [lambda t: pl.ds(pl.multiple_of(t * nb, nb), nb)]
    else:
        groups = [lambda t, bi=bi: pl.ds(bi * T + t, 1) for bi in range(nb)]

    def step(t, carry):
        hf, hb = carry
        nf, nbk = [], []
        for gi, rows_at in enumerate(groups):
            rf, rb = rows_at(t), rows_at(T - 1 - t)
            for k in range(_RG_SLABS):
                n = gi * _RG_SLABS + k
                f = af_ref[k, rf, :] * hf[n] + uf_ref[k, rf, :]
                b = ab_ref[k, rb, :] * hb[n] + ub_ref[k, rb, :]
                hf_ref[k, rf, :] = f
                hb_ref[k, rb, :] = b
                nf.append(f)
                nbk.append(b)
        return tuple(nf), tuple(nbk)

    hf, hb = lax.fori_loop(0, T, step, (tuple(h0f), tuple(h0b)))

    def out_chunk(c, carry):
        rows = pl.ds(pl.multiple_of(c * _RG_CHUNK, _RG_CHUNK), _RG_CHUNK)
        y = yg_ref[rows, :].astype(F32)
        ob_ref[rows, :] = ((get(hf_ref, c) + get(hb_ref, c)) * jax.nn.gelu(y)).astype(BF16)
        return carry

    lax.fori_loop(0, R // _RG_CHUNK, out_chunk, 0)
    return hf, hb


def _rglru_kernel(xr_ref, yg_ref, cw_ref, cb_ref, w4_ref, b4_ref, lam_ref, sf_ref, sb_ref,
                  ob_ref, hlf_ref, hlb_ref, *scratch):
    i = pl.program_id(0)
    shared = (xr_ref, yg_ref, cw_ref, cb_ref, w4_ref, b4_ref, lam_ref)
    nb_ctx = _RG_ROWS // SEQ

    @pl.when(i < _RG_CTX_STEPS)
    def _():
        zero = [jnp.zeros((nb_ctx, LANES), F32)] * _RG_SLABS
        hf, hb = _rg_block(nb_ctx, SEQ, True, *shared, zero, zero, ob_ref, *scratch)
        hlf_ref[...] = jnp.concatenate(hf, axis=1)
        hlb_ref[...] = jnp.concatenate(hb, axis=1)

    @pl.when(i >= _RG_CTX_STEPS)
    def _():
        slabs = lambda ref: [ref[bi:bi + 1, k * LANES:(k + 1) * LANES]
                             for bi in range(DEC_BATCH) for k in range(_RG_SLABS)]
        _rg_block(DEC_BATCH, DEC_SEQ, False, *shared, slabs(sf_ref), slabs(sb_ref), ob_ref, *scratch)


def _rglru(xr, yg, conv_w, conv_b, w4, b4, lam, sf, sb):
    steps = N_TOK // _RG_ROWS
    nb_ctx = _RG_ROWS // SEQ
    const = lambda shape: pl.BlockSpec(shape, lambda i: (0,) * len(shape))
    hl_map = lambda i: (jnp.minimum(i, _RG_CTX_STEPS - 1), 0)
    scan_buf = pltpu.VMEM((_RG_SLABS, _RG_ROWS, LANES), F32)
    return pl.pallas_call(
        _rglru_kernel,
        grid=(steps,),
        in_specs=[pl.BlockSpec((_RG_ROWS, D_RNN), lambda i: (i, 0)),
                  pl.BlockSpec((_RG_ROWS, D_RNN), lambda i: (i, 0)),
                  const((CONV_W, D_RNN)), const((1, D_RNN)),
                  const((D_RNN, 4 * D_RNN)), const((1, 4 * D_RNN)), const((2, D_RNN)),
                  const((DEC_BATCH, D_RNN)), const((DEC_BATCH, D_RNN))],
        out_specs=[pl.BlockSpec((_RG_ROWS, D_RNN), lambda i: (i, 0)),
                   pl.BlockSpec((nb_ctx, D_RNN), hl_map),
                   pl.BlockSpec((nb_ctx, D_RNN), hl_map)],
        out_shape=[jax.ShapeDtypeStruct((N_TOK, D_RNN), BF16),
                   jax.ShapeDtypeStruct((BATCH, D_RNN), F32),
                   jax.ShapeDtypeStruct((BATCH, D_RNN), F32)],
        scratch_shapes=[pltpu.VMEM((_RG_ROWS, D_RNN), F32)] + [scan_buf] * 6,
        compiler_params=_params(("arbitrary",)),
        name="rglru",
    )(xr, yg, conv_w, conv_b, w4, b4, lam, sf, sb)


def _block_diag(w):
    n, d, _ = w.shape
    eye = jnp.eye(n, dtype=w.dtype)
    return jnp.einsum('nij,nm->nimj', w, eye).reshape(n * d, n * d)


def _nt_dot(a, b):
    return lax.dot_general(a, b, (((1,), (1,)), ((), ())), preferred_element_type=F32)


def _route(s, sb):
    n = s.shape[1]
    low = -3.0e38
    sb3 = sb.reshape(N_GROUPS, GROUP_SIZE, n)
    iw = lax.broadcasted_iota(jnp.int32, sb3.shape, 1)
    m1 = jnp.max(sb3, axis=1, keepdims=True)
    i1 = jnp.min(jnp.where(sb3 == m1, iw, GROUP_SIZE), axis=1, keepdims=True)
    m2 = jnp.max(jnp.where(iw == i1, low, sb3), axis=1, keepdims=True)
    gscore = m1 + m2
    ig = lax.broadcasted_iota(jnp.int32, gscore.shape, 0)
    beaten = jnp.zeros(gscore.shape, jnp.int32)
    for g in range(N_GROUPS):
        row = gscore[g:g + 1]
        beaten = beaten + ((row > gscore) | ((row == gscore) & (g < ig))).astype(jnp.int32)
    gsel = beaten < TOPK_GROUP
    work = jnp.where(gsel, sb3, NEG).reshape(N_EXPERTS, n)
    ie = lax.broadcasted_iota(jnp.int32, work.shape, 0)
    chosen = jnp.zeros(work.shape, jnp.bool_)
    for _ in range(TOP_K):
        m = jnp.max(work, axis=0, keepdims=True)
        pick = ie == jnp.min(jnp.where(work == m, ie, N_EXPERTS), axis=0, keepdims=True)
        chosen = chosen | pick
        work = jnp.where(pick, low, work)
    wsel = jnp.where(chosen, s, 0.0)
    return wsel / jnp.sum(wsel, axis=0, keepdims=True) * ROUTED_SCALE


def _merge_kernel(oac_ref, oal_ref, ob_ref, ga_ref, gb_ref, xp_ref, xs_ref, mod_ref, g_ref,
                  wpa_ref, wpb_ref, wo_ref, wrh_ref, wrl_ref, br_ref, wgs_ref, wus_ref, wds_ref,
                  x1s_ref, hf_ref, gates_ref):
    i = pl.program_id(0)
    is_ctx = i < _CTX_TILES
    oa = jnp.where(is_ctx, oac_ref[...], oal_ref[...])
    x = jnp.where(is_ctx, xp_ref[...], xs_ref[...])
    m = mod_ref[0]
    sl = lambda k: m[:, k * D_MODEL:(k + 1) * D_MODEL]
    mix = (_sigmoid(ga_ref[...].astype(F32)) * jnp.dot(oa, wpa_ref[...], preferred_element_type=F32)
           + _sigmoid(gb_ref[...].astype(F32)) * jnp.dot(ob_ref[...], wpb_ref[...], preferred_element_type=F32))
    o = jnp.dot(mix.astype(BF16), wo_ref[...], preferred_element_type=F32)
    x1 = x + sl(2) * o
    h2 = _norm_mod(x1, g_ref[...], sl(3), sl(4))
    for c in range(_FOLD):
        hf_ref[pl.ds(c, _TM, stride=_FOLD), :] = h2[:, c * LANES:(c + 1) * LANES]
    h_hi = h2.astype(BF16)
    a = _silu(jnp.dot(h_hi, wgs_ref[...], preferred_element_type=F32)) * jnp.dot(
        h_hi, wus_ref[...], preferred_element_type=F32)
    x1s_ref[...] = x1 + sl(5) * jnp.dot(a.astype(BF16), wds_ref[...], preferred_element_type=F32)
    h_lo = (h2 - h_hi.astype(F32)).astype(BF16)
    logits = _nt_dot(wrh_ref[...], h_hi) + _nt_dot(wrh_ref[...], h_lo) + _nt_dot(wrl_ref[...], h_hi)
    s = jax.nn.sigmoid(logits)
    gates_ref[...] = _route(s, s + br_ref[...])


def _merge(oac, oal, ob, ga, gb, xp, xs, mod3, g2, wpa, wpb, wo, wrh, wrl, br, wgs, wus, wds):
    row = lambda i: (i, 0)
    const = lambda shape: pl.BlockSpec(shape, lambda i: (0,) * len(shape))
    return pl.pallas_call(
        _merge_kernel,
        grid=(N_TOK // _TM,),
        in_specs=[pl.BlockSpec((_TM, D_ATT), _ctx_map),
                  pl.BlockSpec((_TM, D_ATT), _lat_map),
                  pl.BlockSpec((_TM, D_RNN), row),
                  pl.BlockSpec((_TM, D_MODEL), row),
                  pl.BlockSpec((_TM, D_MODEL), row),
                  pl.BlockSpec((_TM, D_MODEL), _ctx_map),
                  pl.BlockSpec((_TM, D_MODEL), _lat_map),
                  pl.BlockSpec((1, 1, N_MOD * D_MODEL), lambda i: (_mod_row(i, _TM), 0, 0)),
                  const((1, D_MODEL)),
                  const((D_ATT, D_MODEL)), const((D_RNN, D_MODEL)), const((D_MODEL, D_MODEL)),
                  const((N_EXPERTS, D_MODEL)), const((N_EXPERTS, D_MODEL)), const((N_EXPERTS, 1)),
                  const((D_MODEL, D_EXPERT)), const((D_MODEL, D_EXPERT)), const((D_EXPERT, D_MODEL))],
        out_specs=[pl.BlockSpec((_TM, D_MODEL), row),
                   pl.BlockSpec((_TM * _FOLD, LANES), row),
                   pl.BlockSpec((N_EXPERTS, _TM), lambda i: (0, i))],
        out_shape=[jax.ShapeDtypeStruct((N_TOK, D_MODEL), F32),
                   jax.ShapeDtypeStruct((N_TOK * _FOLD, LANES), F32),
                   jax.ShapeDtypeStruct((N_EXPERTS, N_TOK), F32)],
        compiler_params=_params(("parallel",)),
        name="merge",
    )(oac, oal, ob, ga, gb, xp, xs, mod3, g2, wpa, wpb, wo, wrh, wrl, br, wgs, wus, wds)


_FOLD = D_MODEL // LANES
_NH = N_TOK // 2
_R = 512
_SLOTS = 256
_ID_BITS = 13
_DISP_BITS = 12
_VALID_BIT = _ID_BITS + _DISP_BITS + 1


def _chunk_table(cnt):
    shape = (N_EXPERTS, _SLOTS)
    nch = jnp.broadcast_to((cnt + (_R - 1)) // _R, shape)
    erow = lax.broadcasted_iota(jnp.int32, shape, 0)
    q = lax.broadcasted_iota(jnp.int32, shape, 1)
    cum = nch
    s = 1
    while s < N_EXPERTS:
        cum = cum + jnp.where(erow >= s, pltpu.roll(cum, s, 0), 0)
        s *= 2
    total = cum[N_EXPERTS - 1:N_EXPERTS, :]
    used = q[:1] < total
    expert = jnp.minimum(jnp.sum((cum <= q).astype(jnp.int32), axis=0, keepdims=True), N_EXPERTS - 1)
    first = cum - nch
    start = jnp.max(jnp.where(first <= q, first, 0), axis=0, keepdims=True)
    chunk = jnp.where(used, q[:1] - start, 0)
    return jnp.concatenate([expert, chunk, used.astype(jnp.int32), total,
                            jnp.zeros((4, _SLOTS), jnp.int32)], axis=0)


def _dispatch_kernel(g_ref, src_ref, dst_ref, gl_ref, tbl_ref):
    g = g_ref[...]
    n = g.shape[1]
    lane = lax.broadcasted_iota(jnp.int32, g.shape, 1)
    sel = g > 0.0
    m = sel.astype(jnp.int32)
    csum = m
    s = 1
    while s < n:
        csum = csum + jnp.where(lane >= s, pltpu.roll(csum, s, 1), 0)
        s *= 2
    tbl_ref[0] = _chunk_table(csum[:, n - 1:n])
    disp = lane - (csum - m)
    packed = jnp.where(sel, (1 << _VALID_BIT) | (disp << _ID_BITS) | lane, 0)
    g = jnp.where(sel, g, 0.0)
    for b in range(_DISP_BITS):
        shift = n - (1 << b)
        inc = pltpu.roll(packed, shift, 1)
        inc_moves = ((inc >> (_ID_BITS + b)) & 1) == 1
        own_moves = ((packed >> (_ID_BITS + b)) & 1) == 1
        packed = jnp.where(inc_moves, inc, jnp.where(own_moves, 0, packed))
        g = jnp.where(inc_moves, pltpu.roll(g, shift, 1), jnp.where(own_moves, 0.0, g))
    valid = (packed >> _VALID_BIT) == 1
    tok = packed & ((1 << _ID_BITS) - 1)
    src_ref[0] = jnp.where(valid, tok, 0) * _FOLD
    dst_ref[0] = jnp.where(valid, tok, n) * _FOLD
    gl_ref[0] = g


def _dispatch(gates_t):
    halves = N_TOK // _NH
    blk = pl.BlockSpec((1, N_EXPERTS, _NH), lambda h: (h, 0, 0))
    lists = jax.ShapeDtypeStruct((halves, N_EXPERTS, _NH), jnp.int32)
    return pl.pallas_call(
        _dispatch_kernel,
        grid=(halves,),
        in_specs=[pl.BlockSpec((N_EXPERTS, _NH), lambda h: (0, h))],
        out_specs=[blk, blk, blk, pl.BlockSpec((1, 8, _SLOTS), lambda h: (h, 0, 0))],
        out_shape=[lists, lists, jax.ShapeDtypeStruct((halves, N_EXPERTS, _NH), F32),
                   jax.ShapeDtypeStruct((halves, 8, _SLOTS), jnp.int32)],
        compiler_params=_params(("parallel",)),
        name="dispatch",
    )(gates_t)


_GRP = 8
_MXU = 256
_KT = D_MODEL // _MXU
_UNF = 64
_TBL = 8 * _SLOTS
assert _NH * TOP_K // _R + N_EXPERTS + 4 <= _SLOTS


def _tbl(tbl_ref, h, row, k):
    return tbl_ref[h * _TBL + row * _SLOTS + jnp.clip(k, 0, _SLOTS - 1)]


def _moe_kernel(tbl_ref, src_hbm, dst_hbm, gate_hbm, hf_ref, wg_hbm, wu_hbm, wd_hbm, out_hbm,
                acc_ref, wf_refs, wb_refs, xg_ref, xa_ref, xb_ref, ya_ref, yb_ref,
                src_s, dst_s, gate_v, list_sem, w_sem, out_sem):
    h = pl.program_id(0)
    total = _tbl(tbl_ref, h, 3, 0)
    wgb_ref, wub_ref, wdb_ref = wb_refs

    for ref in (acc_ref, ya_ref, yb_ref):
        ref[...] = jnp.zeros_like(ref)

    @pl.when(h == 0)
    def _():
        for ref in (xa_ref, xb_ref, wgb_ref, wub_ref, wdb_ref):
            ref[...] = jnp.zeros_like(ref)

    def list_block(k):
        return (h * N_EXPERTS + _tbl(tbl_ref, h, 0, k)) * (_NH // _R) + _tbl(tbl_ref, h, 1, k)

    def new_expert(k):
        return (k >= 0) & (_tbl(tbl_ref, h, 2, k) == 1) & (_tbl(tbl_ref, h, 1, k) == 0)

    def list_copies(t, p):
        return (pltpu.make_async_copy(src_hbm.at[list_block(t)], src_s.at[p], list_sem.at[p, 0]),
                pltpu.make_async_copy(dst_hbm.at[list_block(t - 2)], dst_s.at[p], list_sem.at[p, 1]),
                pltpu.make_async_copy(gate_hbm.at[list_block(t - 1)], gate_v.at[1 - p], list_sem.at[p, 2]))

    def weight_copies(t, p):
        e = _tbl(tbl_ref, h, 0, t - 1)
        return [pltpu.make_async_copy(w_hbm.at[e], w_f.at[p], w_sem.at[p, n])
                for n, (w_hbm, w_f) in enumerate(zip((wg_hbm, wu_hbm, wd_hbm), wf_refs))]

    def prefetch(t, p):
        for cp in list_copies(t, p):
            cp.start()

        @pl.when(new_expert(t - 1))
        def _():
            for cp in weight_copies(t, p):
                cp.start()

    def arrive_lists(t, p):
        for cp in list_copies(t, p):
            cp.wait()

    def arrive_weights(t, p, cast):
        @pl.when(new_expert(t - 1))
        def _():
            for cp in weight_copies(t, p):
                cp.wait()
            if cast:
                for w_f, w_b in zip(wf_refs, wb_refs):
                    w_b[...] = w_f[p].astype(BF16)

    def row(ref, r):
        return ref.at[pl.ds(pl.multiple_of(r, _FOLD), _FOLD), :]

    def stages(t, p, x_next, x_mm, y_mm, y_scatter):
        mm_used = (t >= 1) & (_tbl(tbl_ref, h, 2, t - 1) == 1)
        src_ref, dst_ref, gate_ref = src_s.at[p, 0], dst_s.at[p, 0], gate_v.at[1 - p]

        def gather_group(j0):
            for j in range(j0, j0 + _GRP):
                row(xg_ref, j * _FOLD)[...] = row(hf_ref, src_ref[j])[...]

        def scatter_group(j0):
            dsts = [row(acc_ref, dst_ref[j0 + u]) for u in range(_GRP)]
            news = [dsts[u][...] + row(y_scatter, (j0 + u) * _FOLD)[...] for u in range(_GRP)]
            for u in range(_GRP):
                dsts[u][...] = news[u]

        def unfold_block(r0):
            for k in range(_FOLD):
                x_next[r0:r0 + _UNF, k * LANES:(k + 1) * LANES] = xg_ref[
                    pl.ds(r0 * _FOLD + k, _UNF, stride=_FOLD), :].astype(BF16)

        moves = []
        for j0 in range(0, _R, _GRP):
            moves += [functools.partial(scatter_group, j0), functools.partial(gather_group, j0)]
            if (j0 + _GRP) % _UNF == 0:
                moves.append(functools.partial(unfold_block, j0 + _GRP - _UNF))
        n_pieces = 3 * _KT
        done = [0]

        def some_moves():
            done[0] += 1
            for f in moves[(done[0] - 1) * len(moves) // n_pieces:done[0] * len(moves) // n_pieces]:
                f()

        diag = (lax.broadcasted_iota(jnp.int32, (_R, _R), 0) == lax.broadcasted_iota(jnp.int32, (_R, _R), 1))
        gates = gate_ref[...] * mm_used.astype(F32)
        gcol = jnp.sum(jnp.where(diag, gates, 0.0), axis=1, keepdims=True)
        gt = up = None
        for kt in range(_KT):
            rows_k = slice(kt * _MXU, (kt + 1) * _MXU)
            xk = x_mm[:, rows_k]
            d = jnp.dot(xk, wgb_ref[rows_k, :], preferred_element_type=F32)
            gt = d if gt is None else gt + d
            some_moves()
            d = jnp.dot(xk, wub_ref[rows_k, :], preferred_element_type=F32)
            up = d if up is None else up + d
            some_moves()
        act = (_silu(gt) * up * gcol).astype(BF16)
        for nt in range(D_MODEL // _MXU):
            y = jnp.dot(act, wdb_ref[:, nt * _MXU:(nt + 1) * _MXU], preferred_element_type=F32)
            some_moves()
            for k in range(_MXU // LANES):
                y_mm[pl.ds(nt * _MXU // LANES + k, _R, stride=_FOLD), :] = y[:, k * LANES:(k + 1) * LANES]

    def step(t, p):
        arrive_lists(t, p)
        prefetch(t + 1, 1 - p)
        arrive_weights(t, p, cast=True)
        if p == 0:
            stages(t, p, xa_ref, xb_ref, yb_ref, ya_ref)
        else:
            stages(t, p, xb_ref, xa_ref, ya_ref, yb_ref)

    def two_steps(i, carry):
        step(2 * i, 0)
        step(2 * i + 1, 1)
        return carry

    pairs = (total + 3) // 2
    prefetch(0, 0)
    lax.fori_loop(0, pairs, two_steps, 0)
    arrive_lists(2 * pairs, 0)
    arrive_weights(2 * pairs, 0, cast=False)

    rows = _NH * _FOLD
    cp = pltpu.make_async_copy(acc_ref.at[pl.ds(0, rows)],
                               out_hbm.at[pl.ds(pl.multiple_of(h * rows, rows), rows)], out_sem)
    cp.start()
    cp.wait()


def _moe(tbl, src, dst, gl, hfold, wg, wu, wd):
    halves = N_TOK // _NH
    hbm = pl.BlockSpec(memory_space=pl.ANY)
    stage = pltpu.VMEM((_R * _FOLD, LANES), F32)
    packed = pltpu.VMEM((_R, D_MODEL), BF16)
    up_shape, down_shape = (D_MODEL, D_EXPERT), (D_EXPERT, D_MODEL)
    return pl.pallas_call(
        _moe_kernel,
        grid_spec=pltpu.PrefetchScalarGridSpec(
            num_scalar_prefetch=1,
            grid=(halves,),
            in_specs=[hbm, hbm, hbm,
                      pl.BlockSpec((_NH * _FOLD, LANES), lambda h, tbl_ref: (h, 0)),
                      hbm, hbm, hbm],
            out_specs=hbm,
            scratch_shapes=[pltpu.VMEM(((_NH + 1) * _FOLD, LANES), F32),
                            tuple(pltpu.VMEM((2,) + s, F32) for s in (up_shape, up_shape, down_shape)),
                            tuple(pltpu.VMEM(s, BF16) for s in (up_shape, up_shape, down_shape)),
                            stage, packed, packed, stage, stage,
                            pltpu.SMEM((2, 1, _R), jnp.int32), pltpu.SMEM((2, 1, _R), jnp.int32),
                            pltpu.VMEM((2, 1, _R), F32),
                            pltpu.SemaphoreType.DMA((2, 3)), pltpu.SemaphoreType.DMA((2, 3)),
                            pltpu.SemaphoreType.DMA]),
        out_shape=jax.ShapeDtypeStruct((N_TOK * _FOLD, LANES), F32),
        compiler_params=_params(("arbitrary",)),
        name="moe",
    )(tbl, src, dst, gl, hfold, wg, wu, wd)


def _final_kernel(x_ref, r_ref, mod_ref, fg_ref, yp_ref, ys_ref):
    i = pl.program_id(0)
    routed = jnp.concatenate([r_ref[pl.ds(k, _TM, stride=_FOLD), :] for k in range(_FOLD)], axis=1)
    m = mod_ref[0]
    x2 = x_ref[...] + m[:, 5 * D_MODEL:6 * D_MODEL] * routed
    y = x2 * lax.rsqrt(jnp.mean(x2 * x2, axis=-1, keepdims=True) + EPS) * fg_ref[...]

    @pl.when(i < _CTX_TILES)
    def _():
        yp_ref[...] = y

    @pl.when(i >= _CTX_TILES)
    def _():
        ys_ref[...] = y


def _final(x1s, routed, mod3, fg):
    return pl.pallas_call(
        _final_kernel,
        grid=(N_TOK // _TM,),
        in_specs=[pl.BlockSpec((_TM, D_MODEL), lambda i: (i, 0)),
                  pl.BlockSpec((_TM * _FOLD, LANES), lambda i: (i, 0)),
                  pl.BlockSpec((1, 1, N_MOD * D_MODEL), lambda i: (_mod_row(i, _TM), 0, 0)),
                  pl.BlockSpec((1, D_MODEL), lambda i: (0, 0))],
        out_specs=[pl.BlockSpec((_TM, D_MODEL), _ctx_map),
                   pl.BlockSpec((_TM, D_MODEL), _lat_map)],
        out_shape=[jax.ShapeDtypeStruct((N_CTX, D_MODEL), F32),
                   jax.ShapeDtypeStruct((N_LAT, D_MODEL), F32)],
        compiler_params=_params(("arbitrary",)),
        name="final",
    )(x1s, routed, mod3, fg)


def kernel(x_prompt, x_sample, cache_k, cache_v, state_h_fwd, state_h_bwd, c, c_ctx, norm1_g, norm2_g, final_g, w_ada, b_ada, w_in, rpb, conv_w, conv_b, rg_wa, rg_ba, rg_wx, rg_bx, rg_lam, w_pa, w_pb, w_out, w_router, b_router, w_gate_e, w_up_e, w_down_e, w_gate_s, w_up_s, w_down_s):
    l = 0
    xp = x_prompt.reshape(N_CTX, D_MODEL)
    xs = x_sample.reshape(N_LAT, D_MODEL)

    cvecs = jnp.concatenate([c_ctx[None, :], c], axis=0)[:, :, None]
    mod = _adaln(cvecs, w_ada[l], b_ada[l][None, :])
    mod3 = mod.reshape(8, 1, N_MOD * D_MODEL)

    q, kc, vc, kl, vl, xr, yg, ga, gb = _inproj(xp, xs, mod3, norm1_g[l][None, :], w_in[l].astype(BF16))

    oac = _attn_ctx(q, kc, vc)
    ck = cache_k[:, l].reshape(DEC_BATCH, PAST_LEN, D_ATT)
    cv = cache_v[:, l].reshape(DEC_BATCH, PAST_LEN, D_ATT)
    oal = _attn_lat(q, kl, vl, ck, cv, _bias_tables(rpb[l]))

    w4 = jnp.concatenate([_block_diag(rg_wa[l, 0]), _block_diag(rg_wx[l, 0]),
                          _block_diag(rg_wa[l, 1]), _block_diag(rg_wx[l, 1])], axis=1).astype(BF16)
    b4 = jnp.concatenate([rg_ba[l, 0], rg_bx[l, 0], rg_ba[l, 1], rg_bx[l, 1]])[None, :]
    ob, hlf, hlb = _rglru(xr, yg, conv_w[l], conv_b[l][None, :], w4, b4, rg_lam[l],
                          state_h_fwd[:, l], state_h_bwd[:, l])

    wr_t = w_router[l].T
    wr_hi = wr_t.astype(BF16)
    wr_lo = (wr_t - wr_hi.astype(F32)).astype(BF16)
    x1s, hfold, gates_t = _merge(oac, oal, ob, ga, gb, xp, xs, mod3, norm2_g[l][None, :],
                                 w_pa[l].astype(BF16), w_pb[l].astype(BF16), w_out[l].astype(BF16),
                                 wr_hi, wr_lo, b_router[l][:, None],
                                 w_gate_s[l].astype(BF16), w_up_s[l].astype(BF16), w_down_s[l].astype(BF16))

    src, dst, gl, tbl = _dispatch(gates_t)
    routed = _moe(tbl.reshape(-1), src.reshape(-1, 1, _R), dst.reshape(-1, 1, _R), gl.reshape(-1, 1, _R), hfold,
                  w_gate_e[l], w_up_e[l], w_down_e[l])
    yp, ys = _final(x1s, routed, mod3, final_g[None, :])

    return (yp.reshape(BATCH, SEQ, D_MODEL),
            ys.reshape(DEC_BATCH, DEC_SEQ, D_MODEL),
            kc.reshape(BATCH, 1, SEQ, N_HEADS_A, HEAD_DIM_A),
            vc.reshape(BATCH, 1, SEQ, N_HEADS_A, HEAD_DIM_A),
            hlf.reshape(BATCH, 1, D_RNN),
            hlb.reshape(BATCH, 1, D_RNN))
```

```python
import functools

import jax
import jax.numpy as jnp
from jax import lax
from jax.experimental import pallas as pl
from jax.experimental.pallas import tpu as pltpu

F32 = jnp.float32
BF16 = jnp.bfloat16

D_MODEL = 1024
BATCH = 16
SEQ = 256
DEC_BATCH = 2
DEC_SEQ = 1024
PAST_LEN = 512
GRID_W = 64
N_HEADS_A = 8
HEAD_DIM_A = 64
D_ATT = N_HEADS_A * HEAD_DIM_A
KH = 8
KW = 16
D_RNN = 512
N_RG_BLOCKS = 8
CONV_W = 4
RG_C = 8.0
N_EXPERTS = 64
TOP_K = 8
N_GROUPS = 8
GROUP_SIZE = N_EXPERTS // N_GROUPS
TOPK_GROUP = 4
D_EXPERT = 256
ROUTED_SCALE = 2.5
N_MOD = 6
EPS = 1e-6
NEG = -1e30
D_IN = 3 * D_ATT + 2 * D_RNN + 2 * D_MODEL

N_CTX = BATCH * SEQ
N_LAT = DEC_BATCH * DEC_SEQ
N_TOK = N_CTX + N_LAT
GRID_ROWS = DEC_SEQ // GRID_W

LANES = 128
VMEM_LIMIT = 56 * 1024 * 1024


def _params(sem, vmem=VMEM_LIMIT):
    return pltpu.CompilerParams(dimension_semantics=sem, vmem_limit_bytes=vmem)


def _mod_row(i, tile_rows):
    n_ctx_tiles = N_CTX // tile_rows
    return jnp.where(i < n_ctx_tiles, 0, 1 + (i - n_ctx_tiles) // (DEC_SEQ // tile_rows))


def _norm_mod(x, g, shift, scale):
    y = x * lax.rsqrt(jnp.mean(x * x, axis=-1, keepdims=True) + EPS)
    return (y * g) * (1.0 + scale) + shift


def _sigmoid(x):
    return 0.5 * jnp.tanh(0.5 * x) + 0.5


def _silu(x):
    return x * _sigmoid(x)


def _adaln_kernel(c_ref, w_ref, b_ref, o_ref):
    w = w_ref[...]
    rows = []
    for r in range(3):
        s = _silu(c_ref[r])
        rows.append(jnp.sum(w * s, axis=0, keepdims=True))
    rows.append(jnp.zeros((5, w.shape[1]), F32))
    o_ref[...] = jnp.concatenate(rows, axis=0) + b_ref[...]


def _adaln(cvecs, w_ada, b_ada):
    tn = 2048
    n = N_MOD * D_MODEL
    return pl.pallas_call(
        _adaln_kernel,
        grid=(n // tn,),
        in_specs=[pl.BlockSpec((3, D_MODEL, 1), lambda j: (0, 0, 0)),
                  pl.BlockSpec((D_MODEL, tn), lambda j: (0, j)),
                  pl.BlockSpec((1, tn), lambda j: (0, j))],
        out_specs=pl.BlockSpec((8, tn), lambda j: (0, j)),
        out_shape=jax.ShapeDtypeStruct((8, n), F32),
        compiler_params=_params(("parallel",)),
        name="adaln",
    )(cvecs, w_ada, b_ada)


_TM = 512
_CTX_TILES = N_CTX // _TM
_LAT_TILES = N_LAT // _TM


def _ctx_map(i):
    return (jnp.minimum(i, _CTX_TILES - 1), 0)


def _lat_map(i):
    return (jnp.maximum(i - _CTX_TILES, 0), 0)


def _inproj_kernel(xp_ref, xs_ref, mod_ref, g_ref, w_ref,
                   q_ref, kc_ref, vc_ref, kl_ref, vl_ref, xr_ref, yg_ref, ga_ref, gb_ref):
    i = pl.program_id(0)
    is_ctx = i < _CTX_TILES
    x = jnp.where(is_ctx, xp_ref[...], xs_ref[...])
    m = mod_ref[0]
    h = _norm_mod(x, g_ref[...], m[:, 0:D_MODEL], m[:, D_MODEL:2 * D_MODEL]).astype(BF16)

    def proj(a, b):
        return jnp.dot(h, w_ref[:, a:b], preferred_element_type=F32)

    q_ref[...] = proj(0, D_ATT).astype(BF16)
    k = proj(D_ATT, 2 * D_ATT)
    v = proj(2 * D_ATT, 3 * D_ATT)

    @pl.when(is_ctx)
    def _():
        kc_ref[...] = k
        vc_ref[...] = v

    @pl.when(jnp.logical_not(is_ctx))
    def _():
        kl_ref[...] = k
        vl_ref[...] = v

    o = 3 * D_ATT
    xr_ref[...] = proj(o, o + D_RNN)
    yg_ref[...] = proj(o + D_RNN, o + 2 * D_RNN).astype(BF16)
    o += 2 * D_RNN
    ga_ref[...] = proj(o, o + D_MODEL).astype(BF16)
    gb_ref[...] = proj(o + D_MODEL, o + 2 * D_MODEL).astype(BF16)


def _inproj(xp, xs, mod3, g1, w_in):
    row = lambda i: (i, 0)
    return pl.pallas_call(
        _inproj_kernel,
        grid=(N_TOK // _TM,),
        in_specs=[pl.BlockSpec((_TM, D_MODEL), _ctx_map),
                  pl.BlockSpec((_TM, D_MODEL), _lat_map),
                  pl.BlockSpec((1, 1, N_MOD * D_MODEL), lambda i: (_mod_row(i, _TM), 0, 0)),
                  pl.BlockSpec((1, D_MODEL), lambda i: (0, 0)),
                  pl.BlockSpec((D_MODEL, D_IN), lambda i: (0, 0))],
        out_specs=[pl.BlockSpec((_TM, D_ATT), row),
                   pl.BlockSpec((_TM, D_ATT), _ctx_map),
                   pl.BlockSpec((_TM, D_ATT), _ctx_map),
                   pl.BlockSpec((_TM, D_ATT), _lat_map),
                   pl.BlockSpec((_TM, D_ATT), _lat_map),
                   pl.BlockSpec((_TM, D_RNN), row),
                   pl.BlockSpec((_TM, D_RNN), row),
                   pl.BlockSpec((_TM, D_MODEL), row),
                   pl.BlockSpec((_TM, D_MODEL), row)],
        out_shape=[jax.ShapeDtypeStruct((N_TOK, D_ATT), BF16),
                   jax.ShapeDtypeStruct((N_CTX, D_ATT), F32),
                   jax.ShapeDtypeStruct((N_CTX, D_ATT), F32),
                   jax.ShapeDtypeStruct((N_LAT, D_ATT), F32),
                   jax.ShapeDtypeStruct((N_LAT, D_ATT), F32),
                   jax.ShapeDtypeStruct((N_TOK, D_RNN), F32),
                   jax.ShapeDtypeStruct((N_TOK, D_RNN), BF16),
                   jax.ShapeDtypeStruct((N_TOK, D_MODEL), BF16),
                   jax.ShapeDtypeStruct((N_TOK, D_MODEL), BF16)],
        compiler_params=_params(("arbitrary",)),
        name="inproj",
    )(xp, xs, mod3, g1, w_in)


def _pair_attention(qp, segs):
    scale = HEAD_DIM_A ** -0.5
    lane = lax.broadcasted_iota(jnp.int32, (1, LANES), 1)
    outs = []
    for a in range(2):
        sel = (lane >= HEAD_DIM_A) if a else (lane < HEAD_DIM_A)
        qm = jnp.where(sel, qp, jnp.zeros_like(qp))
        ss = []
        for kp, _, bias_fn in segs:
            s = lax.dot_general(qm, kp, (((1,), (1,)), ((), ())), preferred_element_type=F32) * scale
            if bias_fn is not None:
                s = s + bias_fn(a)
            ss.append(s)
        m = functools.reduce(jnp.maximum, [jnp.max(s, axis=-1, keepdims=True) for s in ss])
        es = [jnp.exp(s - m) for s in ss]
        inv = 1.0 / functools.reduce(jnp.add, [jnp.sum(e, axis=-1, keepdims=True) for e in es])
        o = functools.reduce(jnp.add, [
            jnp.dot((e * inv).astype(BF16), vp, preferred_element_type=F32)
            for e, (_, vp, _) in zip(es, segs)])
        outs.append(o)
    return jnp.where(lane < HEAD_DIM_A, outs[0], outs[1])


def _attn_ctx_kernel(q_ref, k_ref, v_ref, o_ref):
    for j in range(D_ATT // LANES):
        c = slice(j * LANES, (j + 1) * LANES)
        segs = [(k_ref[:, c].astype(BF16), v_ref[:, c].astype(BF16), None)]
        o_ref[:, c] = _pair_attention(q_ref[:, c], segs).astype(BF16)


def _attn_ctx(q_all, kc, vc):
    blk = pl.BlockSpec((SEQ, D_ATT), lambda b: (b, 0))
    return pl.pallas_call(
        _attn_ctx_kernel,
        grid=(BATCH,),
        in_specs=[blk, blk, blk],
        out_specs=blk,
        out_shape=jax.ShapeDtypeStruct((N_CTX, D_ATT), BF16),
        compiler_params=_params(("parallel",)),
        name="attn_ctx",
    )(q_all, kc, vc)


_QROWS = 4
_QB = _QROWS * GRID_W
_KROWS = 12
_KB = _KROWS * GRID_W


def _build_bias_tables(rows_ref, t2_ref):
    wq = lax.broadcasted_iota(jnp.int32, (GRID_W, LANES), 0)
    wk = lax.broadcasted_iota(jnp.int32, (GRID_W, LANES), 1) & (GRID_W - 1)
    col_start = jnp.clip(wq - KW // 2, 0, GRID_W - KW)
    col_in = (wk >= col_start) & (wk < col_start + KW)
    for r in range(N_HEADS_A * 2 * KH):
        x = jnp.broadcast_to(rows_ref[r:r + 1, :], (GRID_W, LANES))
        x = pltpu.roll(x, LANES - (KW - 1), 1, stride=1, stride_axis=0)
        t2_ref[r // (2 * KH), r % (2 * KH)] = jnp.where(col_in, x, NEG)


def _attn_lat_kernel(q_ref, k_ref, v_ref, ck_ref, cv_ref, rows_ref, o_ref, t2_ref):
    i = pl.program_id(1)

    @pl.when((pl.program_id(0) == 0) & (i == 0))
    def _():
        _build_bias_tables(rows_ref, t2_ref)

    ks = jnp.where(i < 2, 0, GRID_ROWS - _KROWS)
    kstart = pl.multiple_of(ks * GRID_W, 256)
    lane = lax.broadcasted_iota(jnp.int32, (1, LANES), 1)

    def bias_for_head(h):
        rows = []
        for rq in range(_QROWS):
            r = _QROWS * i + rq
            rs = jnp.clip(r - KH // 2, 0, GRID_ROWS - KH)
            tiles = []
            for t in range(_KROWS // 2):
                kr = ks + 2 * t
                d = jnp.clip(kr - r + (KH - 1), -1, 2 * KH - 2)
                tile = t2_ref[h, d + 1]
                v0 = ((kr >= rs) & (kr < rs + KH)).astype(jnp.int32)
                v1 = ((kr + 1 >= rs) & (kr + 1 < rs + KH)).astype(jnp.int32)
                valid = jnp.where(lane < GRID_W, v0, v1) > 0
                tiles.append(jnp.where(valid, tile, NEG))
            rows.append(jnp.concatenate(tiles, axis=1))
        return jnp.concatenate(rows, axis=0)

    for j in range(D_ATT // LANES):
        c = slice(j * LANES, (j + 1) * LANES)
        k_loc = k_ref[pl.ds(kstart, _KB), c].astype(BF16)
        v_loc = v_ref[pl.ds(kstart, _KB), c].astype(BF16)
        segs = [(k_loc, v_loc, lambda a, j=j: bias_for_head(2 * j + a)),
                (ck_ref[:, c].astype(BF16), cv_ref[:, c].astype(BF16), None)]
        o_ref[:, c] = _pair_attention(q_ref[:, c], segs).astype(BF16)


def _attn_lat(q_all, kl, vl, ck, cv, t2):
    qblocks = DEC_SEQ // _QB
    return pl.pallas_call(
        _attn_lat_kernel,
        grid=(DEC_BATCH, qblocks),
        in_specs=[pl.BlockSpec((_QB, D_ATT), lambda b, i: (N_CTX // _QB + b * qblocks + i, 0)),
                  pl.BlockSpec((DEC_SEQ, D_ATT), lambda b, i: (b, 0)),
                  pl.BlockSpec((DEC_SEQ, D_ATT), lambda b, i: (b, 0)),
                  pl.BlockSpec((None, PAST_LEN, D_ATT), lambda b, i: (b, 0, 0)),
                  pl.BlockSpec((None, PAST_LEN, D_ATT), lambda b, i: (b, 0, 0)),
                  pl.BlockSpec((N_HEADS_A * 2 * KH, LANES), lambda b, i: (0, 0))],
        out_specs=pl.BlockSpec((_QB, D_ATT), lambda b, i: (b * qblocks + i, 0)),
        out_shape=jax.ShapeDtypeStruct((N_LAT, D_ATT), BF16),
        scratch_shapes=[pltpu.VMEM((N_HEADS_A, 2 * KH, GRID_W, LANES), F32)],
        compiler_params=_params(("arbitrary", "arbitrary")),
        name="attn_lat",
    )(q_all, kl, vl, ck, cv, t2)


def _bias_tables(rpb):
    half = jnp.pad(rpb, ((0, 0), (0, 0), (0, GRID_W - rpb.shape[-1])))
    neg = jnp.full((N_HEADS_A, 1, GRID_W), NEG, F32)
    left = jnp.concatenate([neg, half], axis=1)
    right = jnp.concatenate([half, neg], axis=1)
    return jnp.concatenate([left, right], axis=-1).reshape(N_HEADS_A * 2 * KH, LANES)


_RG_ROWS = 2048
_RG_CHUNK = 256
_RG_SLABS = D_RNN // LANES
_LOG2E = 1.4426950408889634
_RG_CTX_STEPS = N_CTX // _RG_ROWS


def _rg_block(seq_len, h0f, h0b, xr_ref, yg_ref, cw_ref, cb_ref, w4_ref, b4_ref, lam_ref,
              ob_ref, xc_ref, af_ref, uf_ref, ab_ref, ub_ref, hf_ref, hb_ref, cf_ref, cr_ref):
    T, R = _RG_CHUNK, _RG_ROWS
    nb = R // T
    segs = seq_len // T
    linked = segs > 1
    xr = xr_ref[...]
    t_idx = lax.broadcasted_iota(jnp.int32, (R, 1), 0) & (seq_len - 1)
    cw = cw_ref[...]
    xc = jnp.zeros((R, D_RNN), F32) + cb_ref[...]
    for j in range(CONV_W):
        off = j - 2
        if off == 0:
            tap = xr
        else:
            tap = pltpu.roll(xr, shift=(-off) % R, axis=0)
            tap = jnp.where((t_idx + off >= 0) & (t_idx + off < seq_len), tap, 0.0)
        xc = xc + cw[j:j + 1, :] * tap
    xc_ref[...] = xc

    lam = lam_ref[...]
    z = -lam
    sp = jnp.maximum(z, 0.0) + jnp.log1p(jnp.exp(-jnp.abs(z)))
    decay_k = (-0.5 * RG_C * _LOG2E) * sp

    def put(ref, c, val):
        for k in range(_RG_SLABS):
            ref[k, pl.ds(c, T, stride=nb), :] = val[:, k * LANES:(k + 1) * LANES]

    def get(ref, c):
        return jnp.concatenate([ref[k, pl.ds(c, T, stride=nb), :] for k in range(_RG_SLABS)], axis=1)

    def gate_chunk(c, carry):
        x = xc_ref[pl.ds(pl.multiple_of(c * T, T), T), :]
        g4 = jnp.dot(x.astype(BF16), w4_ref[...], preferred_element_type=F32) + b4_ref[...]
        hx = 0.5 * x
        for d, (a_ref, u_ref) in enumerate(((af_ref, uf_ref), (ab_ref, ub_ref))):
            tr = jnp.tanh(0.5 * g4[:, (2 * d) * D_RNN:(2 * d + 1) * D_RNN])
            ti = jnp.tanh(0.5 * g4[:, (2 * d + 1) * D_RNN:(2 * d + 2) * D_RNN])
            k = decay_k[d:d + 1, :]
            a = jnp.exp2(k * tr + k)
            put(a_ref, c, a)
            put(u_ref, c, jnp.sqrt(jnp.maximum(1.0 - a * a, 0.0)) * (hx * ti + hx))
        return carry

    lax.fori_loop(0, nb, gate_chunk, 0)

    def step(t, carry):
        hf, hb, pf, pb = carry
        rf = pl.ds(pl.multiple_of(t * nb, nb), nb)
        rb = pl.ds(pl.multiple_of((T - 1 - t) * nb, nb), nb)
        nf, nbk, npf, npb = [], [], [], []
        for k in range(_RG_SLABS):
            a_f, a_b = af_ref[k, rf, :], ab_ref[k, rb, :]
            f = a_f * hf[k] + uf_ref[k, rf, :]
            b = a_b * hb[k] + ub_ref[k, rb, :]
            hf_ref[k, rf, :] = f
            hb_ref[k, rb, :] = b
            nf.append(f)
            nbk.append(b)
            if linked:
                npf.append(a_f * pf[k])
                npb.append(a_b * pb[k])
                af_ref[k, rf, :] = npf[-1]
                ab_ref[k, rb, :] = npb[-1]
        return tuple(nf), tuple(nbk), tuple(npf), tuple(npb)

    zeros = (jnp.zeros((nb, LANES), F32),) * _RG_SLABS
    ones = (jnp.ones((nb, LANES), F32),) * _RG_SLABS if linked else ()
    hf, hb, pf, pb = lax.fori_loop(0, T, step, (zeros, zeros, ones, ones))

    if linked:
        seg_row = lax.broadcasted_iota(jnp.int32, (nb, LANES), 0)
        for k in range(_RG_SLABS):
            lanes = slice(k * LANES, (k + 1) * LANES)
            enter_f = jnp.zeros((nb, LANES), F32)
            enter_b = jnp.zeros((nb, LANES), F32)
            for s0 in range(0, nb, segs):
                c = h0f[s0 // segs:s0 // segs + 1, lanes]
                for q in range(s0, s0 + segs):
                    enter_f = jnp.where(seg_row == q, c, enter_f)
                    c = hf[k][q:q + 1, :] + pf[k][q:q + 1, :] * c
                c = h0b[s0 // segs:s0 // segs + 1, lanes]
                for q in reversed(range(s0, s0 + segs)):
                    enter_b = jnp.where(seg_row == q, c, enter_b)
                    c = hb[k][q:q + 1, :] + pb[k][q:q + 1, :] * c
            cf_ref[:, lanes] = enter_f
            cr_ref[:, lanes] = enter_b

    def out_chunk(c, carry):
        rows = pl.ds(pl.multiple_of(c * T, T), T)
        hsum = get(hf_ref, c) + get(hb_ref, c)
        if linked:
            hsum = hsum + get(af_ref, c) * cf_ref[pl.ds(c, 1), :] + get(ab_ref, c) * cr_ref[pl.ds(c, 1), :]
        ob_ref[rows, :] = (hsum * jax.nn.gelu(yg_ref[rows, :].astype(F32))).astype(BF16)
        return carry

    lax.fori_loop(0, nb, out_chunk, 0)
    return hf, hb


def _rglru_kernel(xr_ref, yg_ref, cw_ref, cb_ref, w4_ref, b4_ref, lam_ref, sf_ref, sb_ref,
                  ob_ref, hlf_ref, hlb_ref, *scratch):
    i = pl.program_id(0)
    shared = (xr_ref, yg_ref, cw_ref, cb_ref, w4_ref, b4_ref, lam_ref, ob_ref) + scratch

    @pl.when(i < _RG_CTX_STEPS)
    def _():
        hf, hb = _rg_block(SEQ, None, None, *shared)
        hlf_ref[...] = jnp.concatenate(hf, axis=1)
        hlb_ref[...] = jnp.concatenate(hb, axis=1)

    @pl.when(i >= _RG_CTX_STEPS)
    def _():
        _rg_block(DEC_SEQ, sf_ref[...], sb_ref[...], *shared)


def _rglru(xr, yg, conv_w, conv_b, w4, b4, lam, sf, sb):
    steps = N_TOK // _RG_ROWS
    nb_ctx = _RG_ROWS // SEQ
    const = lambda shape: pl.BlockSpec(shape, lambda i: (0,) * len(shape))
    hl_map = lambda i: (jnp.minimum(i, _RG_CTX_STEPS - 1), 0)
    scan_buf = pltpu.VMEM((_RG_SLABS, _RG_ROWS, LANES), F32)
    enter_buf = pltpu.VMEM((_RG_ROWS // _RG_CHUNK, D_RNN), F32)
    return pl.pallas_call(
        _rglru_kernel,
        grid=(steps,),
        in_specs=[pl.BlockSpec((_RG_ROWS, D_RNN), lambda i: (i, 0)),
                  pl.BlockSpec((_RG_ROWS, D_RNN), lambda i: (i, 0)),
                  const((CONV_W, D_RNN)), const((1, D_RNN)),
                  const((D_RNN, 4 * D_RNN)), const((1, 4 * D_RNN)), const((2, D_RNN)),
                  const((DEC_BATCH, D_RNN)), const((DEC_BATCH, D_RNN))],
        out_specs=[pl.BlockSpec((_RG_ROWS, D_RNN), lambda i: (i, 0)),
                   pl.BlockSpec((nb_ctx, D_RNN), hl_map),
                   pl.BlockSpec((nb_ctx, D_RNN), hl_map)],
        out_shape=[jax.ShapeDtypeStruct((N_TOK, D_RNN), BF16),
                   jax.ShapeDtypeStruct((BATCH, D_RNN), F32),
                   jax.ShapeDtypeStruct((BATCH, D_RNN), F32)],
        scratch_shapes=[pltpu.VMEM((_RG_ROWS, D_RNN), F32)] + [scan_buf] * 6 + [enter_buf] * 2,
        compiler_params=_params(("arbitrary",)),
        name="rglru",
    )(xr, yg, conv_w, conv_b, w4, b4, lam, sf, sb)


def _block_diag(w):
    n, d, _ = w.shape
    eye = jnp.eye(n, dtype=w.dtype)
    return jnp.einsum('nij,nm->nimj', w, eye).reshape(n * d, n * d)


def _nt_dot(a, b):
    return lax.dot_general(a, b, (((1,), (1,)), ((), ())), preferred_element_type=F32)


def _route(s, sb):
    n = s.shape[1]
    low = -3.0e38
    sb3 = sb.reshape(N_GROUPS, GROUP_SIZE, n)
    iw = lax.broadcasted_iota(jnp.int32, sb3.shape, 1)
    m1 = jnp.max(sb3, axis=1, keepdims=True)
    i1 = jnp.min(jnp.where(sb3 == m1, iw, GROUP_SIZE), axis=1, keepdims=True)
    m2 = jnp.max(jnp.where(iw == i1, low, sb3), axis=1, keepdims=True)
    gscore = m1 + m2
    ig = lax.broadcasted_iota(jnp.int32, gscore.shape, 0)
    beaten = jnp.zeros(gscore.shape, jnp.int32)
    for g in range(N_GROUPS):
        row = gscore[g:g + 1]
        beaten = beaten + ((row > gscore) | ((row == gscore) & (g < ig))).astype(jnp.int32)
    gsel = beaten < TOPK_GROUP
    work = jnp.where(gsel, sb3, NEG).reshape(N_EXPERTS, n)
    ie = lax.broadcasted_iota(jnp.int32, work.shape, 0)
    chosen = jnp.zeros(work.shape, jnp.bool_)
    for _ in range(TOP_K):
        m = jnp.max(work, axis=0, keepdims=True)
        pick = ie == jnp.min(jnp.where(work == m, ie, N_EXPERTS), axis=0, keepdims=True)
        chosen = chosen | pick
        work = jnp.where(pick, low, work)
    wsel = jnp.where(chosen, s, 0.0)
    return wsel / jnp.sum(wsel, axis=0, keepdims=True) * ROUTED_SCALE


def _merge_kernel(oac_ref, oal_ref, ob_ref, ga_ref, gb_ref, xp_ref, xs_ref, mod_ref, g_ref,
                  wpa_ref, wpb_ref, wo_ref, wrh_ref, wrl_ref, br_ref, wgs_ref, wus_ref, wds_ref,
                  x1s_ref, hf_ref, gates_ref):
    i = pl.program_id(0)
    is_ctx = i < _CTX_TILES
    oa = jnp.where(is_ctx, oac_ref[...], oal_ref[...])
    x = jnp.where(is_ctx, xp_ref[...], xs_ref[...])
    m = mod_ref[0]
    sl = lambda k: m[:, k * D_MODEL:(k + 1) * D_MODEL]
    mix = (_sigmoid(ga_ref[...].astype(F32)) * jnp.dot(oa, wpa_ref[...], preferred_element_type=F32)
           + _sigmoid(gb_ref[...].astype(F32)) * jnp.dot(ob_ref[...], wpb_ref[...], preferred_element_type=F32))
    o = jnp.dot(mix.astype(BF16), wo_ref[...], preferred_element_type=F32)
    x1 = x + sl(2) * o
    h2 = _norm_mod(x1, g_ref[...], sl(3), sl(4))
    for c in range(_FOLD):
        hf_ref[pl.ds(c, _TM, stride=_FOLD), :] = h2[:, c * LANES:(c + 1) * LANES]
    h_hi = h2.astype(BF16)
    a = _silu(jnp.dot(h_hi, wgs_ref[...], preferred_element_type=F32)) * jnp.dot(
        h_hi, wus_ref[...], preferred_element_type=F32)
    x1s_ref[...] = x1 + sl(5) * jnp.dot(a.astype(BF16), wds_ref[...], preferred_element_type=F32)
    h_lo = (h2 - h_hi.astype(F32)).astype(BF16)
    logits = _nt_dot(wrh_ref[...], h_hi) + _nt_dot(wrh_ref[...], h_lo) + _nt_dot(wrl_ref[...], h_hi)
    s = jax.nn.sigmoid(logits)
    gates_ref[...] = _route(s, s + br_ref[...])


def _merge(oac, oal, ob, ga, gb, xp, xs, mod3, g2, wpa, wpb, wo, wrh, wrl, br, wgs, wus, wds):
    row = lambda i: (i, 0)
    const = lambda shape: pl.BlockSpec(shape, lambda i: (0,) * len(shape))
    return pl.pallas_call(
        _merge_kernel,
        grid=(N_TOK // _TM,),
        in_specs=[pl.BlockSpec((_TM, D_ATT), _ctx_map),
                  pl.BlockSpec((_TM, D_ATT), _lat_map),
                  pl.BlockSpec((_TM, D_RNN), row),
                  pl.BlockSpec((_TM, D_MODEL), row),
                  pl.BlockSpec((_TM, D_MODEL), row),
                  pl.BlockSpec((_TM, D_MODEL), _ctx_map),
                  pl.BlockSpec((_TM, D_MODEL), _lat_map),
                  pl.BlockSpec((1, 1, N_MOD * D_MODEL), lambda i: (_mod_row(i, _TM), 0, 0)),
                  const((1, D_MODEL)),
                  const((D_ATT, D_MODEL)), const((D_RNN, D_MODEL)), const((D_MODEL, D_MODEL)),
                  const((N_EXPERTS, D_MODEL)), const((N_EXPERTS, D_MODEL)), const((N_EXPERTS, 1)),
                  const((D_MODEL, D_EXPERT)), const((D_MODEL, D_EXPERT)), const((D_EXPERT, D_MODEL))],
        out_specs=[pl.BlockSpec((_TM, D_MODEL), row),
                   pl.BlockSpec((_TM * _FOLD, LANES), row),
                   pl.BlockSpec((N_EXPERTS, _TM), lambda i: (0, i))],
        out_shape=[jax.ShapeDtypeStruct((N_TOK, D_MODEL), F32),
                   jax.ShapeDtypeStruct((N_TOK * _FOLD, LANES), F32),
                   jax.ShapeDtypeStruct((N_EXPERTS, N_TOK), F32)],
        compiler_params=_params(("parallel",)),
        name="merge",
    )(oac, oal, ob, ga, gb, xp, xs, mod3, g2, wpa, wpb, wo, wrh, wrl, br, wgs, wus, wds)


_FOLD = D_MODEL // LANES
_NH = N_TOK // 2
_R = 512
_SLOTS = 256
_ID_BITS = 13
_DISP_BITS = 12
_VALID_BIT = _ID_BITS + _DISP_BITS + 1


def _chunk_table(cnt):
    shape = (N_EXPERTS, _SLOTS)
    nch = jnp.broadcast_to((cnt + (_R - 1)) // _R, shape)
    erow = lax.broadcasted_iota(jnp.int32, shape, 0)
    q = lax.broadcasted_iota(jnp.int32, shape, 1)
    cum = nch
    s = 1
    while s < N_EXPERTS:
        cum = cum + jnp.where(erow >= s, pltpu.roll(cum, s, 0), 0)
        s *= 2
    total = cum[N_EXPERTS - 1:N_EXPERTS, :]
    used = q[:1] < total
    expert = jnp.minimum(jnp.sum((cum <= q).astype(jnp.int32), axis=0, keepdims=True), N_EXPERTS - 1)
    first = cum - nch
    start = jnp.max(jnp.where(first <= q, first, 0), axis=0, keepdims=True)
    chunk = jnp.where(used, q[:1] - start, 0)
    return jnp.concatenate([expert, chunk, used.astype(jnp.int32), total,
                            jnp.zeros((4, _SLOTS), jnp.int32)], axis=0)


def _dispatch_kernel(g_ref, src_ref, dst_ref, gl_ref, tbl_ref):
    g = g_ref[...]
    n = g.shape[1]
    lane = lax.broadcasted_iota(jnp.int32, g.shape, 1)
    sel = g > 0.0
    m = sel.astype(jnp.int32)
    csum = m
    s = 1
    while s < n:
        csum = csum + jnp.where(lane >= s, pltpu.roll(csum, s, 1), 0)
        s *= 2
    tbl_ref[0] = _chunk_table(csum[:, n - 1:n])
    disp = lane - (csum - m)
    packed = jnp.where(sel, (1 << _VALID_BIT) | (disp << _ID_BITS) | lane, 0)
    g = jnp.where(sel, g, 0.0)
    for b in range(_DISP_BITS):
        shift = n - (1 << b)
        inc = pltpu.roll(packed, shift, 1)
        inc_moves = ((inc >> (_ID_BITS + b)) & 1) == 1
        own_moves = ((packed >> (_ID_BITS + b)) & 1) == 1
        packed = jnp.where(inc_moves, inc, jnp.where(own_moves, 0, packed))
        g = jnp.where(inc_moves, pltpu.roll(g, shift, 1), jnp.where(own_moves, 0.0, g))
    valid = (packed >> _VALID_BIT) == 1
    tok = packed & ((1 << _ID_BITS) - 1)
    src_ref[0] = jnp.where(valid, tok, 0) * _FOLD
    dst_ref[0] = jnp.where(valid, tok, n) * _FOLD
    gl_ref[0] = g


def _dispatch(gates_t):
    halves = N_TOK // _NH
    blk = pl.BlockSpec((1, N_EXPERTS, _NH), lambda h: (h, 0, 0))
    lists = jax.ShapeDtypeStruct((halves, N_EXPERTS, _NH), jnp.int32)
    return pl.pallas_call(
        _dispatch_kernel,
        grid=(halves,),
        in_specs=[pl.BlockSpec((N_EXPERTS, _NH), lambda h: (0, h))],
        out_specs=[blk, blk, blk, pl.BlockSpec((1, 8, _SLOTS), lambda h: (h, 0, 0))],
        out_shape=[lists, lists, jax.ShapeDtypeStruct((halves, N_EXPERTS, _NH), F32),
                   jax.ShapeDtypeStruct((halves, 8, _SLOTS), jnp.int32)],
        compiler_params=_params(("parallel",)),
        name="dispatch",
    )(gates_t)


_GRP = 8
_MXU = 256
_KT = D_MODEL // _MXU
_UNF = 64
_TBL = 8 * _SLOTS
assert _NH * TOP_K // _R + N_EXPERTS + 4 <= _SLOTS


def _tbl(tbl_ref, h, row, k):
    return tbl_ref[h * _TBL + row * _SLOTS + jnp.clip(k, 0, _SLOTS - 1)]


def _moe_kernel(tbl_ref, src_hbm, dst_hbm, gate_hbm, hf_ref, wg_hbm, wu_hbm, wd_hbm, out_hbm,
                acc_ref, wf_refs, wb_refs, xg_ref, xa_ref, xb_ref, ya_ref, yb_ref,
                src_s, dst_s, gate_v, list_sem, w_sem, out_sem):
    h = pl.program_id(0)
    total = _tbl(tbl_ref, h, 3, 0)
    wgb_ref, wub_ref, wdb_ref = wb_refs

    for ref in (acc_ref, ya_ref, yb_ref):
        ref[...] = jnp.zeros_like(ref)

    @pl.when(h == 0)
    def _():
        for ref in (xa_ref, xb_ref, wgb_ref, wub_ref, wdb_ref):
            ref[...] = jnp.zeros_like(ref)

    def list_block(k):
        return (h * N_EXPERTS + _tbl(tbl_ref, h, 0, k)) * (_NH // _R) + _tbl(tbl_ref, h, 1, k)

    def new_expert(k):
        return (k >= 0) & (_tbl(tbl_ref, h, 2, k) == 1) & (_tbl(tbl_ref, h, 1, k) == 0)

    def list_copies(t, p):
        return (pltpu.make_async_copy(src_hbm.at[list_block(t)], src_s.at[p], list_sem.at[p, 0]),
                pltpu.make_async_copy(dst_hbm.at[list_block(t - 2)], dst_s.at[p], list_sem.at[p, 1]),
                pltpu.make_async_copy(gate_hbm.at[list_block(t - 1)], gate_v.at[1 - p], list_sem.at[p, 2]))

    def weight_copies(t, p):
        e = _tbl(tbl_ref, h, 0, t - 1)
        return [pltpu.make_async_copy(w_hbm.at[e], w_f.at[p], w_sem.at[p, n])
                for n, (w_hbm, w_f) in enumerate(zip((wg_hbm, wu_hbm, wd_hbm), wf_refs))]

    def prefetch(t, p):
        for cp in list_copies(t, p):
            cp.start()

        @pl.when(new_expert(t - 1))
        def _():
            for cp in weight_copies(t, p):
                cp.start()

    def arrive_lists(t, p):
        for cp in list_copies(t, p):
            cp.wait()

    def arrive_weights(t, p, cast):
        @pl.when(new_expert(t - 1))
        def _():
            for cp in weight_copies(t, p):
                cp.wait()
            if cast:
                for w_f, w_b in zip(wf_refs, wb_refs):
                    w_b[...] = w_f[p].astype(BF16)

    def row(ref, r):
        return ref.at[pl.ds(pl.multiple_of(r, _FOLD), _FOLD), :]

    def stages(t, p, x_next, x_mm, y_mm, y_scatter):
        mm_used = (t >= 1) & (_tbl(tbl_ref, h, 2, t - 1) == 1)
        src_ref, dst_ref, gate_ref = src_s.at[p, 0], dst_s.at[p, 0], gate_v.at[1 - p]

        def gather_group(j0):
            for j in range(j0, j0 + _GRP):
                row(xg_ref, j * _FOLD)[...] = row(hf_ref, src_ref[j])[...]

        def scatter_group(j0):
            dsts = [row(acc_ref, dst_ref[j0 + u]) for u in range(_GRP)]
            news = [dsts[u][...] + row(y_scatter, (j0 + u) * _FOLD)[...] for u in range(_GRP)]
            for u in range(_GRP):
                dsts[u][...] = news[u]

        def unfold_block(r0):
            for k in range(_FOLD):
                x_next[r0:r0 + _UNF, k * LANES:(k + 1) * LANES] = xg_ref[
                    pl.ds(r0 * _FOLD + k, _UNF, stride=_FOLD), :].astype(BF16)

        moves = []
        for j0 in range(0, _R, _GRP):
            moves += [functools.partial(scatter_group, j0), functools.partial(gather_group, j0)]
            if (j0 + _GRP) % _UNF == 0:
                moves.append(functools.partial(unfold_block, j0 + _GRP - _UNF))
        n_pieces = 3 * _KT
        done = [0]

        def some_moves():
            done[0] += 1
            for f in moves[(done[0] - 1) * len(moves) // n_pieces:done[0] * len(moves) // n_pieces]:
                f()

        diag = (lax.broadcasted_iota(jnp.int32, (_R, _R), 0) == lax.broadcasted_iota(jnp.int32, (_R, _R), 1))
        gates = gate_ref[...] * mm_used.astype(F32)
        gcol = jnp.sum(jnp.where(diag, gates, 0.0), axis=1, keepdims=True)
        gt = up = None
        for kt in range(_KT):
            rows_k = slice(kt * _MXU, (kt + 1) * _MXU)
            xk = x_mm[:, rows_k]
            d = jnp.dot(xk, wgb_ref[rows_k, :], preferred_element_type=F32)
            gt = d if gt is None else gt + d
            some_moves()
            d = jnp.dot(xk, wub_ref[rows_k, :], preferred_element_type=F32)
            up = d if up is None else up + d
            some_moves()
        act = (_silu(gt) * up * gcol).astype(BF16)
        for nt in range(D_MODEL // _MXU):
            y = jnp.dot(act, wdb_ref[:, nt * _MXU:(nt + 1) * _MXU], preferred_element_type=F32)
            some_moves()
            for k in range(_MXU // LANES):
                y_mm[pl.ds(nt * _MXU // LANES + k, _R, stride=_FOLD), :] = y[:, k * LANES:(k + 1) * LANES]

    def step(t, p):
        arrive_lists(t, p)
        prefetch(t + 1, 1 - p)
        arrive_weights(t, p, cast=True)
        if p == 0:
            stages(t, p, xa_ref, xb_ref, yb_ref, ya_ref)
        else:
            stages(t, p, xb_ref, xa_ref, ya_ref, yb_ref)

    def two_steps(i, carry):
        step(2 * i, 0)
        step(2 * i + 1, 1)
        return carry

    pairs = (total + 3) // 2
    prefetch(0, 0)
    lax.fori_loop(0, pairs, two_steps, 0)
    arrive_lists(2 * pairs, 0)
    arrive_weights(2 * pairs, 0, cast=False)

    rows = _NH * _FOLD
    cp = pltpu.make_async_copy(acc_ref.at[pl.ds(0, rows)],
                               out_hbm.at[pl.ds(pl.multiple_of(h * rows, rows), rows)], out_sem)
    cp.start()
    cp.wait()


def _moe(tbl, src, dst, gl, hfold, wg, wu, wd):
    halves = N_TOK // _NH
    hbm = pl.BlockSpec(memory_space=pl.ANY)
    stage = pltpu.VMEM((_R * _FOLD, LANES), F32)
    packed = pltpu.VMEM((_R, D_MODEL), BF16)
    up_shape, down_shape = (D_MODEL, D_EXPERT), (D_EXPERT, D_MODEL)
    return pl.pallas_call(
        _moe_kernel,
        grid_spec=pltpu.PrefetchScalarGridSpec(
            num_scalar_prefetch=1,
            grid=(halves,),
            in_specs=[hbm, hbm, hbm,
                      pl.BlockSpec((_NH * _FOLD, LANES), lambda h, tbl_ref: (h, 0)),
                      hbm, hbm, hbm],
            out_specs=hbm,
            scratch_shapes=[pltpu.VMEM(((_NH + 1) * _FOLD, LANES), F32),
                            tuple(pltpu.VMEM((2,) + s, F32) for s in (up_shape, up_shape, down_shape)),
                            tuple(pltpu.VMEM(s, BF16) for s in (up_shape, up_shape, down_shape)),
                            stage, packed, packed, stage, stage,
                            pltpu.SMEM((2, 1, _R), jnp.int32), pltpu.SMEM((2, 1, _R), jnp.int32),
                            pltpu.VMEM((2, 1, _R), F32),
                            pltpu.SemaphoreType.DMA((2, 3)), pltpu.SemaphoreType.DMA((2, 3)),
                            pltpu.SemaphoreType.DMA]),
        out_shape=jax.ShapeDtypeStruct((N_TOK * _FOLD, LANES), F32),
        compiler_params=_params(("arbitrary",)),
        name="moe",
    )(tbl, src, dst, gl, hfold, wg, wu, wd)


def _final_kernel(x_ref, r_ref, mod_ref, fg_ref, yp_ref, ys_ref):
    i = pl.program_id(0)
    routed = jnp.concatenate([r_ref[pl.ds(k, _TM, stride=_FOLD), :] for k in range(_FOLD)], axis=1)
    m = mod_ref[0]
    x2 = x_ref[...] + m[:, 5 * D_MODEL:6 * D_MODEL] * routed
    y = x2 * lax.rsqrt(jnp.mean(x2 * x2, axis=-1, keepdims=True) + EPS) * fg_ref[...]

    @pl.when(i < _CTX_TILES)
    def _():
        yp_ref[...] = y

    @pl.when(i >= _CTX_TILES)
    def _():
        ys_ref[...] = y


def _final(x1s, routed, mod3, fg):
    return pl.pallas_call(
        _final_kernel,
        grid=(N_TOK // _TM,),
        in_specs=[pl.BlockSpec((_TM, D_MODEL), lambda i: (i, 0)),
                  pl.BlockSpec((_TM * _FOLD, LANES), lambda i: (i, 0)),
                  pl.BlockSpec((1, 1, N_MOD * D_MODEL), lambda i: (_mod_row(i, _TM), 0, 0)),
                  pl.BlockSpec((1, D_MODEL), lambda i: (0, 0))],
        out_specs=[pl.BlockSpec((_TM, D_MODEL), _ctx_map),
                   pl.BlockSpec((_TM, D_MODEL), _lat_map)],
        out_shape=[jax.ShapeDtypeStruct((N_CTX, D_MODEL), F32),
                   jax.ShapeDtypeStruct((N_LAT, D_MODEL), F32)],
        compiler_params=_params(("arbitrary",)),
        name="final",
    )(x1s, routed, mod3, fg)


def kernel(x_prompt, x_sample, cache_k, cache_v, state_h_fwd, state_h_bwd, c, c_ctx, norm1_g, norm2_g, final_g, w_ada, b_ada, w_in, rpb, conv_w, conv_b, rg_wa, rg_ba, rg_wx, rg_bx, rg_lam, w_pa, w_pb, w_out, w_router, b_router, w_gate_e, w_up_e, w_down_e, w_gate_s, w_up_s, w_down_s):
    l = 0
    xp = x_prompt.reshape(N_CTX, D_MODEL)
    xs = x_sample.reshape(N_LAT, D_MODEL)

    cvecs = jnp.concatenate([c_ctx[None, :], c], axis=0)[:, :, None]
    mod = _adaln(cvecs, w_ada[l], b_ada[l][None, :])
    mod3 = mod.reshape(8, 1, N_MOD * D_MODEL)

    q, kc, vc, kl, vl, xr, yg, ga, gb = _inproj(xp, xs, mod3, norm1_g[l][None, :], w_in[l].astype(BF16))

    oac = _attn_ctx(q, kc, vc)
    ck = cache_k[:, l].reshape(DEC_BATCH, PAST_LEN, D_ATT)
    cv = cache_v[:, l].reshape(DEC_BATCH, PAST_LEN, D_ATT)
    oal = _attn_lat(q, kl, vl, ck, cv, _bias_tables(rpb[l]))

    w4 = jnp.concatenate([_block_diag(rg_wa[l, 0]), _block_diag(rg_wx[l, 0]),
                          _block_diag(rg_wa[l, 1]), _block_diag(rg_wx[l, 1])], axis=1).astype(BF16)
    b4 = jnp.concatenate([rg_ba[l, 0], rg_bx[l, 0], rg_ba[l, 1], rg_bx[l, 1]])[None, :]
    ob, hlf, hlb = _rglru(xr, yg, conv_w[l], conv_b[l][None, :], w4, b4, rg_lam[l],
                          state_h_fwd[:, l], state_h_bwd[:, l])

    wr_t = w_router[l].T
    wr_hi = wr_t.astype(BF16)
    wr_lo = (wr_t - wr_hi.astype(F32)).astype(BF16)
    x1s, hfold, gates_t = _merge(oac, oal, ob, ga, gb, xp, xs, mod3, norm2_g[l][None, :],
                                 w_pa[l].astype(BF16), w_pb[l].astype(BF16), w_out[l].astype(BF16),
                                 wr_hi, wr_lo, b_router[l][:, None],
                                 w_gate_s[l].astype(BF16), w_up_s[l].astype(BF16), w_down_s[l].astype(BF16))

    src, dst, gl, tbl = _dispatch(gates_t)
    routed = _moe(tbl.reshape(-1), src.reshape(-1, 1, _R), dst.reshape(-1, 1, _R), gl.reshape(-1, 1, _R), hfold,
                  w_gate_e[l], w_up_e[l], w_down_e[l])
    yp, ys = _final(x1s, routed, mod3, final_g[None, :])

    return (yp.reshape(BATCH, SEQ, D_MODEL),
            ys.reshape(DEC_BATCH, DEC_SEQ, D_MODEL),
            kc.reshape(BATCH, 1, SEQ, N_HEADS_A, HEAD_DIM_A),
            vc.reshape(BATCH, 1, SEQ, N_HEADS_A, HEAD_DIM_A),
            hlf.reshape(BATCH, 1, D_RNN),
            hlb.reshape(BATCH, 1, D_RNN))
```

```python
import functools
import math

import jax
import jax.numpy as jnp
from jax import lax
from jax.experimental import pallas as pl
from jax.experimental.pallas import tpu as pltpu

F32 = jnp.float32
BF16 = jnp.bfloat16

D_MODEL = 1024
BATCH = 16
SEQ = 256
DEC_BATCH = 2
DEC_SEQ = 1024
PAST_LEN = 512
GRID_W = 64
N_HEADS_A = 8
HEAD_DIM_A = 64
D_ATT = N_HEADS_A * HEAD_DIM_A
KH = 8
KW = 16
D_RNN = 512
N_RG_BLOCKS = 8
CONV_W = 4
RG_C = 8.0
N_EXPERTS = 64
TOP_K = 8
N_GROUPS = 8
GROUP_SIZE = N_EXPERTS // N_GROUPS
TOPK_GROUP = 4
D_EXPERT = 256
ROUTED_SCALE = 2.5
N_MOD = 6
EPS = 1e-6
NEG = -1e30
D_IN = 3 * D_ATT + 2 * D_RNN + 2 * D_MODEL

N_CTX = BATCH * SEQ
N_LAT = DEC_BATCH * DEC_SEQ
N_TOK = N_CTX + N_LAT
GRID_ROWS = DEC_SEQ // GRID_W

LANES = 128
VMEM_LIMIT = 56 * 1024 * 1024


def _params(sem, vmem=VMEM_LIMIT):
    return pltpu.CompilerParams(dimension_semantics=sem, vmem_limit_bytes=vmem)


def _mod_row(i, tile_rows):
    n_ctx_tiles = N_CTX // tile_rows
    return jnp.where(i < n_ctx_tiles, 0, 1 + (i - n_ctx_tiles) // (DEC_SEQ // tile_rows))


def _norm_mod(x, g, shift, scale):
    y = x * lax.rsqrt(jnp.mean(x * x, axis=-1, keepdims=True) + EPS)
    return (y * g) * (1.0 + scale) + shift


def _sigmoid(x):
    return 0.5 * jnp.tanh(0.5 * x) + 0.5


def _silu(x):
    return x * _sigmoid(x)


def _adaln_kernel(c_ref, w_ref, b_ref, o_ref):
    w = w_ref[...]
    rows = []
    for r in range(3):
        s = _silu(c_ref[r])
        rows.append(jnp.sum(w * s, axis=0, keepdims=True))
    rows.append(jnp.zeros((5, w.shape[1]), F32))
    o_ref[...] = jnp.concatenate(rows, axis=0) + b_ref[...]


def _adaln(cvecs, w_ada, b_ada):
    tn = 2048
    n = N_MOD * D_MODEL
    return pl.pallas_call(
        _adaln_kernel,
        grid=(n // tn,),
        in_specs=[pl.BlockSpec((3, D_MODEL, 1), lambda j: (0, 0, 0)),
                  pl.BlockSpec((D_MODEL, tn), lambda j: (0, j)),
                  pl.BlockSpec((1, tn), lambda j: (0, j))],
        out_specs=pl.BlockSpec((8, tn), lambda j: (0, j)),
        out_shape=jax.ShapeDtypeStruct((8, n), F32),
        compiler_params=_params(("parallel",)),
        name="adaln",
    )(cvecs, w_ada, b_ada)


_TM = 512
_CTX_TILES = N_CTX // _TM
_LAT_TILES = N_LAT // _TM


def _ctx_map(i):
    return (jnp.minimum(i, _CTX_TILES - 1), 0)


def _lat_map(i):
    return (jnp.maximum(i - _CTX_TILES, 0), 0)


def _inproj_kernel(xp_ref, xs_ref, mod_ref, g_ref, w_ref,
                   q_ref, kc_ref, vc_ref, kl_ref, vl_ref, xr_ref, yg_ref, ga_ref, gb_ref):
    i = pl.program_id(0)
    is_ctx = i < _CTX_TILES
    x = jnp.where(is_ctx, xp_ref[...], xs_ref[...])
    m = mod_ref[0]
    h = _norm_mod(x, g_ref[...], m[:, 0:D_MODEL], m[:, D_MODEL:2 * D_MODEL]).astype(BF16)

    def proj(a, b):
        return jnp.dot(h, w_ref[:, a:b], preferred_element_type=F32)

    q_ref[...] = proj(0, D_ATT).astype(BF16)
    k = proj(D_ATT, 2 * D_ATT)
    v = proj(2 * D_ATT, 3 * D_ATT)

    @pl.when(is_ctx)
    def _():
        kc_ref[...] = k
        vc_ref[...] = v

    @pl.when(jnp.logical_not(is_ctx))
    def _():
        kl_ref[...] = k
        vl_ref[...] = v

    o = 3 * D_ATT
    xr_ref[...] = proj(o, o + D_RNN)
    yg_ref[...] = proj(o + D_RNN, o + 2 * D_RNN).astype(BF16)
    o += 2 * D_RNN
    ga_ref[...] = proj(o, o + D_MODEL).astype(BF16)
    gb_ref[...] = proj(o + D_MODEL, o + 2 * D_MODEL).astype(BF16)


def _inproj(xp, xs, mod3, g1, w_in):
    row = lambda i: (i, 0)
    return pl.pallas_call(
        _inproj_kernel,
        grid=(N_TOK // _TM,),
        in_specs=[pl.BlockSpec((_TM, D_MODEL), _ctx_map),
                  pl.BlockSpec((_TM, D_MODEL), _lat_map),
                  pl.BlockSpec((1, 1, N_MOD * D_MODEL), lambda i: (_mod_row(i, _TM), 0, 0)),
                  pl.BlockSpec((1, D_MODEL), lambda i: (0, 0)),
                  pl.BlockSpec((D_MODEL, D_IN), lambda i: (0, 0))],
        out_specs=[pl.BlockSpec((_TM, D_ATT), row),
                   pl.BlockSpec((_TM, D_ATT), _ctx_map),
                   pl.BlockSpec((_TM, D_ATT), _ctx_map),
                   pl.BlockSpec((_TM, D_ATT), _lat_map),
                   pl.BlockSpec((_TM, D_ATT), _lat_map),
                   pl.BlockSpec((_TM, D_RNN), row),
                   pl.BlockSpec((_TM, D_RNN), row),
                   pl.BlockSpec((_TM, D_MODEL), row),
                   pl.BlockSpec((_TM, D_MODEL), row)],
        out_shape=[jax.ShapeDtypeStruct((N_TOK, D_ATT), BF16),
                   jax.ShapeDtypeStruct((N_CTX, D_ATT), F32),
                   jax.ShapeDtypeStruct((N_CTX, D_ATT), F32),
                   jax.ShapeDtypeStruct((N_LAT, D_ATT), F32),
                   jax.ShapeDtypeStruct((N_LAT, D_ATT), F32),
                   jax.ShapeDtypeStruct((N_TOK, D_RNN), F32),
                   jax.ShapeDtypeStruct((N_TOK, D_RNN), BF16),
                   jax.ShapeDtypeStruct((N_TOK, D_MODEL), BF16),
                   jax.ShapeDtypeStruct((N_TOK, D_MODEL), BF16)],
        compiler_params=_params(("arbitrary",)),
        name="inproj",
    )(xp, xs, mod3, g1, w_in)


def _pair_attention(qp, segs):
    scale = HEAD_DIM_A ** -0.5
    assert math.log2(scale).is_integer()
    lane = lax.broadcasted_iota(jnp.int32, (1, LANES), 1)
    qs = qp * scale
    outs = []
    for a in range(2):
        sel = (lane >= HEAD_DIM_A) if a else (lane < HEAD_DIM_A)
        qm = jnp.where(sel, qs, jnp.zeros_like(qs))
        ss = []
        for kp, _, bias_fn in segs:
            s = lax.dot_general(qm, kp, (((1,), (1,)), ((), ())), preferred_element_type=F32)
            if bias_fn is not None:
                s = s + bias_fn(a)
            ss.append(s)
        m = functools.reduce(jnp.maximum, [jnp.max(s, axis=-1, keepdims=True) for s in ss])
        es = [jnp.exp(s - m) for s in ss]
        inv = 1.0 / functools.reduce(jnp.add, [jnp.sum(e, axis=-1, keepdims=True) for e in es])
        o = inv * functools.reduce(jnp.add, [
            jnp.dot(e.astype(BF16), vp, preferred_element_type=F32)
            for e, (_, vp, _) in zip(es, segs)])
        outs.append(o)
    return jnp.where(lane < HEAD_DIM_A, outs[0], outs[1])


def _attn_ctx_kernel(q_ref, k_ref, v_ref, o_ref):
    for j in range(D_ATT // LANES):
        c = slice(j * LANES, (j + 1) * LANES)
        segs = [(k_ref[:, c].astype(BF16), v_ref[:, c].astype(BF16), None)]
        o_ref[:, c] = _pair_attention(q_ref[:, c], segs).astype(BF16)


def _attn_ctx(q_all, kc, vc):
    blk = pl.BlockSpec((SEQ, D_ATT), lambda b: (b, 0))
    return pl.pallas_call(
        _attn_ctx_kernel,
        grid=(BATCH,),
        in_specs=[blk, blk, blk],
        out_specs=blk,
        out_shape=jax.ShapeDtypeStruct((N_CTX, D_ATT), BF16),
        compiler_params=_params(("parallel",)),
        name="attn_ctx",
    )(q_all, kc, vc)


_QROWS = 4
_QB = _QROWS * GRID_W
_KROWS = 12
_KB = _KROWS * GRID_W


def _build_bias_tables(rows_ref, t2_ref):
    wq = lax.broadcasted_iota(jnp.int32, (GRID_W, LANES), 0)
    wk = lax.broadcasted_iota(jnp.int32, (GRID_W, LANES), 1) & (GRID_W - 1)
    col_start = jnp.clip(wq - KW // 2, 0, GRID_W - KW)
    col_in = (wk >= col_start) & (wk < col_start + KW)
    for r in range(N_HEADS_A * 2 * KH):
        x = jnp.broadcast_to(rows_ref[r:r + 1, :], (GRID_W, LANES))
        x = pltpu.roll(x, LANES - (KW - 1), 1, stride=1, stride_axis=0)
        t2_ref[r // (2 * KH), r % (2 * KH)] = jnp.where(col_in, x, NEG)


def _attn_lat_kernel(q_ref, k_ref, v_ref, ck_ref, cv_ref, rows_ref, o_ref, t2_ref):
    i = pl.program_id(1)

    @pl.when((pl.program_id(0) == 0) & (i == 0))
    def _():
        _build_bias_tables(rows_ref, t2_ref)

    ks = jnp.where(i < 2, 0, GRID_ROWS - _KROWS)
    kstart = pl.multiple_of(ks * GRID_W, 256)
    lane = lax.broadcasted_iota(jnp.int32, (1, LANES), 1)

    def bias_for_head(h):
        rows = []
        for rq in range(_QROWS):
            r = _QROWS * i + rq
            rs = jnp.clip(r - KH // 2, 0, GRID_ROWS - KH)
            tiles = []
            for t in range(_KROWS // 2):
                kr = ks + 2 * t
                d = jnp.clip(kr - r + (KH - 1), -1, 2 * KH - 2)
                tile = t2_ref[h, d + 1]
                v0 = ((kr >= rs) & (kr < rs + KH)).astype(jnp.int32)
                v1 = ((kr + 1 >= rs) & (kr + 1 < rs + KH)).astype(jnp.int32)
                valid = jnp.where(lane < GRID_W, v0, v1) > 0
                tiles.append(jnp.where(valid, tile, NEG))
            rows.append(jnp.concatenate(tiles, axis=1))
        return jnp.concatenate(rows, axis=0)

    for j in range(D_ATT // LANES):
        c = slice(j * LANES, (j + 1) * LANES)
        k_loc = k_ref[pl.ds(kstart, _KB), c].astype(BF16)
        v_loc = v_ref[pl.ds(kstart, _KB), c].astype(BF16)
        segs = [(k_loc, v_loc, lambda a, j=j: bias_for_head(2 * j + a)),
                (ck_ref[:, c].astype(BF16), cv_ref[:, c].astype(BF16), None)]
        o_ref[:, c] = _pair_attention(q_ref[:, c], segs).astype(BF16)


def _attn_lat(q_all, kl, vl, ck, cv, t2):
    qblocks = DEC_SEQ // _QB
    return pl.pallas_call(
        _attn_lat_kernel,
        grid=(DEC_BATCH, qblocks),
        in_specs=[pl.BlockSpec((_QB, D_ATT), lambda b, i: (N_CTX // _QB + b * qblocks + i, 0)),
                  pl.BlockSpec((DEC_SEQ, D_ATT), lambda b, i: (b, 0)),
                  pl.BlockSpec((DEC_SEQ, D_ATT), lambda b, i: (b, 0)),
                  pl.BlockSpec((None, PAST_LEN, D_ATT), lambda b, i: (b, 0, 0)),
                  pl.BlockSpec((None, PAST_LEN, D_ATT), lambda b, i: (b, 0, 0)),
                  pl.BlockSpec((N_HEADS_A * 2 * KH, LANES), lambda b, i: (0, 0))],
        out_specs=pl.BlockSpec((_QB, D_ATT), lambda b, i: (b * qblocks + i, 0)),
        out_shape=jax.ShapeDtypeStruct((N_LAT, D_ATT), BF16),
        scratch_shapes=[pltpu.VMEM((N_HEADS_A, 2 * KH, GRID_W, LANES), F32)],
        compiler_params=_params(("arbitrary", "arbitrary")),
        name="attn_lat",
    )(q_all, kl, vl, ck, cv, t2)


def _bias_tables(rpb):
    half = jnp.pad(rpb, ((0, 0), (0, 0), (0, GRID_W - rpb.shape[-1])))
    neg = jnp.full((N_HEADS_A, 1, GRID_W), NEG, F32)
    left = jnp.concatenate([neg, half], axis=1)
    right = jnp.concatenate([half, neg], axis=1)
    return jnp.concatenate([left, right], axis=-1).reshape(N_HEADS_A * 2 * KH, LANES)


_RG_ROWS = 2048
_RG_CHUNK = 256
_RG_SLABS = D_RNN // LANES
_LOG2E = 1.4426950408889634
_RG_CTX_STEPS = N_CTX // _RG_ROWS


def _rg_block(seq_len, h0f, h0b, xr_ref, yg_ref, cw_ref, cb_ref, w4_ref, b4_ref, lam_ref,
              ob_ref, xc_ref, af_ref, uf_ref, ab_ref, ub_ref, hf_ref, hb_ref, cf_ref, cr_ref):
    T, R = _RG_CHUNK, _RG_ROWS
    nb = R // T
    segs = seq_len // T
    linked = segs > 1
    xr = xr_ref[...]
    t_idx = lax.broadcasted_iota(jnp.int32, (R, 1), 0) & (seq_len - 1)
    cw = cw_ref[...]
    xc = jnp.zeros((R, D_RNN), F32) + cb_ref[...]
    for j in range(CONV_W):
        off = j - 2
        if off == 0:
            tap = xr
        else:
            tap = pltpu.roll(xr, shift=(-off) % R, axis=0)
            tap = jnp.where((t_idx + off >= 0) & (t_idx + off < seq_len), tap, 0.0)
        xc = xc + cw[j:j + 1, :] * tap
    xc_ref[...] = xc

    lam = lam_ref[...]
    z = -lam
    sp = jnp.maximum(z, 0.0) + jnp.log1p(jnp.exp(-jnp.abs(z)))
    decay_k = (-0.5 * RG_C * _LOG2E) * sp

    def put(ref, c, val):
        for k in range(_RG_SLABS):
            ref[k, pl.ds(c, T, stride=nb), :] = val[:, k * LANES:(k + 1) * LANES]

    def get(ref, c):
        return jnp.concatenate([ref[k, pl.ds(c, T, stride=nb), :] for k in range(_RG_SLABS)], axis=1)

    def gate_chunk(c, carry):
        x = xc_ref[pl.ds(pl.multiple_of(c * T, T), T), :]
        g4 = jnp.dot(x.astype(BF16), w4_ref[...], preferred_element_type=F32) + b4_ref[...]
        hx = 0.5 * x
        for d, (a_ref, u_ref) in enumerate(((af_ref, uf_ref), (ab_ref, ub_ref))):
            tr = jnp.tanh(0.5 * g4[:, (2 * d) * D_RNN:(2 * d + 1) * D_RNN])
            ti = jnp.tanh(0.5 * g4[:, (2 * d + 1) * D_RNN:(2 * d + 2) * D_RNN])
            k = decay_k[d:d + 1, :]
            a = jnp.exp2(k * tr + k)
            put(a_ref, c, a)
            put(u_ref, c, jnp.sqrt(jnp.maximum(1.0 - a * a, 0.0)) * (hx * ti + hx))
        return carry

    lax.fori_loop(0, nb, gate_chunk, 0)

    def step(t, carry):
        hf, hb, pf, pb = carry
        rf = pl.ds(pl.multiple_of(t * nb, nb), nb)
        rb = pl.ds(pl.multiple_of((T - 1 - t) * nb, nb), nb)
        nf, nbk, npf, npb = [], [], [], []
        for k in range(_RG_SLABS):
            a_f, a_b = af_ref[k, rf, :], ab_ref[k, rb, :]
            f = a_f * hf[k] + uf_ref[k, rf, :]
            b = a_b * hb[k] + ub_ref[k, rb, :]
            hf_ref[k, rf, :] = f
            hb_ref[k, rb, :] = b
            nf.append(f)
            nbk.append(b)
            if linked:
                npf.append(a_f * pf[k])
                npb.append(a_b * pb[k])
                af_ref[k, rf, :] = npf[-1]
                ab_ref[k, rb, :] = npb[-1]
        return tuple(nf), tuple(nbk), tuple(npf), tuple(npb)

    zeros = (jnp.zeros((nb, LANES), F32),) * _RG_SLABS
    ones = (jnp.ones((nb, LANES), F32),) * _RG_SLABS if linked else ()
    hf, hb, pf, pb = lax.fori_loop(0, T, step, (zeros, zeros, ones, ones))

    if linked:
        seg_row = lax.broadcasted_iota(jnp.int32, (nb, LANES), 0)
        for k in range(_RG_SLABS):
            lanes = slice(k * LANES, (k + 1) * LANES)
            enter_f = jnp.zeros((nb, LANES), F32)
            enter_b = jnp.zeros((nb, LANES), F32)
            for s0 in range(0, nb, segs):
                c = h0f[s0 // segs:s0 // segs + 1, lanes]
                for q in range(s0, s0 + segs):
                    enter_f = jnp.where(seg_row == q, c, enter_f)
                    c = hf[k][q:q + 1, :] + pf[k][q:q + 1, :] * c
                c = h0b[s0 // segs:s0 // segs + 1, lanes]
                for q in reversed(range(s0, s0 + segs)):
                    enter_b = jnp.where(seg_row == q, c, enter_b)
                    c = hb[k][q:q + 1, :] + pb[k][q:q + 1, :] * c
            cf_ref[:, lanes] = enter_f
            cr_ref[:, lanes] = enter_b

    def out_chunk(c, carry):
        rows = pl.ds(pl.multiple_of(c * T, T), T)
        hsum = get(hf_ref, c) + get(hb_ref, c)
        if linked:
            hsum = hsum + get(af_ref, c) * cf_ref[pl.ds(c, 1), :] + get(ab_ref, c) * cr_ref[pl.ds(c, 1), :]
        ob_ref[rows, :] = (hsum * jax.nn.gelu(yg_ref[rows, :].astype(F32))).astype(BF16)
        return carry

    lax.fori_loop(0, nb, out_chunk, 0)
    return hf, hb


def _rglru_kernel(xr_ref, yg_ref, cw_ref, cb_ref, w4_ref, b4_ref, lam_ref, sf_ref, sb_ref,
                  ob_ref, hlf_ref, hlb_ref, *scratch):
    i = pl.program_id(0)
    shared = (xr_ref, yg_ref, cw_ref, cb_ref, w4_ref, b4_ref, lam_ref, ob_ref) + scratch

    @pl.when(i < _RG_CTX_STEPS)
    def _():
        hf, hb = _rg_block(SEQ, None, None, *shared)
        hlf_ref[...] = jnp.concatenate(hf, axis=1)
        hlb_ref[...] = jnp.concatenate(hb, axis=1)

    @pl.when(i >= _RG_CTX_STEPS)
    def _():
        _rg_block(DEC_SEQ, sf_ref[...], sb_ref[...], *shared)


def _rglru(xr, yg, conv_w, conv_b, w4, b4, lam, sf, sb):
    steps = N_TOK // _RG_ROWS
    nb_ctx = _RG_ROWS // SEQ
    const = lambda shape: pl.BlockSpec(shape, lambda i: (0,) * len(shape))
    hl_map = lambda i: (jnp.minimum(i, _RG_CTX_STEPS - 1), 0)
    scan_buf = pltpu.VMEM((_RG_SLABS, _RG_ROWS, LANES), F32)
    enter_buf = pltpu.VMEM((_RG_ROWS // _RG_CHUNK, D_RNN), F32)
    return pl.pallas_call(
        _rglru_kernel,
        grid=(steps,),
        in_specs=[pl.BlockSpec((_RG_ROWS, D_RNN), lambda i: (i, 0)),
                  pl.BlockSpec((_RG_ROWS, D_RNN), lambda i: (i, 0)),
                  const((CONV_W, D_RNN)), const((1, D_RNN)),
                  const((D_RNN, 4 * D_RNN)), const((1, 4 * D_RNN)), const((2, D_RNN)),
                  const((DEC_BATCH, D_RNN)), const((DEC_BATCH, D_RNN))],
        out_specs=[pl.BlockSpec((_RG_ROWS, D_RNN), lambda i: (i, 0)),
                   pl.BlockSpec((nb_ctx, D_RNN), hl_map),
                   pl.BlockSpec((nb_ctx, D_RNN), hl_map)],
        out_shape=[jax.ShapeDtypeStruct((N_TOK, D_RNN), BF16),
                   jax.ShapeDtypeStruct((BATCH, D_RNN), F32),
                   jax.ShapeDtypeStruct((BATCH, D_RNN), F32)],
        scratch_shapes=[pltpu.VMEM((_RG_ROWS, D_RNN), F32)] + [scan_buf] * 6 + [enter_buf] * 2,
        compiler_params=_params(("arbitrary",)),
        name="rglru",
    )(xr, yg, conv_w, conv_b, w4, b4, lam, sf, sb)


def _block_diag(w):
    n, d, _ = w.shape
    eye = jnp.eye(n, dtype=w.dtype)
    return jnp.einsum('nij,nm->nimj', w, eye).reshape(n * d, n * d)


def _nt_dot(a, b):
    return lax.dot_general(a, b, (((1,), (1,)), ((), ())), preferred_element_type=F32)


def _route(s, sb):
    n = s.shape[1]
    low = -3.0e38
    sb3 = sb.reshape(N_GROUPS, GROUP_SIZE, n)
    iw = lax.broadcasted_iota(jnp.int32, sb3.shape, 1)
    m1 = jnp.max(sb3, axis=1, keepdims=True)
    i1 = jnp.min(jnp.where(sb3 == m1, iw, GROUP_SIZE), axis=1, keepdims=True)
    m2 = jnp.max(jnp.where(iw == i1, low, sb3), axis=1, keepdims=True)
    gscore = m1 + m2
    ig = lax.broadcasted_iota(jnp.int32, gscore.shape, 0)
    beaten = jnp.zeros(gscore.shape, jnp.int32)
    for g in range(N_GROUPS):
        row = gscore[g:g + 1]
        beaten = beaten + ((row > gscore) | ((row == gscore) & (g < ig))).astype(jnp.int32)
    gsel = beaten < TOPK_GROUP
    work = jnp.where(gsel, sb3, NEG).reshape(N_EXPERTS, n)
    ie = lax.broadcasted_iota(jnp.int32, work.shape, 0)
    chosen = jnp.zeros(work.shape, jnp.bool_)
    for _ in range(TOP_K):
        m = jnp.max(work, axis=0, keepdims=True)
        pick = ie == jnp.min(jnp.where(work == m, ie, N_EXPERTS), axis=0, keepdims=True)
        chosen = chosen | pick
        work = jnp.where(pick, low, work)
    wsel = jnp.where(chosen, s, 0.0)
    return wsel / jnp.sum(wsel, axis=0, keepdims=True) * ROUTED_SCALE


def _merge_kernel(oac_ref, oal_ref, ob_ref, ga_ref, gb_ref, xp_ref, xs_ref, mod_ref, g_ref,
                  wpa_ref, wpb_ref, wo_ref, wrh_ref, wrl_ref, br_ref, wgs_ref, wus_ref, wds_ref,
                  x1s_ref, hf_ref, gates_ref):
    i = pl.program_id(0)
    is_ctx = i < _CTX_TILES
    oa = jnp.where(is_ctx, oac_ref[...], oal_ref[...])
    x = jnp.where(is_ctx, xp_ref[...], xs_ref[...])
    m = mod_ref[0]
    sl = lambda k: m[:, k * D_MODEL:(k + 1) * D_MODEL]
    mix = (_sigmoid(ga_ref[...].astype(F32)) * jnp.dot(oa, wpa_ref[...], preferred_element_type=F32)
           + _sigmoid(gb_ref[...].astype(F32)) * jnp.dot(ob_ref[...], wpb_ref[...], preferred_element_type=F32))
    o = jnp.dot(mix.astype(BF16), wo_ref[...], preferred_element_type=F32)
    x1 = x + sl(2) * o
    h2 = _norm_mod(x1, g_ref[...], sl(3), sl(4))
    for c in range(_FOLD):
        hf_ref[pl.ds(c, _TM, stride=_FOLD), :] = h2[:, c * LANES:(c + 1) * LANES]
    h_hi = h2.astype(BF16)
    a = _silu(jnp.dot(h_hi, wgs_ref[...], preferred_element_type=F32)) * jnp.dot(
        h_hi, wus_ref[...], preferred_element_type=F32)
    x1s_ref[...] = x1 + sl(5) * jnp.dot(a.astype(BF16), wds_ref[...], preferred_element_type=F32)
    h_lo = (h2 - h_hi.astype(F32)).astype(BF16)
    logits = _nt_dot(wrh_ref[...], h_hi) + _nt_dot(wrh_ref[...], h_lo) + _nt_dot(wrl_ref[...], h_hi)
    s = jax.nn.sigmoid(logits)
    gates_ref[...] = _route(s, s + br_ref[...])


def _merge(oac, oal, ob, ga, gb, xp, xs, mod3, g2, wpa, wpb, wo, wrh, wrl, br, wgs, wus, wds):
    row = lambda i: (i, 0)
    const = lambda shape: pl.BlockSpec(shape, lambda i: (0,) * len(shape))
    return pl.pallas_call(
        _merge_kernel,
        grid=(N_TOK // _TM,),
        in_specs=[pl.BlockSpec((_TM, D_ATT), _ctx_map),
                  pl.BlockSpec((_TM, D_ATT), _lat_map),
                  pl.BlockSpec((_TM, D_RNN), row),
                  pl.BlockSpec((_TM, D_MODEL), row),
                  pl.BlockSpec((_TM, D_MODEL), row),
                  pl.BlockSpec((_TM, D_MODEL), _ctx_map),
                  pl.BlockSpec((_TM, D_MODEL), _lat_map),
                  pl.BlockSpec((1, 1, N_MOD * D_MODEL), lambda i: (_mod_row(i, _TM), 0, 0)),
                  const((1, D_MODEL)),
                  const((D_ATT, D_MODEL)), const((D_RNN, D_MODEL)), const((D_MODEL, D_MODEL)),
                  const((N_EXPERTS, D_MODEL)), const((N_EXPERTS, D_MODEL)), const((N_EXPERTS, 1)),
                  const((D_MODEL, D_EXPERT)), const((D_MODEL, D_EXPERT)), const((D_EXPERT, D_MODEL))],
        out_specs=[pl.BlockSpec((_TM, D_MODEL), row),
                   pl.BlockSpec((_TM * _FOLD, LANES), row),
                   pl.BlockSpec((N_EXPERTS, _TM), lambda i: (0, i))],
        out_shape=[jax.ShapeDtypeStruct((N_TOK, D_MODEL), F32),
                   jax.ShapeDtypeStruct((N_TOK * _FOLD, LANES), F32),
                   jax.ShapeDtypeStruct((N_EXPERTS, N_TOK), F32)],
        compiler_params=_params(("parallel",)),
        name="merge",
    )(oac, oal, ob, ga, gb, xp, xs, mod3, g2, wpa, wpb, wo, wrh, wrl, br, wgs, wus, wds)


_FOLD = D_MODEL // LANES
_NH = N_TOK // 2
_R = 448
_LIST = -(-_NH // _R) * _R
_SLOTS = 256
_ID_BITS = 13
_DISP_BITS = 12
_VALID_BIT = _ID_BITS + _DISP_BITS + 1


def _chunk_table(cnt):
    shape = (N_EXPERTS, _SLOTS)
    nch = jnp.broadcast_to((cnt + (_R - 1)) // _R, shape)
    erow = lax.broadcasted_iota(jnp.int32, shape, 0)
    q = lax.broadcasted_iota(jnp.int32, shape, 1)
    cum = nch
    s = 1
    while s < N_EXPERTS:
        cum = cum + jnp.where(erow >= s, pltpu.roll(cum, s, 0), 0)
        s *= 2
    total = cum[N_EXPERTS - 1:N_EXPERTS, :]
    used = q[:1] < total
    expert = jnp.minimum(jnp.sum((cum <= q).astype(jnp.int32), axis=0, keepdims=True), N_EXPERTS - 1)
    first = cum - nch
    start = jnp.max(jnp.where(first <= q, first, 0), axis=0, keepdims=True)
    chunk = jnp.where(used, q[:1] - start, 0)
    return jnp.concatenate([expert, chunk, used.astype(jnp.int32), total,
                            jnp.zeros((4, _SLOTS), jnp.int32)], axis=0)


def _dispatch_kernel(g_ref, src_ref, dst_ref, gl_ref, tbl_ref, cnt_ref):
    n = g_ref.shape[1]
    ne = 16
    lane = lax.broadcasted_iota(jnp.int32, (ne, n), 1)

    def expert_group(i, carry):
        rows = pl.ds(pl.multiple_of(i * ne, ne), ne)
        g = g_ref[rows, :]
        sel = g > 0.0
        m = sel.astype(jnp.int32)
        csum = m
        s = 1
        while s < n:
            csum = csum + jnp.where(lane >= s, pltpu.roll(csum, s, 1), 0)
            s *= 2
        cnt_ref[rows, :] = jnp.broadcast_to(csum[:, n - 1:n], (ne, LANES))
        disp = lane - (csum - m)
        packed = jnp.where(sel, (1 << _VALID_BIT) | (disp << _ID_BITS) | lane, 0)
        g = jnp.where(sel, g, 0.0)
        for b in range(_DISP_BITS):
            shift = n - (1 << b)
            inc = pltpu.roll(packed, shift, 1)
            inc_moves = ((inc >> (_ID_BITS + b)) & 1) == 1
            own_moves = ((packed >> (_ID_BITS + b)) & 1) == 1
            packed = jnp.where(inc_moves, inc, jnp.where(own_moves, 0, packed))
            g = jnp.where(inc_moves, pltpu.roll(g, shift, 1), jnp.where(own_moves, 0.0, g))
        valid = (packed >> _VALID_BIT) == 1
        tok = packed & ((1 << _ID_BITS) - 1)
        src_ref[0, rows, :n] = jnp.where(valid, tok, 0) * _FOLD
        dst_ref[0, rows, :n] = jnp.where(valid, tok, n) * _FOLD
        gl_ref[0, rows, :n] = g
        pad = (ne, _LIST - n)
        src_ref[0, rows, n:] = jnp.zeros(pad, jnp.int32)
        dst_ref[0, rows, n:] = jnp.full(pad, n * _FOLD, jnp.int32)
        gl_ref[0, rows, n:] = jnp.zeros(pad, F32)
        return carry

    lax.fori_loop(0, N_EXPERTS // ne, expert_group, 0)
    tbl_ref[0] = _chunk_table(cnt_ref[:, :1])


def _dispatch(gates_t):
    halves = N_TOK // _NH
    blk = pl.BlockSpec((1, N_EXPERTS, _LIST), lambda h: (h, 0, 0))
    lists = jax.ShapeDtypeStruct((halves, N_EXPERTS, _LIST), jnp.int32)
    return pl.pallas_call(
        _dispatch_kernel,
        grid=(halves,),
        in_specs=[pl.BlockSpec((N_EXPERTS, _NH), lambda h: (0, h))],
        out_specs=[blk, blk, blk, pl.BlockSpec((1, 8, _SLOTS), lambda h: (h, 0, 0))],
        out_shape=[lists, lists, jax.ShapeDtypeStruct((halves, N_EXPERTS, _LIST), F32),
                   jax.ShapeDtypeStruct((halves, 8, _SLOTS), jnp.int32)],
        scratch_shapes=[pltpu.VMEM((N_EXPERTS, LANES), jnp.int32)],
        compiler_params=_params(("parallel",)),
        name="dispatch",
    )(gates_t)


_GRP = 8
_MXU = 256
_KT = D_MODEL // _MXU
_UNF = 64
_TBL = 8 * _SLOTS
assert _NH * TOP_K // _R + N_EXPERTS + 4 <= _SLOTS


def _tbl(tbl_ref, h, row, k):
    return tbl_ref[h * _TBL + row * _SLOTS + jnp.clip(k, 0, _SLOTS - 1)]


def _moe_kernel(tbl_ref, src_hbm, dst_hbm, gate_hbm, hf_ref, wg_hbm, wu_hbm, wd_hbm, out_hbm,
                acc_ref, wf_refs, wb_refs, xg_ref, xa_ref, xb_ref, ya_ref, yb_ref,
                src_s, dst_s, gate_v, list_sem, w_sem, out_sem):
    h = pl.program_id(0)
    total = _tbl(tbl_ref, h, 3, 0)
    wgb_ref, wub_ref, wdb_ref = wb_refs

    for ref in (acc_ref, ya_ref, yb_ref):
        ref[...] = jnp.zeros_like(ref)

    @pl.when(h == 0)
    def _():
        for ref in (xa_ref, xb_ref, wgb_ref, wub_ref, wdb_ref):
            ref[...] = jnp.zeros_like(ref)

    def list_block(k):
        return (h * N_EXPERTS + _tbl(tbl_ref, h, 0, k)) * (_LIST // _R) + _tbl(tbl_ref, h, 1, k)

    def new_expert(k):
        return (k >= 0) & (_tbl(tbl_ref, h, 2, k) == 1) & (_tbl(tbl_ref, h, 1, k) == 0)

    def list_copies(t, p):
        return (pltpu.make_async_copy(src_hbm.at[list_block(t)], src_s.at[p], list_sem.at[p, 0]),
                pltpu.make_async_copy(dst_hbm.at[list_block(t - 2)], dst_s.at[p], list_sem.at[p, 1]),
                pltpu.make_async_copy(gate_hbm.at[list_block(t - 1)], gate_v.at[1 - p], list_sem.at[p, 2]))

    def weight_copies(t, p):
        e = _tbl(tbl_ref, h, 0, t - 1)
        return [pltpu.make_async_copy(w_hbm.at[e], w_f.at[p], w_sem.at[p, n])
                for n, (w_hbm, w_f) in enumerate(zip((wg_hbm, wu_hbm, wd_hbm), wf_refs))]

    def prefetch(t, p):
        for cp in list_copies(t, p):
            cp.start()

        @pl.when(new_expert(t - 1))
        def _():
            for cp in weight_copies(t, p):
                cp.start()

    def arrive_lists(t, p):
        for cp in list_copies(t, p):
            cp.wait()

    def arrive_weights(t, p, cast):
        @pl.when(new_expert(t - 1))
        def _():
            for cp in weight_copies(t, p):
                cp.wait()
            if cast:
                for w_f, w_b in zip(wf_refs, wb_refs):
                    w_b[...] = w_f[p].astype(BF16)

    def row(ref, r):
        return ref.at[pl.ds(pl.multiple_of(r, _FOLD), _FOLD), :]

    def stages(t, p, x_next, x_mm, y_mm, y_scatter):
        mm_used = (t >= 1) & (_tbl(tbl_ref, h, 2, t - 1) == 1)
        src_ref, dst_ref, gate_ref = src_s.at[p, 0], dst_s.at[p, 0], gate_v.at[1 - p]

        def gather_group(j0):
            for j in range(j0, j0 + _GRP):
                row(xg_ref, j * _FOLD)[...] = row(hf_ref, src_ref[j])[...]

        def scatter_group(j0):
            dsts = [row(acc_ref, dst_ref[j0 + u]) for u in range(_GRP)]
            news = [dsts[u][...] + row(y_scatter, (j0 + u) * _FOLD)[...] for u in range(_GRP)]
            for u in range(_GRP):
                dsts[u][...] = news[u]

        def unfold_block(r0):
            for k in range(_FOLD):
                x_next[r0:r0 + _UNF, k * LANES:(k + 1) * LANES] = xg_ref[
                    pl.ds(r0 * _FOLD + k, _UNF, stride=_FOLD), :].astype(BF16)

        moves = []
        for j0 in range(0, _R, _GRP):
            moves += [functools.partial(scatter_group, j0), functools.partial(gather_group, j0)]
            if (j0 + _GRP) % _UNF == 0:
                moves.append(functools.partial(unfold_block, j0 + _GRP - _UNF))
        n_pieces = 3 * _KT
        done = [0]

        def some_moves():
            done[0] += 1
            for f in moves[(done[0] - 1) * len(moves) // n_pieces:done[0] * len(moves) // n_pieces]:
                f()

        diag = (lax.broadcasted_iota(jnp.int32, (_R, _R), 0) == lax.broadcasted_iota(jnp.int32, (_R, _R), 1))
        gates = gate_ref[...] * mm_used.astype(F32)
        gcol = jnp.sum(jnp.where(diag, gates, 0.0), axis=1, keepdims=True)
        gt = up = None
        for kt in range(_KT):
            rows_k = slice(kt * _MXU, (kt + 1) * _MXU)
            xk = x_mm[:, rows_k]
            d = jnp.dot(xk, wgb_ref[rows_k, :], preferred_element_type=F32)
            gt = d if gt is None else gt + d
            some_moves()
            d = jnp.dot(xk, wub_ref[rows_k, :], preferred_element_type=F32)
            up = d if up is None else up + d
            some_moves()
        act = (_silu(gt) * up * gcol).astype(BF16)
        for nt in range(D_MODEL // _MXU):
            y = jnp.dot(act, wdb_ref[:, nt * _MXU:(nt + 1) * _MXU], preferred_element_type=F32)
            some_moves()
            for k in range(_MXU // LANES):
                y_mm[pl.ds(nt * _MXU // LANES + k, _R, stride=_FOLD), :] = y[:, k * LANES:(k + 1) * LANES]

    def step(t, p):
        arrive_lists(t, p)
        prefetch(t + 1, 1 - p)
        arrive_weights(t, p, cast=True)
        if p == 0:
            stages(t, p, xa_ref, xb_ref, yb_ref, ya_ref)
        else:
            stages(t, p, xb_ref, xa_ref, ya_ref, yb_ref)

    def two_steps(i, carry):
        step(2 * i, 0)
        step(2 * i + 1, 1)
        return carry

    pairs = (total + 3) // 2
    prefetch(0, 0)
    lax.fori_loop(0, pairs, two_steps, 0)
    arrive_lists(2 * pairs, 0)
    arrive_weights(2 * pairs, 0, cast=False)

    rows = _NH * _FOLD
    cp = pltpu.make_async_copy(acc_ref.at[pl.ds(0, rows)],
                               out_hbm.at[pl.ds(pl.multiple_of(h * rows, rows), rows)], out_sem)
    cp.start()
    cp.wait()


def _moe(tbl, src, dst, gl, hfold, wg, wu, wd):
    halves = N_TOK // _NH
    hbm = pl.BlockSpec(memory_space=pl.ANY)
    stage = pltpu.VMEM((_R * _FOLD, LANES), F32)
    packed = pltpu.VMEM((_R, D_MODEL), BF16)
    up_shape, down_shape = (D_MODEL, D_EXPERT), (D_EXPERT, D_MODEL)
    return pl.pallas_call(
        _moe_kernel,
        grid_spec=pltpu.PrefetchScalarGridSpec(
            num_scalar_prefetch=1,
            grid=(halves,),
            in_specs=[hbm, hbm, hbm,
                      pl.BlockSpec((_NH * _FOLD, LANES), lambda h, tbl_ref: (h, 0)),
                      hbm, hbm, hbm],
            out_specs=hbm,
            scratch_shapes=[pltpu.VMEM(((_NH + 1) * _FOLD, LANES), F32),
                            tuple(pltpu.VMEM((2,) + s, F32) for s in (up_shape, up_shape, down_shape)),
                            tuple(pltpu.VMEM(s, BF16) for s in (up_shape, up_shape, down_shape)),
                            stage, packed, packed, stage, stage,
                            pltpu.SMEM((2, 1, _R), jnp.int32), pltpu.SMEM((2, 1, _R), jnp.int32),
                            pltpu.VMEM((2, 1, _R), F32),
                            pltpu.SemaphoreType.DMA((2, 3)), pltpu.SemaphoreType.DMA((2, 3)),
                            pltpu.SemaphoreType.DMA]),
        out_shape=jax.ShapeDtypeStruct((N_TOK * _FOLD, LANES), F32),
        compiler_params=_params(("arbitrary",)),
        name="moe",
    )(tbl, src, dst, gl, hfold, wg, wu, wd)


def _final_kernel(x_ref, r_ref, mod_ref, fg_ref, yp_ref, ys_ref):
    i = pl.program_id(0)
    routed = jnp.concatenate([r_ref[pl.ds(k, _TM, stride=_FOLD), :] for k in range(_FOLD)], axis=1)
    m = mod_ref[0]
    x2 = x_ref[...] + m[:, 5 * D_MODEL:6 * D_MODEL] * routed
    y = x2 * lax.rsqrt(jnp.mean(x2 * x2, axis=-1, keepdims=True) + EPS) * fg_ref[...]

    @pl.when(i < _CTX_TILES)
    def _():
        yp_ref[...] = y

    @pl.when(i >= _CTX_TILES)
    def _():
        ys_ref[...] = y


def _final(x1s, routed, mod3, fg):
    return pl.pallas_call(
        _final_kernel,
        grid=(N_TOK // _TM,),
        in_specs=[pl.BlockSpec((_TM, D_MODEL), lambda i: (i, 0)),
                  pl.BlockSpec((_TM * _FOLD, LANES), lambda i: (i, 0)),
                  pl.BlockSpec((1, 1, N_MOD * D_MODEL), lambda i: (_mod_row(i, _TM), 0, 0)),
                  pl.BlockSpec((1, D_MODEL), lambda i: (0, 0))],
        out_specs=[pl.BlockSpec((_TM, D_MODEL), _ctx_map),
                   pl.BlockSpec((_TM, D_MODEL), _lat_map)],
        out_shape=[jax.ShapeDtypeStruct((N_CTX, D_MODEL), F32),
                   jax.ShapeDtypeStruct((N_LAT, D_MODEL), F32)],
        compiler_params=_params(("arbitrary",)),
        name="final",
    )(x1s, routed, mod3, fg)


def kernel(x_prompt, x_sample, cache_k, cache_v, state_h_fwd, state_h_bwd, c, c_ctx, norm1_g, norm2_g, final_g, w_ada, b_ada, w_in, rpb, conv_w, conv_b, rg_wa, rg_ba, rg_wx, rg_bx, rg_lam, w_pa, w_pb, w_out, w_router, b_router, w_gate_e, w_up_e, w_down_e, w_gate_s, w_up_s, w_down_s):
    l = 0
    xp = x_prompt.reshape(N_CTX, D_MODEL)
    xs = x_sample.reshape(N_LAT, D_MODEL)

    cvecs = jnp.concatenate([c_ctx[None, :], c], axis=0)[:, :, None]
    mod = _adaln(cvecs, w_ada[l], b_ada[l][None, :])
    mod3 = mod.reshape(8, 1, N_MOD * D_MODEL)

    q, kc, vc, kl, vl, xr, yg, ga, gb = _inproj(xp, xs, mod3, norm1_g[l][None, :], w_in[l].astype(BF16))

    oac = _attn_ctx(q, kc, vc)
    ck = cache_k[:, l].reshape(DEC_BATCH, PAST_LEN, D_ATT)
    cv = cache_v[:, l].reshape(DEC_BATCH, PAST_LEN, D_ATT)
    oal = _attn_lat(q, kl, vl, ck, cv, _bias_tables(rpb[l]))

    w4 = jnp.concatenate([_block_diag(rg_wa[l, 0]), _block_diag(rg_wx[l, 0]),
                          _block_diag(rg_wa[l, 1]), _block_diag(rg_wx[l, 1])], axis=1).astype(BF16)
    b4 = jnp.concatenate([rg_ba[l, 0], rg_bx[l, 0], rg_ba[l, 1], rg_bx[l, 1]])[None, :]
    ob, hlf, hlb = _rglru(xr, yg, conv_w[l], conv_b[l][None, :], w4, b4, rg_lam[l],
                          state_h_fwd[:, l], state_h_bwd[:, l])

    wr_t = w_router[l].T
    wr_hi = wr_t.astype(BF16)
    wr_lo = (wr_t - wr_hi.astype(F32)).astype(BF16)
    x1s, hfold, gates_t = _merge(oac, oal, ob, ga, gb, xp, xs, mod3, norm2_g[l][None, :],
                                 w_pa[l].astype(BF16), w_pb[l].astype(BF16), w_out[l].astype(BF16),
                                 wr_hi, wr_lo, b_router[l][:, None],
                                 w_gate_s[l].astype(BF16), w_up_s[l].astype(BF16), w_down_s[l].astype(BF16))

    src, dst, gl, tbl = _dispatch(gates_t)
    routed = _moe(tbl.reshape(-1), src.reshape(-1, 1, _R), dst.reshape(-1, 1, _R), gl.reshape(-1, 1, _R), hfold,
                  w_gate_e[l], w_up_e[l], w_down_e[l])
    yp, ys = _final(x1s, routed, mod3, final_g[None, :])

    return (yp.reshape(BATCH, SEQ, D_MODEL),
            ys.reshape(DEC_BATCH, DEC_SEQ, D_MODEL),
            kc.reshape(BATCH, 1, SEQ, N_HEADS_A, HEAD_DIM_A),
            vc.reshape(BATCH, 1, SEQ, N_HEADS_A, HEAD_DIM_A),
            hlf.reshape(BATCH, 1, D_RNN),
            hlb.reshape(BATCH, 1, D_RNN))
```

```python
import functools
import math

import jax
import jax.numpy as jnp
from jax import lax
from jax.experimental import pallas as pl
from jax.experimental.pallas import tpu as pltpu

F32 = jnp.float32
BF16 = jnp.bfloat16

D_MODEL = 1024
BATCH = 16
SEQ = 256
DEC_BATCH = 2
DEC_SEQ = 1024
PAST_LEN = 512
GRID_W = 64
N_HEADS_A = 8
HEAD_DIM_A = 64
D_ATT = N_HEADS_A * HEAD_DIM_A
KH = 8
KW = 16
D_RNN = 512
N_RG_BLOCKS = 8
CONV_W = 4
RG_C = 8.0
N_EXPERTS = 64
TOP_K = 8
N_GROUPS = 8
GROUP_SIZE = N_EXPERTS // N_GROUPS
TOPK_GROUP = 4
D_EXPERT = 256
ROUTED_SCALE = 2.5
N_MOD = 6
EPS = 1e-6
NEG = -1e30
D_IN = 3 * D_ATT + 2 * D_RNN + 2 * D_MODEL

N_CTX = BATCH * SEQ
N_LAT = DEC_BATCH * DEC_SEQ
N_TOK = N_CTX + N_LAT
GRID_ROWS = DEC_SEQ // GRID_W

LANES = 128
VMEM_LIMIT = 56 * 1024 * 1024


def _params(sem, vmem=VMEM_LIMIT):
    return pltpu.CompilerParams(dimension_semantics=sem, vmem_limit_bytes=vmem)


def _mod_row(i, tile_rows):
    n_ctx_tiles = N_CTX // tile_rows
    return jnp.where(i < n_ctx_tiles, 0, 1 + (i - n_ctx_tiles) // (DEC_SEQ // tile_rows))


def _norm_mod(x, g, shift, scale):
    y = x * lax.rsqrt(jnp.mean(x * x, axis=-1, keepdims=True) + EPS)
    return (y * g) * (1.0 + scale) + shift


def _sigmoid(x):
    return 0.5 * jnp.tanh(0.5 * x) + 0.5


def _silu(x):
    return x * _sigmoid(x)


def _adaln_kernel(c_ref, w_ref, b_ref, o_ref):
    w = w_ref[...]
    rows = []
    for r in range(3):
        s = _silu(c_ref[r])
        rows.append(jnp.sum(w * s, axis=0, keepdims=True))
    rows.append(jnp.zeros((5, w.shape[1]), F32))
    o_ref[...] = jnp.concatenate(rows, axis=0) + b_ref[...]


def _adaln(cvecs, w_ada, b_ada):
    tn = 2048
    n = N_MOD * D_MODEL
    return pl.pallas_call(
        _adaln_kernel,
        grid=(n // tn,),
        in_specs=[pl.BlockSpec((3, D_MODEL, 1), lambda j: (0, 0, 0)),
                  pl.BlockSpec((D_MODEL, tn), lambda j: (0, j)),
                  pl.BlockSpec((1, tn), lambda j: (0, j))],
        out_specs=pl.BlockSpec((8, tn), lambda j: (0, j)),
        out_shape=jax.ShapeDtypeStruct((8, n), F32),
        compiler_params=_params(("parallel",)),
        name="adaln",
    )(cvecs, w_ada, b_ada)


_TM = 512
_CTX_TILES = N_CTX // _TM
_LAT_TILES = N_LAT // _TM


def _ctx_map(i):
    return (jnp.minimum(i, _CTX_TILES - 1), 0)


def _lat_map(i):
    return (jnp.maximum(i - _CTX_TILES, 0), 0)


def _inproj_kernel(xp_ref, xs_ref, mod_ref, g_ref, w_ref,
                   q_ref, kc_ref, vc_ref, kl_ref, vl_ref, xr_ref, yg_ref, ga_ref, gb_ref):
    i = pl.program_id(0)
    is_ctx = i < _CTX_TILES
    x = jnp.where(is_ctx, xp_ref[...], xs_ref[...])
    m = mod_ref[0]
    h = _norm_mod(x, g_ref[...], m[:, 0:D_MODEL], m[:, D_MODEL:2 * D_MODEL]).astype(BF16)

    def proj(a, b):
        return jnp.dot(h, w_ref[:, a:b], preferred_element_type=F32)

    q_ref[...] = proj(0, D_ATT).astype(BF16)
    k = proj(D_ATT, 2 * D_ATT)
    v = proj(2 * D_ATT, 3 * D_ATT)

    @pl.when(is_ctx)
    def _():
        kc_ref[...] = k
        vc_ref[...] = v

    @pl.when(jnp.logical_not(is_ctx))
    def _():
        kl_ref[...] = k
        vl_ref[...] = v

    o = 3 * D_ATT
    xr_ref[...] = proj(o, o + D_RNN)
    yg_ref[...] = proj(o + D_RNN, o + 2 * D_RNN).astype(BF16)
    o += 2 * D_RNN
    ga_ref[...] = proj(o, o + D_MODEL).astype(BF16)
    gb_ref[...] = proj(o + D_MODEL, o + 2 * D_MODEL).astype(BF16)


def _inproj(xp, xs, mod3, g1, w_in):
    row = lambda i: (i, 0)
    return pl.pallas_call(
        _inproj_kernel,
        grid=(N_TOK // _TM,),
        in_specs=[pl.BlockSpec((_TM, D_MODEL), _ctx_map),
                  pl.BlockSpec((_TM, D_MODEL), _lat_map),
                  pl.BlockSpec((1, 1, N_MOD * D_MODEL), lambda i: (_mod_row(i, _TM), 0, 0)),
                  pl.BlockSpec((1, D_MODEL), lambda i: (0, 0)),
                  pl.BlockSpec((D_MODEL, D_IN), lambda i: (0, 0))],
        out_specs=[pl.BlockSpec((_TM, D_ATT), row),
                   pl.BlockSpec((_TM, D_ATT), _ctx_map),
                   pl.BlockSpec((_TM, D_ATT), _ctx_map),
                   pl.BlockSpec((_TM, D_ATT), _lat_map),
                   pl.BlockSpec((_TM, D_ATT), _lat_map),
                   pl.BlockSpec((_TM, D_RNN), row),
                   pl.BlockSpec((_TM, D_RNN), row),
                   pl.BlockSpec((_TM, D_MODEL), row),
                   pl.BlockSpec((_TM, D_MODEL), row)],
        out_shape=[jax.ShapeDtypeStruct((N_TOK, D_ATT), BF16),
                   jax.ShapeDtypeStruct((N_CTX, D_ATT), F32),
                   jax.ShapeDtypeStruct((N_CTX, D_ATT), F32),
                   jax.ShapeDtypeStruct((N_LAT, D_ATT), F32),
                   jax.ShapeDtypeStruct((N_LAT, D_ATT), F32),
                   jax.ShapeDtypeStruct((N_TOK, D_RNN), F32),
                   jax.ShapeDtypeStruct((N_TOK, D_RNN), BF16),
                   jax.ShapeDtypeStruct((N_TOK, D_MODEL), BF16),
                   jax.ShapeDtypeStruct((N_TOK, D_MODEL), BF16)],
        compiler_params=_params(("arbitrary",)),
        name="inproj",
    )(xp, xs, mod3, g1, w_in)


def _pair_attention(qp, segs):
    scale = HEAD_DIM_A ** -0.5
    assert math.log2(scale).is_integer()
    lane = lax.broadcasted_iota(jnp.int32, (1, LANES), 1)
    qs = qp * scale
    outs = []
    for a in range(2):
        sel = (lane >= HEAD_DIM_A) if a else (lane < HEAD_DIM_A)
        qm = jnp.where(sel, qs, jnp.zeros_like(qs))
        ss = []
        for kp, _, bias_fn in segs:
            s = lax.dot_general(qm, kp, (((1,), (1,)), ((), ())), preferred_element_type=F32)
            if bias_fn is not None:
                s = s + bias_fn(a)
            ss.append(s)
        m = functools.reduce(jnp.maximum, [jnp.max(s, axis=-1, keepdims=True) for s in ss])
        es = [jnp.exp(s - m) for s in ss]
        inv = 1.0 / functools.reduce(jnp.add, [jnp.sum(e, axis=-1, keepdims=True) for e in es])
        o = inv * functools.reduce(jnp.add, [
            jnp.dot(e.astype(BF16), vp, preferred_element_type=F32)
            for e, (_, vp, _) in zip(es, segs)])
        outs.append(o)
    return jnp.where(lane < HEAD_DIM_A, outs[0], outs[1])


def _attn_ctx_kernel(q_ref, k_ref, v_ref, o_ref):
    for j in range(D_ATT // LANES):
        c = slice(j * LANES, (j + 1) * LANES)
        segs = [(k_ref[:, c].astype(BF16), v_ref[:, c].astype(BF16), None)]
        o_ref[:, c] = _pair_attention(q_ref[:, c], segs).astype(BF16)


def _attn_ctx(q_all, kc, vc):
    blk = pl.BlockSpec((SEQ, D_ATT), lambda b: (b, 0))
    return pl.pallas_call(
        _attn_ctx_kernel,
        grid=(BATCH,),
        in_specs=[blk, blk, blk],
        out_specs=blk,
        out_shape=jax.ShapeDtypeStruct((N_CTX, D_ATT), BF16),
        compiler_params=_params(("parallel",)),
        name="attn_ctx",
    )(q_all, kc, vc)


_QROWS = 4
_QB = _QROWS * GRID_W
_KROWS = 12
_KB = _KROWS * GRID_W


def _build_bias_tables(rows_ref, t2_ref):
    wq = lax.broadcasted_iota(jnp.int32, (GRID_W, LANES), 0)
    wk = lax.broadcasted_iota(jnp.int32, (GRID_W, LANES), 1) & (GRID_W - 1)
    col_start = jnp.clip(wq - KW // 2, 0, GRID_W - KW)
    col_in = (wk >= col_start) & (wk < col_start + KW)
    for r in range(N_HEADS_A * 2 * KH):
        x = jnp.broadcast_to(rows_ref[r:r + 1, :], (GRID_W, LANES))
        x = pltpu.roll(x, LANES - (KW - 1), 1, stride=1, stride_axis=0)
        t2_ref[r // (2 * KH), r % (2 * KH)] = jnp.where(col_in, x, NEG)


def _attn_lat_kernel(q_ref, k_ref, v_ref, ck_ref, cv_ref, rows_ref, o_ref, t2_ref):
    i = pl.program_id(1)

    @pl.when((pl.program_id(0) == 0) & (i == 0))
    def _():
        _build_bias_tables(rows_ref, t2_ref)

    ks = jnp.where(i < 2, 0, GRID_ROWS - _KROWS)
    kstart = pl.multiple_of(ks * GRID_W, 256)
    lane = lax.broadcasted_iota(jnp.int32, (1, LANES), 1)

    def bias_for_head(h):
        rows = []
        for rq in range(_QROWS):
            r = _QROWS * i + rq
            rs = jnp.clip(r - KH // 2, 0, GRID_ROWS - KH)
            tiles = []
            for t in range(_KROWS // 2):
                kr = ks + 2 * t
                d = jnp.clip(kr - r + (KH - 1), -1, 2 * KH - 2)
                tile = t2_ref[h, d + 1]
                v0 = ((kr >= rs) & (kr < rs + KH)).astype(jnp.int32)
                v1 = ((kr + 1 >= rs) & (kr + 1 < rs + KH)).astype(jnp.int32)
                valid = jnp.where(lane < GRID_W, v0, v1) > 0
                tiles.append(jnp.where(valid, tile, NEG))
            rows.append(jnp.concatenate(tiles, axis=1))
        return jnp.concatenate(rows, axis=0)

    for j in range(D_ATT // LANES):
        c = slice(j * LANES, (j + 1) * LANES)
        k_loc = k_ref[pl.ds(kstart, _KB), c].astype(BF16)
        v_loc = v_ref[pl.ds(kstart, _KB), c].astype(BF16)
        segs = [(k_loc, v_loc, lambda a, j=j: bias_for_head(2 * j + a)),
                (ck_ref[:, c].astype(BF16), cv_ref[:, c].astype(BF16), None)]
        o_ref[:, c] = _pair_attention(q_ref[:, c], segs).astype(BF16)


def _attn_lat(q_all, kl, vl, ck, cv, t2):
    qblocks = DEC_SEQ // _QB
    return pl.pallas_call(
        _attn_lat_kernel,
        grid=(DEC_BATCH, qblocks),
        in_specs=[pl.BlockSpec((_QB, D_ATT), lambda b, i: (N_CTX // _QB + b * qblocks + i, 0)),
                  pl.BlockSpec((DEC_SEQ, D_ATT), lambda b, i: (b, 0)),
                  pl.BlockSpec((DEC_SEQ, D_ATT), lambda b, i: (b, 0)),
                  pl.BlockSpec((None, PAST_LEN, D_ATT), lambda b, i: (b, 0, 0)),
                  pl.BlockSpec((None, PAST_LEN, D_ATT), lambda b, i: (b, 0, 0)),
                  pl.BlockSpec((N_HEADS_A * 2 * KH, LANES), lambda b, i: (0, 0))],
        out_specs=pl.BlockSpec((_QB, D_ATT), lambda b, i: (b * qblocks + i, 0)),
        out_shape=jax.ShapeDtypeStruct((N_LAT, D_ATT), BF16),
        scratch_shapes=[pltpu.VMEM((N_HEADS_A, 2 * KH, GRID_W, LANES), F32)],
        compiler_params=_params(("arbitrary", "arbitrary")),
        name="attn_lat",
    )(q_all, kl, vl, ck, cv, t2)


def _bias_tables(rpb):
    half = jnp.pad(rpb, ((0, 0), (0, 0), (0, GRID_W - rpb.shape[-1])))
    neg = jnp.full((N_HEADS_A, 1, GRID_W), NEG, F32)
    left = jnp.concatenate([neg, half], axis=1)
    right = jnp.concatenate([half, neg], axis=1)
    return jnp.concatenate([left, right], axis=-1).reshape(N_HEADS_A * 2 * KH, LANES)


_RG_ROWS = 2048
_RG_CHUNK = 256
_RG_SLABS = D_RNN // LANES
_LOG2E = 1.4426950408889634
_RG_CTX_STEPS = N_CTX // _RG_ROWS


def _rg_block(seq_len, h0f, h0b, xr_ref, yg_ref, cw_ref, cb_ref, w4_ref, b4_ref, lam_ref,
              ob_ref, xc_ref, af_ref, uf_ref, ab_ref, ub_ref, hf_ref, hb_ref, cf_ref, cr_ref):
    T, R = _RG_CHUNK, _RG_ROWS
    nb = R // T
    segs = seq_len // T
    linked = segs > 1
    xr = xr_ref[...]
    t_idx = lax.broadcasted_iota(jnp.int32, (R, 1), 0) & (seq_len - 1)
    cw = cw_ref[...]
    xc = jnp.zeros((R, D_RNN), F32) + cb_ref[...]
    for j in range(CONV_W):
        off = j - 2
        if off == 0:
            tap = xr
        else:
            tap = pltpu.roll(xr, shift=(-off) % R, axis=0)
            tap = jnp.where((t_idx + off >= 0) & (t_idx + off < seq_len), tap, 0.0)
        xc = xc + cw[j:j + 1, :] * tap
    xc_ref[...] = xc

    lam = lam_ref[...]
    z = -lam
    sp = jnp.maximum(z, 0.0) + jnp.log1p(jnp.exp(-jnp.abs(z)))
    decay_k = (-0.5 * RG_C * _LOG2E) * sp

    def put(ref, c, val):
        for k in range(_RG_SLABS):
            ref[k, pl.ds(c, T, stride=nb), :] = val[:, k * LANES:(k + 1) * LANES]

    def get(ref, c):
        return jnp.concatenate([ref[k, pl.ds(c, T, stride=nb), :] for k in range(_RG_SLABS)], axis=1)

    def gate_chunk(c, carry):
        x = xc_ref[pl.ds(pl.multiple_of(c * T, T), T), :]
        g4 = jnp.dot(x.astype(BF16), w4_ref[...], preferred_element_type=F32) + b4_ref[...]
        hx = 0.5 * x
        for d, (a_ref, u_ref) in enumerate(((af_ref, uf_ref), (ab_ref, ub_ref))):
            tr = jnp.tanh(0.5 * g4[:, (2 * d) * D_RNN:(2 * d + 1) * D_RNN])
            ti = jnp.tanh(0.5 * g4[:, (2 * d + 1) * D_RNN:(2 * d + 2) * D_RNN])
            k = decay_k[d:d + 1, :]
            a = jnp.exp2(k * tr + k)
            put(a_ref, c, a)
            put(u_ref, c, jnp.sqrt(jnp.maximum(1.0 - a * a, 0.0)) * (hx * ti + hx))
        return carry

    lax.fori_loop(0, nb, gate_chunk, 0)

    def step(t, carry):
        hf, hb, pf, pb = carry
        rf = pl.ds(pl.multiple_of(t * nb, nb), nb)
        rb = pl.ds(pl.multiple_of((T - 1 - t) * nb, nb), nb)
        nf, nbk, npf, npb = [], [], [], []
        for k in range(_RG_SLABS):
            a_f, a_b = af_ref[k, rf, :], ab_ref[k, rb, :]
            f = a_f * hf[k] + uf_ref[k, rf, :]
            b = a_b * hb[k] + ub_ref[k, rb, :]
            hf_ref[k, rf, :] = f
            hb_ref[k, rb, :] = b
            nf.append(f)
            nbk.append(b)
            if linked:
                npf.append(a_f * pf[k])
                npb.append(a_b * pb[k])
                af_ref[k, rf, :] = npf[-1]
                ab_ref[k, rb, :] = npb[-1]
        return tuple(nf), tuple(nbk), tuple(npf), tuple(npb)

    zeros = (jnp.zeros((nb, LANES), F32),) * _RG_SLABS
    ones = (jnp.ones((nb, LANES), F32),) * _RG_SLABS if linked else ()
    hf, hb, pf, pb = lax.fori_loop(0, T, step, (zeros, zeros, ones, ones))

    if linked:
        seg_row = lax.broadcasted_iota(jnp.int32, (nb, LANES), 0)
        for k in range(_RG_SLABS):
            lanes = slice(k * LANES, (k + 1) * LANES)
            enter_f = jnp.zeros((nb, LANES), F32)
            enter_b = jnp.zeros((nb, LANES), F32)
            for s0 in range(0, nb, segs):
                c = h0f[s0 // segs:s0 // segs + 1, lanes]
                for q in range(s0, s0 + segs):
                    enter_f = jnp.where(seg_row == q, c, enter_f)
                    c = hf[k][q:q + 1, :] + pf[k][q:q + 1, :] * c
                c = h0b[s0 // segs:s0 // segs + 1, lanes]
                for q in reversed(range(s0, s0 + segs)):
                    enter_b = jnp.where(seg_row == q, c, enter_b)
                    c = hb[k][q:q + 1, :] + pb[k][q:q + 1, :] * c
            cf_ref[:, lanes] = enter_f
            cr_ref[:, lanes] = enter_b

    def out_chunk(c, carry):
        rows = pl.ds(pl.multiple_of(c * T, T), T)
        hsum = get(hf_ref, c) + get(hb_ref, c)
        if linked:
            hsum = hsum + get(af_ref, c) * cf_ref[pl.ds(c, 1), :] + get(ab_ref, c) * cr_ref[pl.ds(c, 1), :]
        ob_ref[rows, :] = (hsum * jax.nn.gelu(yg_ref[rows, :].astype(F32))).astype(BF16)
        return carry

    lax.fori_loop(0, nb, out_chunk, 0)
    return hf, hb


def _rglru_kernel(xr_ref, yg_ref, cw_ref, cb_ref, w4_ref, b4_ref, lam_ref, sf_ref, sb_ref,
                  ob_ref, hlf_ref, hlb_ref, *scratch):
    i = pl.program_id(0)
    shared = (xr_ref, yg_ref, cw_ref, cb_ref, w4_ref, b4_ref, lam_ref, ob_ref) + scratch

    @pl.when(i < _RG_CTX_STEPS)
    def _():
        hf, hb = _rg_block(SEQ, None, None, *shared)
        hlf_ref[...] = jnp.concatenate(hf, axis=1)
        hlb_ref[...] = jnp.concatenate(hb, axis=1)

    @pl.when(i >= _RG_CTX_STEPS)
    def _():
        _rg_block(DEC_SEQ, sf_ref[...], sb_ref[...], *shared)


def _rglru(xr, yg, conv_w, conv_b, w4, b4, lam, sf, sb):
    steps = N_TOK // _RG_ROWS
    nb_ctx = _RG_ROWS // SEQ
    const = lambda shape: pl.BlockSpec(shape, lambda i: (0,) * len(shape))
    hl_map = lambda i: (jnp.minimum(i, _RG_CTX_STEPS - 1), 0)
    scan_buf = pltpu.VMEM((_RG_SLABS, _RG_ROWS, LANES), F32)
    enter_buf = pltpu.VMEM((_RG_ROWS // _RG_CHUNK, D_RNN), F32)
    return pl.pallas_call(
        _rglru_kernel,
        grid=(steps,),
        in_specs=[pl.BlockSpec((_RG_ROWS, D_RNN), lambda i: (i, 0)),
                  pl.BlockSpec((_RG_ROWS, D_RNN), lambda i: (i, 0)),
                  const((CONV_W, D_RNN)), const((1, D_RNN)),
                  const((D_RNN, 4 * D_RNN)), const((1, 4 * D_RNN)), const((2, D_RNN)),
                  const((DEC_BATCH, D_RNN)), const((DEC_BATCH, D_RNN))],
        out_specs=[pl.BlockSpec((_RG_ROWS, D_RNN), lambda i: (i, 0)),
                   pl.BlockSpec((nb_ctx, D_RNN), hl_map),
                   pl.BlockSpec((nb_ctx, D_RNN), hl_map)],
        out_shape=[jax.ShapeDtypeStruct((N_TOK, D_RNN), BF16),
                   jax.ShapeDtypeStruct((BATCH, D_RNN), F32),
                   jax.ShapeDtypeStruct((BATCH, D_RNN), F32)],
        scratch_shapes=[pltpu.VMEM((_RG_ROWS, D_RNN), F32)] + [scan_buf] * 6 + [enter_buf] * 2,
        compiler_params=_params(("arbitrary",)),
        name="rglru",
    )(xr, yg, conv_w, conv_b, w4, b4, lam, sf, sb)


def _block_diag(w):
    n, d, _ = w.shape
    eye = jnp.eye(n, dtype=w.dtype)
    return jnp.einsum('nij,nm->nimj', w, eye).reshape(n * d, n * d)


def _nt_dot(a, b):
    return lax.dot_general(a, b, (((1,), (1,)), ((), ())), preferred_element_type=F32)


def _route(s, sb):
    n = s.shape[1]
    low = -3.0e38
    sb3 = sb.reshape(N_GROUPS, GROUP_SIZE, n)
    iw = lax.broadcasted_iota(jnp.int32, sb3.shape, 1)
    m1 = jnp.max(sb3, axis=1, keepdims=True)
    i1 = jnp.min(jnp.where(sb3 == m1, iw, GROUP_SIZE), axis=1, keepdims=True)
    m2 = jnp.max(jnp.where(iw == i1, low, sb3), axis=1, keepdims=True)
    gscore = m1 + m2
    ig = lax.broadcasted_iota(jnp.int32, gscore.shape, 0)
    beaten = jnp.zeros(gscore.shape, jnp.int32)
    for g in range(N_GROUPS):
        row = gscore[g:g + 1]
        beaten = beaten + ((row > gscore) | ((row == gscore) & (g < ig))).astype(jnp.int32)
    gsel = beaten < TOPK_GROUP
    work = jnp.where(gsel, sb3, NEG).reshape(N_EXPERTS, n)
    ie = lax.broadcasted_iota(jnp.int32, work.shape, 0)
    chosen = jnp.zeros(work.shape, jnp.bool_)
    for _ in range(TOP_K):
        m = jnp.max(work, axis=0, keepdims=True)
        pick = ie == jnp.min(jnp.where(work == m, ie, N_EXPERTS), axis=0, keepdims=True)
        chosen = chosen | pick
        work = jnp.where(pick, low, work)
    wsel = jnp.where(chosen, s, 0.0)
    return wsel / jnp.sum(wsel, axis=0, keepdims=True) * ROUTED_SCALE


def _merge_kernel(oac_ref, oal_ref, ob_ref, ga_ref, gb_ref, xp_ref, xs_ref, mod_ref, g_ref,
                  wpa_ref, wpb_ref, wo_ref, wrh_ref, wrl_ref, br_ref, wgs_ref, wus_ref, wds_ref,
                  x1s_ref, hf_ref, gates_ref):
    i = pl.program_id(0)
    is_ctx = i < _CTX_TILES
    oa = jnp.where(is_ctx, oac_ref[...], oal_ref[...])
    x = jnp.where(is_ctx, xp_ref[...], xs_ref[...])
    m = mod_ref[0]
    sl = lambda k: m[:, k * D_MODEL:(k + 1) * D_MODEL]
    mix = (_sigmoid(ga_ref[...].astype(F32)) * jnp.dot(oa, wpa_ref[...], preferred_element_type=F32)
           + _sigmoid(gb_ref[...].astype(F32)) * jnp.dot(ob_ref[...], wpb_ref[...], preferred_element_type=F32))
    o = jnp.dot(mix.astype(BF16), wo_ref[...], preferred_element_type=F32)
    x1 = x + sl(2) * o
    h2 = _norm_mod(x1, g_ref[...], sl(3), sl(4))
    for c in range(_FOLD):
        hf_ref[pl.ds(c, _TM, stride=_FOLD), :] = h2[:, c * LANES:(c + 1) * LANES]
    h_hi = h2.astype(BF16)
    a = _silu(jnp.dot(h_hi, wgs_ref[...], preferred_element_type=F32)) * jnp.dot(
        h_hi, wus_ref[...], preferred_element_type=F32)
    x1s_ref[...] = x1 + sl(5) * jnp.dot(a.astype(BF16), wds_ref[...], preferred_element_type=F32)
    h_lo = (h2 - h_hi.astype(F32)).astype(BF16)
    logits = _nt_dot(wrh_ref[...], h_hi) + _nt_dot(wrh_ref[...], h_lo) + _nt_dot(wrl_ref[...], h_hi)
    s = jax.nn.sigmoid(logits)
    gates_ref[...] = _route(s, s + br_ref[...])


def _merge(oac, oal, ob, ga, gb, xp, xs, mod3, g2, wpa, wpb, wo, wrh, wrl, br, wgs, wus, wds):
    row = lambda i: (i, 0)
    const = lambda shape: pl.BlockSpec(shape, lambda i: (0,) * len(shape))
    return pl.pallas_call(
        _merge_kernel,
        grid=(N_TOK // _TM,),
        in_specs=[pl.BlockSpec((_TM, D_ATT), _ctx_map),
                  pl.BlockSpec((_TM, D_ATT), _lat_map),
                  pl.BlockSpec((_TM, D_RNN), row),
                  pl.BlockSpec((_TM, D_MODEL), row),
                  pl.BlockSpec((_TM, D_MODEL), row),
                  pl.BlockSpec((_TM, D_MODEL), _ctx_map),
                  pl.BlockSpec((_TM, D_MODEL), _lat_map),
                  pl.BlockSpec((1, 1, N_MOD * D_MODEL), lambda i: (_mod_row(i, _TM), 0, 0)),
                  const((1, D_MODEL)),
                  const((D_ATT, D_MODEL)), const((D_RNN, D_MODEL)), const((D_MODEL, D_MODEL)),
                  const((N_EXPERTS, D_MODEL)), const((N_EXPERTS, D_MODEL)), const((N_EXPERTS, 1)),
                  const((D_MODEL, D_EXPERT)), const((D_MODEL, D_EXPERT)), const((D_EXPERT, D_MODEL))],
        out_specs=[pl.BlockSpec((_TM, D_MODEL), row),
                   pl.BlockSpec((_TM * _FOLD, LANES), row),
                   pl.BlockSpec((N_EXPERTS, _TM), lambda i: (0, i))],
        out_shape=[jax.ShapeDtypeStruct((N_TOK, D_MODEL), F32),
                   jax.ShapeDtypeStruct((N_TOK * _FOLD, LANES), F32),
                   jax.ShapeDtypeStruct((N_EXPERTS, N_TOK), F32)],
        compiler_params=_params(("parallel",)),
        name="merge",
    )(oac, oal, ob, ga, gb, xp, xs, mod3, g2, wpa, wpb, wo, wrh, wrl, br, wgs, wus, wds)


_FOLD = D_MODEL // LANES
_NH = N_TOK // 2
_R = 512
_LIST = -(-_NH // _R) * _R
_SLOTS = 256
_ID_BITS = 13
_DISP_BITS = 12
_VALID_BIT = _ID_BITS + _DISP_BITS + 1


def _chunk_table(cnt):
    shape = (N_EXPERTS, _SLOTS)
    nch = jnp.broadcast_to((cnt + (_R - 1)) // _R, shape)
    erow = lax.broadcasted_iota(jnp.int32, shape, 0)
    q = lax.broadcasted_iota(jnp.int32, shape, 1)
    cum = nch
    s = 1
    while s < N_EXPERTS:
        cum = cum + jnp.where(erow >= s, pltpu.roll(cum, s, 0), 0)
        s *= 2
    total = cum[N_EXPERTS - 1:N_EXPERTS, :]
    used = q[:1] < total
    expert = jnp.minimum(jnp.sum((cum <= q).astype(jnp.int32), axis=0, keepdims=True), N_EXPERTS - 1)
    first = cum - nch
    start = jnp.max(jnp.where(first <= q, first, 0), axis=0, keepdims=True)
    chunk = jnp.where(used, q[:1] - start, 0)
    return jnp.concatenate([expert, chunk, used.astype(jnp.int32), total,
                            jnp.zeros((4, _SLOTS), jnp.int32)], axis=0)


def _dispatch_kernel(g_ref, src_ref, dst_ref, gl_ref, tbl_ref, cnt_ref):
    n = g_ref.shape[1]
    ne = 16
    lane = lax.broadcasted_iota(jnp.int32, (ne, n), 1)

    def expert_group(i, carry):
        rows = pl.ds(pl.multiple_of(i * ne, ne), ne)
        g = g_ref[rows, :]
        sel = g > 0.0
        m = sel.astype(jnp.int32)
        csum = m
        s = 1
        while s < n:
            csum = csum + jnp.where(lane >= s, pltpu.roll(csum, s, 1), 0)
            s *= 2
        cnt_ref[rows, :] = jnp.broadcast_to(csum[:, n - 1:n], (ne, LANES))
        disp = lane - (csum - m)
        packed = jnp.where(sel, (1 << _VALID_BIT) | (disp << _ID_BITS) | lane, 0)
        g = jnp.where(sel, g, 0.0)
        for b in range(_DISP_BITS):
            shift = n - (1 << b)
            inc = pltpu.roll(packed, shift, 1)
            inc_moves = ((inc >> (_ID_BITS + b)) & 1) == 1
            own_moves = ((packed >> (_ID_BITS + b)) & 1) == 1
            packed = jnp.where(inc_moves, inc, jnp.where(own_moves, 0, packed))
            g = jnp.where(inc_moves, pltpu.roll(g, shift, 1), jnp.where(own_moves, 0.0, g))
        valid = (packed >> _VALID_BIT) == 1
        tok = packed & ((1 << _ID_BITS) - 1)
        src_ref[0, rows, :n] = jnp.where(valid, tok, 0) * _FOLD
        dst_ref[0, rows, :n] = jnp.where(valid, tok, n) * _FOLD
        gl_ref[0, rows, :n] = g
        if _LIST > n:
            pad = (ne, _LIST - n)
            src_ref[0, rows, n:] = jnp.zeros(pad, jnp.int32)
            dst_ref[0, rows, n:] = jnp.full(pad, n * _FOLD, jnp.int32)
            gl_ref[0, rows, n:] = jnp.zeros(pad, F32)
        return carry

    lax.fori_loop(0, N_EXPERTS // ne, expert_group, 0)
    tbl_ref[0] = _chunk_table(cnt_ref[:, :1])


def _dispatch(gates_t):
    halves = N_TOK // _NH
    blk = pl.BlockSpec((1, N_EXPERTS, _LIST), lambda h: (h, 0, 0))
    lists = jax.ShapeDtypeStruct((halves, N_EXPERTS, _LIST), jnp.int32)
    return pl.pallas_call(
        _dispatch_kernel,
        grid=(halves,),
        in_specs=[pl.BlockSpec((N_EXPERTS, _NH), lambda h: (0, h))],
        out_specs=[blk, blk, blk, pl.BlockSpec((1, 8, _SLOTS), lambda h: (h, 0, 0))],
        out_shape=[lists, lists, jax.ShapeDtypeStruct((halves, N_EXPERTS, _LIST), F32),
                   jax.ShapeDtypeStruct((halves, 8, _SLOTS), jnp.int32)],
        scratch_shapes=[pltpu.VMEM((N_EXPERTS, LANES), jnp.int32)],
        compiler_params=_params(("parallel",)),
        name="dispatch",
    )(gates_t)


_GRP = 8
_MXU = 256
_KT = D_MODEL // _MXU
_UNF = 64
_TBL = 8 * _SLOTS
assert _NH * TOP_K // _R + N_EXPERTS + 4 <= _SLOTS


def _tbl(tbl_ref, h, row, k):
    return tbl_ref[h * _TBL + row * _SLOTS + jnp.clip(k, 0, _SLOTS - 1)]


def _moe_kernel(tbl_ref, src_hbm, dst_hbm, gate_hbm, hf_ref, wg_hbm, wu_hbm, wd_hbm, out_hbm,
                acc_ref, wf_refs, wb_refs, xg_ref, xa_ref, xb_ref, ya_ref, yb_ref,
                src_s, dst_s, gate_v, list_sem, w_sem, out_sem):
    h = pl.program_id(0)
    total = _tbl(tbl_ref, h, 3, 0)
    wgb_ref, wub_ref, wdb_ref = wb_refs

    for ref in (acc_ref, ya_ref, yb_ref):
        ref[...] = jnp.zeros_like(ref)

    @pl.when(h == 0)
    def _():
        for ref in (xa_ref, xb_ref, wgb_ref, wub_ref, wdb_ref):
            ref[...] = jnp.zeros_like(ref)

    def list_block(k):
        return (h * N_EXPERTS + _tbl(tbl_ref, h, 0, k)) * (_LIST // _R) + _tbl(tbl_ref, h, 1, k)

    def new_expert(k):
        return (k >= 0) & (_tbl(tbl_ref, h, 2, k) == 1) & (_tbl(tbl_ref, h, 1, k) == 0)

    def list_copies(t, p):
        return (pltpu.make_async_copy(src_hbm.at[list_block(t)], src_s.at[p], list_sem.at[p, 0]),
                pltpu.make_async_copy(dst_hbm.at[list_block(t - 2)], dst_s.at[p], list_sem.at[p, 1]),
                pltpu.make_async_copy(gate_hbm.at[list_block(t - 1)], gate_v.at[1 - p], list_sem.at[p, 2]))

    def weight_copies(t, p):
        e = _tbl(tbl_ref, h, 0, t - 1)
        return [pltpu.make_async_copy(w_hbm.at[e], w_f.at[p], w_sem.at[p, n])
                for n, (w_hbm, w_f) in enumerate(zip((wg_hbm, wu_hbm, wd_hbm), wf_refs))]

    def prefetch(t, p):
        for cp in list_copies(t, p):
            cp.start()

        @pl.when(new_expert(t - 1))
        def _():
            for cp in weight_copies(t, p):
                cp.start()

    def arrive_lists(t, p):
        for cp in list_copies(t, p):
            cp.wait()

    def arrive_weights(t, p, cast):
        @pl.when(new_expert(t - 1))
        def _():
            for cp in weight_copies(t, p):
                cp.wait()
            if cast:
                for w_f, w_b in zip(wf_refs, wb_refs):
                    w_b[...] = w_f[p].astype(BF16)

    def row(ref, r):
        return ref.at[pl.ds(pl.multiple_of(r, _FOLD), _FOLD), :]

    def stages(t, p, x_next, x_mm, y_mm, y_scatter):
        mm_used = (t >= 1) & (_tbl(tbl_ref, h, 2, t - 1) == 1)
        src_ref, dst_ref, gate_ref = src_s.at[p, 0], dst_s.at[p, 0], gate_v.at[1 - p]

        def gather_group(j0):
            for j in range(j0, j0 + _GRP):
                row(xg_ref, j * _FOLD)[...] = row(hf_ref, src_ref[j])[...]

        def scatter_group(j0):
            dsts = [row(acc_ref, dst_ref[j0 + u]) for u in range(_GRP)]
            news = [dsts[u][...] + row(y_scatter, (j0 + u) * _FOLD)[...] for u in range(_GRP)]
            for u in range(_GRP):
                dsts[u][...] = news[u]

        def unfold_block(r0):
            for k in range(_FOLD):
                x_next[r0:r0 + _UNF, k * LANES:(k + 1) * LANES] = xg_ref[
                    pl.ds(r0 * _FOLD + k, _UNF, stride=_FOLD), :].astype(BF16)

        moves = []
        for j0 in range(0, _R, _GRP):
            moves += [functools.partial(scatter_group, j0), functools.partial(gather_group, j0)]
            if (j0 + _GRP) % _UNF == 0:
                moves.append(functools.partial(unfold_block, j0 + _GRP - _UNF))
        n_pieces = 3 * _KT
        done = [0]

        def some_moves():
            done[0] += 1
            for f in moves[(done[0] - 1) * len(moves) // n_pieces:done[0] * len(moves) // n_pieces]:
                f()

        diag = (lax.broadcasted_iota(jnp.int32, (_R, _R), 0) == lax.broadcasted_iota(jnp.int32, (_R, _R), 1))
        gates = gate_ref[...] * mm_used.astype(F32)
        gcol = jnp.sum(jnp.where(diag, gates, 0.0), axis=1, keepdims=True)
        gt = up = None
        for kt in range(_KT):
            rows_k = slice(kt * _MXU, (kt + 1) * _MXU)
            xk = x_mm[:, rows_k]
            d = jnp.dot(xk, wgb_ref[rows_k, :], preferred_element_type=F32)
            gt = d if gt is None else gt + d
            some_moves()
            d = jnp.dot(xk, wub_ref[rows_k, :], preferred_element_type=F32)
            up = d if up is None else up + d
            some_moves()
        act = (_silu(gt) * up * gcol).astype(BF16)
        for nt in range(D_MODEL // _MXU):
            y = jnp.dot(act, wdb_ref[:, nt * _MXU:(nt + 1) * _MXU], preferred_element_type=F32)
            some_moves()
            for k in range(_MXU // LANES):
                y_mm[pl.ds(nt * _MXU // LANES + k, _R, stride=_FOLD), :] = y[:, k * LANES:(k + 1) * LANES]

    def step(t, p):
        arrive_lists(t, p)
        prefetch(t + 1, 1 - p)
        arrive_weights(t, p, cast=True)
        if p == 0:
            stages(t, p, xa_ref, xb_ref, yb_ref, ya_ref)
        else:
            stages(t, p, xb_ref, xa_ref, ya_ref, yb_ref)

    def two_steps(i, carry):
        step(2 * i, 0)
        step(2 * i + 1, 1)
        return carry

    pairs = (total + 3) // 2
    prefetch(0, 0)
    lax.fori_loop(0, pairs, two_steps, 0)
    arrive_lists(2 * pairs, 0)
    arrive_weights(2 * pairs, 0, cast=False)

    rows = _NH * _FOLD
    cp = pltpu.make_async_copy(acc_ref.at[pl.ds(0, rows)],
                               out_hbm.at[pl.ds(pl.multiple_of(h * rows, rows), rows)], out_sem)
    cp.start()
    cp.wait()


def _moe(tbl, src, dst, gl, hfold, wg, wu, wd):
    halves = N_TOK // _NH
    hbm = pl.BlockSpec(memory_space=pl.ANY)
    stage = pltpu.VMEM((_R * _FOLD, LANES), F32)
    packed = pltpu.VMEM((_R, D_MODEL), BF16)
    up_shape, down_shape = (D_MODEL, D_EXPERT), (D_EXPERT, D_MODEL)
    return pl.pallas_call(
        _moe_kernel,
        grid_spec=pltpu.PrefetchScalarGridSpec(
            num_scalar_prefetch=1,
            grid=(halves,),
            in_specs=[hbm, hbm, hbm,
                      pl.BlockSpec((_NH * _FOLD, LANES), lambda h, tbl_ref: (h, 0)),
                      hbm, hbm, hbm],
            out_specs=hbm,
            scratch_shapes=[pltpu.VMEM(((_NH + 1) * _FOLD, LANES), F32),
                            tuple(pltpu.VMEM((2,) + s, F32) for s in (up_shape, up_shape, down_shape)),
                            tuple(pltpu.VMEM(s, BF16) for s in (up_shape, up_shape, down_shape)),
                            stage, packed, packed, stage, stage,
                            pltpu.SMEM((2, 1, _R), jnp.int32), pltpu.SMEM((2, 1, _R), jnp.int32),
                            pltpu.VMEM((2, 1, _R), F32),
                            pltpu.SemaphoreType.DMA((2, 3)), pltpu.SemaphoreType.DMA((2, 3)),
                            pltpu.SemaphoreType.DMA]),
        out_shape=jax.ShapeDtypeStruct((N_TOK * _FOLD, LANES), F32),
        compiler_params=_params(("arbitrary",)),
        name="moe",
    )(tbl, src, dst, gl, hfold, wg, wu, wd)


def _final_kernel(x_ref, r_ref, mod_ref, fg_ref, yp_ref, ys_ref):
    i = pl.program_id(0)
    routed = jnp.concatenate([r_ref[pl.ds(k, _TM, stride=_FOLD), :] for k in range(_FOLD)], axis=1)
    m = mod_ref[0]
    x2 = x_ref[...] + m[:, 5 * D_MODEL:6 * D_MODEL] * routed
    y = x2 * lax.rsqrt(jnp.mean(x2 * x2, axis=-1, keepdims=True) + EPS) * fg_ref[...]

    @pl.when(i < _CTX_TILES)
    def _():
        yp_ref[...] = y

    @pl.when(i >= _CTX_TILES)
    def _():
        ys_ref[...] = y


def _final(x1s, routed, mod3, fg):
    return pl.pallas_call(
        _final_kernel,
        grid=(N_TOK // _TM,),
        in_specs=[pl.BlockSpec((_TM, D_MODEL), lambda i: (i, 0)),
                  pl.BlockSpec((_TM * _FOLD, LANES), lambda i: (i, 0)),
                  pl.BlockSpec((1, 1, N_MOD * D_MODEL), lambda i: (_mod_row(i, _TM), 0, 0)),
                  pl.BlockSpec((1, D_MODEL), lambda i: (0, 0))],
        out_specs=[pl.BlockSpec((_TM, D_MODEL), _ctx_map),
                   pl.BlockSpec((_TM, D_MODEL), _lat_map)],
        out_shape=[jax.ShapeDtypeStruct((N_CTX, D_MODEL), F32),
                   jax.ShapeDtypeStruct((N_LAT, D_MODEL), F32)],
        compiler_params=_params(("arbitrary",)),
        name="final",
    )(x1s, routed, mod3, fg)


def kernel(x_prompt, x_sample, cache_k, cache_v, state_h_fwd, state_h_bwd, c, c_ctx, norm1_g, norm2_g, final_g, w_ada, b_ada, w_in, rpb, conv_w, conv_b, rg_wa, rg_ba, rg_wx, rg_bx, rg_lam, w_pa, w_pb, w_out, w_router, b_router, w_gate_e, w_up_e, w_down_e, w_gate_s, w_up_s, w_down_s):
    l = 0
    xp = x_prompt.reshape(N_CTX, D_MODEL)
    xs = x_sample.reshape(N_LAT, D_MODEL)

    cvecs = jnp.concatenate([c_ctx[None, :], c], axis=0)[:, :, None]
    mod = _adaln(cvecs, w_ada[l], b_ada[l][None, :])
    mod3 = mod.reshape(8, 1, N_MOD * D_MODEL)

    q, kc, vc, kl, vl, xr, yg, ga, gb = _inproj(xp, xs, mod3, norm1_g[l][None, :], w_in[l].astype(BF16))

    oac = _attn_ctx(q, kc, vc)
    ck = cache_k[:, l].reshape(DEC_BATCH, PAST_LEN, D_ATT)
    cv = cache_v[:, l].reshape(DEC_BATCH, PAST_LEN, D_ATT)
    oal = _attn_lat(q, kl, vl, ck, cv, _bias_tables(rpb[l]))

    w4 = jnp.concatenate([_block_diag(rg_wa[l, 0]), _block_diag(rg_wx[l, 0]),
                          _block_diag(rg_wa[l, 1]), _block_diag(rg_wx[l, 1])], axis=1).astype(BF16)
    b4 = jnp.concatenate([rg_ba[l, 0], rg_bx[l, 0], rg_ba[l, 1], rg_bx[l, 1]])[None, :]
    ob, hlf, hlb = _rglru(xr, yg, conv_w[l], conv_b[l][None, :], w4, b4, rg_lam[l],
                          state_h_fwd[:, l], state_h_bwd[:, l])

    wr_t = w_router[l].T
    wr_hi = wr_t.astype(BF16)
    wr_lo = (wr_t - wr_hi.astype(F32)).astype(BF16)
    x1s, hfold, gates_t = _merge(oac, oal, ob, ga, gb, xp, xs, mod3, norm2_g[l][None, :],
                                 w_pa[l].astype(BF16), w_pb[l].astype(BF16), w_out[l].astype(BF16),
                                 wr_hi, wr_lo, b_router[l][:, None],
                                 w_gate_s[l].astype(BF16), w_up_s[l].astype(BF16), w_down_s[l].astype(BF16))

    src, dst, gl, tbl = _dispatch(gates_t)
    routed = _moe(tbl.reshape(-1), src.reshape(-1, 1, _R), dst.reshape(-1, 1, _R), gl.reshape(-1, 1, _R), hfold,
                  w_gate_e[l], w_up_e[l], w_down_e[l])
    yp, ys = _final(x1s, routed, mod3, final_g[None, :])

    return (yp.reshape(BATCH, SEQ, D_MODEL),
            ys.reshape(DEC_BATCH, DEC_SEQ, D_MODEL),
            kc.reshape(BATCH, 1, SEQ, N_HEADS_A, HEAD_DIM_A),
            vc.reshape(BATCH, 1, SEQ, N_HEADS_A, HEAD_DIM_A),
            hlf.reshape(BATCH, 1, D_RNN),
            hlb.reshape(BATCH, 1, D_RNN))
```

```python
import functools
import math

import jax
import jax.numpy as jnp
from jax import lax
from jax.experimental import pallas as pl
from jax.experimental.pallas import tpu as pltpu

F32 = jnp.float32
BF16 = jnp.bfloat16

D_MODEL = 1024
BATCH = 16
SEQ = 256
DEC_BATCH = 2
DEC_SEQ = 1024
PAST_LEN = 512
GRID_W = 64
N_HEADS_A = 8
HEAD_DIM_A = 64
D_ATT = N_HEADS_A * HEAD_DIM_A
KH = 8
KW = 16
D_RNN = 512
N_RG_BLOCKS = 8
CONV_W = 4
RG_C = 8.0
N_EXPERTS = 64
TOP_K = 8
N_GROUPS = 8
GROUP_SIZE = N_EXPERTS // N_GROUPS
TOPK_GROUP = 4
D_EXPERT = 256
ROUTED_SCALE = 2.5
N_MOD = 6
EPS = 1e-6
NEG = -1e30
D_IN = 3 * D_ATT + 2 * D_RNN + 2 * D_MODEL

N_CTX = BATCH * SEQ
N_LAT = DEC_BATCH * DEC_SEQ
N_TOK = N_CTX + N_LAT
GRID_ROWS = DEC_SEQ // GRID_W

LANES = 128
VMEM_LIMIT = 56 * 1024 * 1024


def _params(sem, vmem=VMEM_LIMIT):
    return pltpu.CompilerParams(dimension_semantics=sem, vmem_limit_bytes=vmem)


def _mod_row(i, tile_rows):
    n_ctx_tiles = N_CTX // tile_rows
    return jnp.where(i < n_ctx_tiles, 0, 1 + (i - n_ctx_tiles) // (DEC_SEQ // tile_rows))


def _norm_mod(x, g, shift, scale):
    y = x * lax.rsqrt(jnp.mean(x * x, axis=-1, keepdims=True) + EPS)
    return (y * g) * (1.0 + scale) + shift


def _sigmoid(x):
    return 0.5 * jnp.tanh(0.5 * x) + 0.5


def _silu(x):
    return x * _sigmoid(x)


def _adaln_kernel(c_ref, w_ref, b_ref, o_ref):
    w = w_ref[...]
    rows = []
    for r in range(3):
        s = _silu(c_ref[r])
        rows.append(jnp.sum(w * s, axis=0, keepdims=True))
    rows.append(jnp.zeros((5, w.shape[1]), F32))
    o_ref[...] = jnp.concatenate(rows, axis=0) + b_ref[...]


def _adaln(cvecs, w_ada, b_ada):
    tn = 2048
    n = N_MOD * D_MODEL
    return pl.pallas_call(
        _adaln_kernel,
        grid=(n // tn,),
        in_specs=[pl.BlockSpec((3, D_MODEL, 1), lambda j: (0, 0, 0)),
                  pl.BlockSpec((D_MODEL, tn), lambda j: (0, j)),
                  pl.BlockSpec((1, tn), lambda j: (0, j))],
        out_specs=pl.BlockSpec((8, tn), lambda j: (0, j)),
        out_shape=jax.ShapeDtypeStruct((8, n), F32),
        compiler_params=_params(("parallel",)),
        name="adaln",
    )(cvecs, w_ada, b_ada)


_TM = 512
_CTX_TILES = N_CTX // _TM
_LAT_TILES = N_LAT // _TM


def _ctx_map(i):
    return (jnp.minimum(i, _CTX_TILES - 1), 0)


def _lat_map(i):
    return (jnp.maximum(i - _CTX_TILES, 0), 0)


def _inproj_kernel(xp_ref, xs_ref, mod_ref, g_ref, w_ref,
                   q_ref, kc_ref, vc_ref, kl_ref, vl_ref, xr_ref, yg_ref, ga_ref, gb_ref):
    i = pl.program_id(0)
    is_ctx = i < _CTX_TILES
    x = jnp.where(is_ctx, xp_ref[...], xs_ref[...])
    m = mod_ref[0]
    h = _norm_mod(x, g_ref[...], m[:, 0:D_MODEL], m[:, D_MODEL:2 * D_MODEL]).astype(BF16)

    def proj(a, b):
        return jnp.dot(h, w_ref[:, a:b], preferred_element_type=F32)

    q_ref[...] = proj(0, D_ATT).astype(BF16)
    k = proj(D_ATT, 2 * D_ATT)
    v = proj(2 * D_ATT, 3 * D_ATT)

    @pl.when(is_ctx)
    def _():
        kc_ref[...] = k
        vc_ref[...] = v

    @pl.when(jnp.logical_not(is_ctx))
    def _():
        kl_ref[...] = k
        vl_ref[...] = v

    o = 3 * D_ATT
    xr_ref[...] = proj(o, o + D_RNN)
    yg_ref[...] = proj(o + D_RNN, o + 2 * D_RNN).astype(BF16)
    o += 2 * D_RNN
    ga_ref[...] = proj(o, o + D_MODEL).astype(BF16)
    gb_ref[...] = proj(o + D_MODEL, o + 2 * D_MODEL).astype(BF16)


def _inproj(xp, xs, mod3, g1, w_in):
    row = lambda i: (i, 0)
    return pl.pallas_call(
        _inproj_kernel,
        grid=(N_TOK // _TM,),
        in_specs=[pl.BlockSpec((_TM, D_MODEL), _ctx_map),
                  pl.BlockSpec((_TM, D_MODEL), _lat_map),
                  pl.BlockSpec((1, 1, N_MOD * D_MODEL), lambda i: (_mod_row(i, _TM), 0, 0)),
                  pl.BlockSpec((1, D_MODEL), lambda i: (0, 0)),
                  pl.BlockSpec((D_MODEL, D_IN), lambda i: (0, 0))],
        out_specs=[pl.BlockSpec((_TM, D_ATT), row),
                   pl.BlockSpec((_TM, D_ATT), _ctx_map),
                   pl.BlockSpec((_TM, D_ATT), _ctx_map),
                   pl.BlockSpec((_TM, D_ATT), _lat_map),
                   pl.BlockSpec((_TM, D_ATT), _lat_map),
                   pl.BlockSpec((_TM, D_RNN), row),
                   pl.BlockSpec((_TM, D_RNN), row),
                   pl.BlockSpec((_TM, D_MODEL), row),
                   pl.BlockSpec((_TM, D_MODEL), row)],
        out_shape=[jax.ShapeDtypeStruct((N_TOK, D_ATT), BF16),
                   jax.ShapeDtypeStruct((N_CTX, D_ATT), F32),
                   jax.ShapeDtypeStruct((N_CTX, D_ATT), F32),
                   jax.ShapeDtypeStruct((N_LAT, D_ATT), F32),
                   jax.ShapeDtypeStruct((N_LAT, D_ATT), F32),
                   jax.ShapeDtypeStruct((N_TOK, D_RNN), F32),
                   jax.ShapeDtypeStruct((N_TOK, D_RNN), BF16),
                   jax.ShapeDtypeStruct((N_TOK, D_MODEL), BF16),
                   jax.ShapeDtypeStruct((N_TOK, D_MODEL), BF16)],
        compiler_params=_params(("arbitrary",)),
        name="inproj",
    )(xp, xs, mod3, g1, w_in)


def _pair_attention(qp, segs):
    scale = HEAD_DIM_A ** -0.5
    assert math.log2(scale).is_integer()
    lane = lax.broadcasted_iota(jnp.int32, (1, LANES), 1)
    qs = qp * scale
    outs = []
    for a in range(2):
        sel = (lane >= HEAD_DIM_A) if a else (lane < HEAD_DIM_A)
        qm = jnp.where(sel, qs, jnp.zeros_like(qs))
        ss = []
        for kp, _, bias_fn in segs:
            s = lax.dot_general(qm, kp, (((1,), (1,)), ((), ())), preferred_element_type=F32)
            if bias_fn is not None:
                s = s + bias_fn(a)
            ss.append(s)
        m = functools.reduce(jnp.maximum, [jnp.max(s, axis=-1, keepdims=True) for s in ss])
        es = [jnp.exp(s - m) for s in ss]
        inv = 1.0 / functools.reduce(jnp.add, [jnp.sum(e, axis=-1, keepdims=True) for e in es])
        o = inv * functools.reduce(jnp.add, [
            jnp.dot(e.astype(BF16), vp, preferred_element_type=F32)
            for e, (_, vp, _) in zip(es, segs)])
        outs.append(o)
    return jnp.where(lane < HEAD_DIM_A, outs[0], outs[1])


def _attn_ctx_kernel(q_ref, k_ref, v_ref, o_ref):
    for j in range(D_ATT // LANES):
        c = slice(j * LANES, (j + 1) * LANES)
        segs = [(k_ref[:, c].astype(BF16), v_ref[:, c].astype(BF16), None)]
        o_ref[:, c] = _pair_attention(q_ref[:, c], segs).astype(BF16)


def _attn_ctx(q_all, kc, vc):
    blk = pl.BlockSpec((SEQ, D_ATT), lambda b: (b, 0))
    return pl.pallas_call(
        _attn_ctx_kernel,
        grid=(BATCH,),
        in_specs=[blk, blk, blk],
        out_specs=blk,
        out_shape=jax.ShapeDtypeStruct((N_CTX, D_ATT), BF16),
        compiler_params=_params(("parallel",)),
        name="attn_ctx",
    )(q_all, kc, vc)


_QROWS = 4
_QB = _QROWS * GRID_W
_KROWS = 12
_KB = _KROWS * GRID_W


def _build_bias_tables(rows_ref, t2_ref):
    wq = lax.broadcasted_iota(jnp.int32, (GRID_W, LANES), 0)
    wk = lax.broadcasted_iota(jnp.int32, (GRID_W, LANES), 1) & (GRID_W - 1)
    col_start = jnp.clip(wq - KW // 2, 0, GRID_W - KW)
    col_in = (wk >= col_start) & (wk < col_start + KW)
    for r in range(N_HEADS_A * 2 * KH):
        x = jnp.broadcast_to(rows_ref[r:r + 1, :], (GRID_W, LANES))
        x = pltpu.roll(x, LANES - (KW - 1), 1, stride=1, stride_axis=0)
        t2_ref[r // (2 * KH), r % (2 * KH)] = jnp.where(col_in, x, NEG)


def _attn_lat_kernel(q_ref, k_ref, v_ref, ck_ref, cv_ref, rows_ref, o_ref, t2_ref):
    i = pl.program_id(1)

    @pl.when((pl.program_id(0) == 0) & (i == 0))
    def _():
        _build_bias_tables(rows_ref, t2_ref)

    ks = jnp.where(i < 2, 0, GRID_ROWS - _KROWS)
    kstart = pl.multiple_of(ks * GRID_W, 256)
    lane = lax.broadcasted_iota(jnp.int32, (1, LANES), 1)

    def bias_for_head(h):
        rows = []
        for rq in range(_QROWS):
            r = _QROWS * i + rq
            rs = jnp.clip(r - KH // 2, 0, GRID_ROWS - KH)
            tiles = []
            for t in range(_KROWS // 2):
                kr = ks + 2 * t
                d = jnp.clip(kr - r + (KH - 1), -1, 2 * KH - 2)
                tile = t2_ref[h, d + 1]
                v0 = ((kr >= rs) & (kr < rs + KH)).astype(jnp.int32)
                v1 = ((kr + 1 >= rs) & (kr + 1 < rs + KH)).astype(jnp.int32)
                valid = jnp.where(lane < GRID_W, v0, v1) > 0
                tiles.append(jnp.where(valid, tile, NEG))
            rows.append(jnp.concatenate(tiles, axis=1))
        return jnp.concatenate(rows, axis=0)

    for j in range(D_ATT // LANES):
        c = slice(j * LANES, (j + 1) * LANES)
        k_loc = k_ref[pl.ds(kstart, _KB), c].astype(BF16)
        v_loc = v_ref[pl.ds(kstart, _KB), c].astype(BF16)
        segs = [(k_loc, v_loc, lambda a, j=j: bias_for_head(2 * j + a)),
                (ck_ref[:, c].astype(BF16), cv_ref[:, c].astype(BF16), None)]
        o_ref[:, c] = _pair_attention(q_ref[:, c], segs).astype(BF16)


def _attn_lat(q_all, kl, vl, ck, cv, t2):
    qblocks = DEC_SEQ // _QB
    return pl.pallas_call(
        _attn_lat_kernel,
        grid=(DEC_BATCH, qblocks),
        in_specs=[pl.BlockSpec((_QB, D_ATT), lambda b, i: (N_CTX // _QB + b * qblocks + i, 0)),
                  pl.BlockSpec((DEC_SEQ, D_ATT), lambda b, i: (b, 0)),
                  pl.BlockSpec((DEC_SEQ, D_ATT), lambda b, i: (b, 0)),
                  pl.BlockSpec((None, PAST_LEN, D_ATT), lambda b, i: (b, 0, 0)),
                  pl.BlockSpec((None, PAST_LEN, D_ATT), lambda b, i: (b, 0, 0)),
                  pl.BlockSpec((N_HEADS_A * 2 * KH, LANES), lambda b, i: (0, 0))],
        out_specs=pl.BlockSpec((_QB, D_ATT), lambda b, i: (b * qblocks + i, 0)),
        out_shape=jax.ShapeDtypeStruct((N_LAT, D_ATT), BF16),
        scratch_shapes=[pltpu.VMEM((N_HEADS_A, 2 * KH, GRID_W, LANES), F32)],
        compiler_params=_params(("arbitrary", "arbitrary")),
        name="attn_lat",
    )(q_all, kl, vl, ck, cv, t2)


def _bias_tables(rpb):
    half = jnp.pad(rpb, ((0, 0), (0, 0), (0, GRID_W - rpb.shape[-1])))
    neg = jnp.full((N_HEADS_A, 1, GRID_W), NEG, F32)
    left = jnp.concatenate([neg, half], axis=1)
    right = jnp.concatenate([half, neg], axis=1)
    return jnp.concatenate([left, right], axis=-1).reshape(N_HEADS_A * 2 * KH, LANES)


_RG_ROWS = 2048
_RG_CHUNK = 256
_RG_SLABS = D_RNN // LANES
_LOG2E = 1.4426950408889634
_RG_CTX_STEPS = N_CTX // _RG_ROWS


def _rg_block(seq_len, h0f, h0b, xr_ref, yg_ref, cw_ref, cb_ref, w4_ref, b4_ref, lam_ref,
              ob_ref, xc_ref, af_ref, uf_ref, ab_ref, ub_ref, hf_ref, hb_ref, cf_ref, cr_ref):
    T, R = _RG_CHUNK, _RG_ROWS
    nb = R // T
    segs = seq_len // T
    linked = segs > 1
    xr = xr_ref[...]
    t_idx = lax.broadcasted_iota(jnp.int32, (R, 1), 0) & (seq_len - 1)
    cw = cw_ref[...]
    xc = jnp.zeros((R, D_RNN), F32) + cb_ref[...]
    for j in range(CONV_W):
        off = j - 2
        if off == 0:
            tap = xr
        else:
            tap = pltpu.roll(xr, shift=(-off) % R, axis=0)
            tap = jnp.where((t_idx + off >= 0) & (t_idx + off < seq_len), tap, 0.0)
        xc = xc + cw[j:j + 1, :] * tap
    xc_ref[...] = xc

    lam = lam_ref[...]
    z = -lam
    sp = jnp.maximum(z, 0.0) + jnp.log1p(jnp.exp(-jnp.abs(z)))
    decay_k = (-0.5 * RG_C * _LOG2E) * sp

    def put(ref, c, val):
        for k in range(_RG_SLABS):
            ref[k, pl.ds(c, T, stride=nb), :] = val[:, k * LANES:(k + 1) * LANES]

    def get(ref, c):
        return jnp.concatenate([ref[k, pl.ds(c, T, stride=nb), :] for k in range(_RG_SLABS)], axis=1)

    def gate_chunk(c, carry):
        x = xc_ref[pl.ds(pl.multiple_of(c * T, T), T), :]
        g4 = jnp.dot(x.astype(BF16), w4_ref[...], preferred_element_type=F32) + b4_ref[...]
        hx = 0.5 * x
        for d, (a_ref, u_ref) in enumerate(((af_ref, uf_ref), (ab_ref, ub_ref))):
            tr = jnp.tanh(0.5 * g4[:, (2 * d) * D_RNN:(2 * d + 1) * D_RNN])
            ti = jnp.tanh(0.5 * g4[:, (2 * d + 1) * D_RNN:(2 * d + 2) * D_RNN])
            k = decay_k[d:d + 1, :]
            a = jnp.exp2(k * tr + k)
            put(a_ref, c, a)
            put(u_ref, c, jnp.sqrt(jnp.maximum(1.0 - a * a, 0.0)) * (hx * ti + hx))
        return carry

    lax.fori_loop(0, nb, gate_chunk, 0)

    def step(t, carry):
        hf, hb, pf, pb = carry
        rf = pl.ds(pl.multiple_of(t * nb, nb), nb)
        rb = pl.ds(pl.multiple_of((T - 1 - t) * nb, nb), nb)
        nf, nbk, npf, npb = [], [], [], []
        for k in range(_RG_SLABS):
            a_f, a_b = af_ref[k, rf, :], ab_ref[k, rb, :]
            f = a_f * hf[k] + uf_ref[k, rf, :]
            b = a_b * hb[k] + ub_ref[k, rb, :]
            hf_ref[k, rf, :] = f
            hb_ref[k, rb, :] = b
            nf.append(f)
            nbk.append(b)
            if linked:
                npf.append(a_f * pf[k])
                npb.append(a_b * pb[k])
                af_ref[k, rf, :] = npf[-1]
                ab_ref[k, rb, :] = npb[-1]
        return tuple(nf), tuple(nbk), tuple(npf), tuple(npb)

    zeros = (jnp.zeros((nb, LANES), F32),) * _RG_SLABS
    ones = (jnp.ones((nb, LANES), F32),) * _RG_SLABS if linked else ()
    hf, hb, pf, pb = lax.fori_loop(0, T, step, (zeros, zeros, ones, ones))

    if linked:
        seg_row = lax.broadcasted_iota(jnp.int32, (nb, LANES), 0)
        for k in range(_RG_SLABS):
            lanes = slice(k * LANES, (k + 1) * LANES)
            enter_f = jnp.zeros((nb, LANES), F32)
            enter_b = jnp.zeros((nb, LANES), F32)
            for s0 in range(0, nb, segs):
                c = h0f[s0 // segs:s0 // segs + 1, lanes]
                for q in range(s0, s0 + segs):
                    enter_f = jnp.where(seg_row == q, c, enter_f)
                    c = hf[k][q:q + 1, :] + pf[k][q:q + 1, :] * c
                c = h0b[s0 // segs:s0 // segs + 1, lanes]
                for q in reversed(range(s0, s0 + segs)):
                    enter_b = jnp.where(seg_row == q, c, enter_b)
                    c = hb[k][q:q + 1, :] + pb[k][q:q + 1, :] * c
            cf_ref[:, lanes] = enter_f
            cr_ref[:, lanes] = enter_b

    def out_chunk(c, carry):
        rows = pl.ds(pl.multiple_of(c * T, T), T)
        hsum = get(hf_ref, c) + get(hb_ref, c)
        if linked:
            hsum = hsum + get(af_ref, c) * cf_ref[pl.ds(c, 1), :] + get(ab_ref, c) * cr_ref[pl.ds(c, 1), :]
        ob_ref[rows, :] = (hsum * jax.nn.gelu(yg_ref[rows, :].astype(F32))).astype(BF16)
        return carry

    lax.fori_loop(0, nb, out_chunk, 0)
    return hf, hb


def _rglru_kernel(xr_ref, yg_ref, cw_ref, cb_ref, w4_ref, b4_ref, lam_ref, sf_ref, sb_ref,
                  ob_ref, hlf_ref, hlb_ref, *scratch):
    i = pl.program_id(0)
    shared = (xr_ref, yg_ref, cw_ref, cb_ref, w4_ref, b4_ref, lam_ref, ob_ref) + scratch

    @pl.when(i < _RG_CTX_STEPS)
    def _():
        hf, hb = _rg_block(SEQ, None, None, *shared)
        hlf_ref[...] = jnp.concatenate(hf, axis=1)
        hlb_ref[...] = jnp.concatenate(hb, axis=1)

    @pl.when(i >= _RG_CTX_STEPS)
    def _():
        _rg_block(DEC_SEQ, sf_ref[...], sb_ref[...], *shared)


def _rglru(xr, yg, conv_w, conv_b, w4, b4, lam, sf, sb):
    steps = N_TOK // _RG_ROWS
    nb_ctx = _RG_ROWS // SEQ
    const = lambda shape: pl.BlockSpec(shape, lambda i: (0,) * len(shape))
    hl_map = lambda i: (jnp.minimum(i, _RG_CTX_STEPS - 1), 0)
    scan_buf = pltpu.VMEM((_RG_SLABS, _RG_ROWS, LANES), F32)
    enter_buf = pltpu.VMEM((_RG_ROWS // _RG_CHUNK, D_RNN), F32)
    return pl.pallas_call(
        _rglru_kernel,
        grid=(steps,),
        in_specs=[pl.BlockSpec((_RG_ROWS, D_RNN), lambda i: (i, 0)),
                  pl.BlockSpec((_RG_ROWS, D_RNN), lambda i: (i, 0)),
                  const((CONV_W, D_RNN)), const((1, D_RNN)),
                  const((D_RNN, 4 * D_RNN)), const((1, 4 * D_RNN)), const((2, D_RNN)),
                  const((DEC_BATCH, D_RNN)), const((DEC_BATCH, D_RNN))],
        out_specs=[pl.BlockSpec((_RG_ROWS, D_RNN), lambda i: (i, 0)),
                   pl.BlockSpec((nb_ctx, D_RNN), hl_map),
                   pl.BlockSpec((nb_ctx, D_RNN), hl_map)],
        out_shape=[jax.ShapeDtypeStruct((N_TOK, D_RNN), BF16),
                   jax.ShapeDtypeStruct((BATCH, D_RNN), F32),
                   jax.ShapeDtypeStruct((BATCH, D_RNN), F32)],
        scratch_shapes=[pltpu.VMEM((_RG_ROWS, D_RNN), F32)] + [scan_buf] * 6 + [enter_buf] * 2,
        compiler_params=_params(("arbitrary",)),
        name="rglru",
    )(xr, yg, conv_w, conv_b, w4, b4, lam, sf, sb)


def _block_diag(w):
    n, d, _ = w.shape
    eye = jnp.eye(n, dtype=w.dtype)
    return jnp.einsum('nij,nm->nimj', w, eye).reshape(n * d, n * d)


def _nt_dot(a, b):
    return lax.dot_general(a, b, (((1,), (1,)), ((), ())), preferred_element_type=F32)


def _route(s, sb):
    n = s.shape[1]
    low = -3.0e38
    sb3 = sb.reshape(N_GROUPS, GROUP_SIZE, n)
    iw = lax.broadcasted_iota(jnp.int32, sb3.shape, 1)
    m1 = jnp.max(sb3, axis=1, keepdims=True)
    i1 = jnp.min(jnp.where(sb3 == m1, iw, GROUP_SIZE), axis=1, keepdims=True)
    m2 = jnp.max(jnp.where(iw == i1, low, sb3), axis=1, keepdims=True)
    gscore = m1 + m2
    ig = lax.broadcasted_iota(jnp.int32, gscore.shape, 0)
    beaten = jnp.zeros(gscore.shape, jnp.int32)
    for g in range(N_GROUPS):
        row = gscore[g:g + 1]
        beaten = beaten + ((row > gscore) | ((row == gscore) & (g < ig))).astype(jnp.int32)
    gsel = beaten < TOPK_GROUP
    work = jnp.where(gsel, sb3, NEG).reshape(N_EXPERTS, n)
    ie = lax.broadcasted_iota(jnp.int32, work.shape, 0)
    chosen = jnp.zeros(work.shape, jnp.bool_)
    for _ in range(TOP_K):
        m = jnp.max(work, axis=0, keepdims=True)
        pick = ie == jnp.min(jnp.where(work == m, ie, N_EXPERTS), axis=0, keepdims=True)
        chosen = chosen | pick
        work = jnp.where(pick, low, work)
    wsel = jnp.where(chosen, s, 0.0)
    return wsel / jnp.sum(wsel, axis=0, keepdims=True) * ROUTED_SCALE


def _merge_kernel(oac_ref, oal_ref, ob_ref, ga_ref, gb_ref, xp_ref, xs_ref, mod_ref, g_ref,
                  wpa_ref, wpb_ref, wo_ref, wrh_ref, wrl_ref, br_ref, wgs_ref, wus_ref, wds_ref,
                  x1s_ref, hf_ref, gates_ref):
    i = pl.program_id(0)
    is_ctx = i < _CTX_TILES
    oa = jnp.where(is_ctx, oac_ref[...], oal_ref[...])
    x = jnp.where(is_ctx, xp_ref[...], xs_ref[...])
    m = mod_ref[0]
    sl = lambda k: m[:, k * D_MODEL:(k + 1) * D_MODEL]
    mix = (_sigmoid(ga_ref[...].astype(F32)) * jnp.dot(oa, wpa_ref[...], preferred_element_type=F32)
           + _sigmoid(gb_ref[...].astype(F32)) * jnp.dot(ob_ref[...], wpb_ref[...], preferred_element_type=F32))
    o = jnp.dot(mix.astype(BF16), wo_ref[...], preferred_element_type=F32)
    x1 = x + sl(2) * o
    h2 = _norm_mod(x1, g_ref[...], sl(3), sl(4))
    for c in range(_FOLD):
        hf_ref[pl.ds(c, _TM, stride=_FOLD), :] = h2[:, c * LANES:(c + 1) * LANES]
    h_hi = h2.astype(BF16)
    a = _silu(jnp.dot(h_hi, wgs_ref[...], preferred_element_type=F32)) * jnp.dot(
        h_hi, wus_ref[...], preferred_element_type=F32)
    x1s_ref[...] = x1 + sl(5) * jnp.dot(a.astype(BF16), wds_ref[...], preferred_element_type=F32)
    h_lo = (h2 - h_hi.astype(F32)).astype(BF16)
    logits = _nt_dot(wrh_ref[...], h_hi) + _nt_dot(wrh_ref[...], h_lo) + _nt_dot(wrl_ref[...], h_hi)
    s = jax.nn.sigmoid(logits)
    gates_ref[...] = _route(s, s + br_ref[...])


def _merge(oac, oal, ob, ga, gb, xp, xs, mod3, g2, wpa, wpb, wo, wrh, wrl, br, wgs, wus, wds):
    row = lambda i: (i, 0)
    const = lambda shape: pl.BlockSpec(shape, lambda i: (0,) * len(shape))
    return pl.pallas_call(
        _merge_kernel,
        grid=(N_TOK // _TM,),
        in_specs=[pl.BlockSpec((_TM, D_ATT), _ctx_map),
                  pl.BlockSpec((_TM, D_ATT), _lat_map),
                  pl.BlockSpec((_TM, D_RNN), row),
                  pl.BlockSpec((_TM, D_MODEL), row),
                  pl.BlockSpec((_TM, D_MODEL), row),
                  pl.BlockSpec((_TM, D_MODEL), _ctx_map),
                  pl.BlockSpec((_TM, D_MODEL), _lat_map),
                  pl.BlockSpec((1, 1, N_MOD * D_MODEL), lambda i: (_mod_row(i, _TM), 0, 0)),
                  const((1, D_MODEL)),
                  const((D_ATT, D_MODEL)), const((D_RNN, D_MODEL)), const((D_MODEL, D_MODEL)),
                  const((N_EXPERTS, D_MODEL)), const((N_EXPERTS, D_MODEL)), const((N_EXPERTS, 1)),
                  const((D_MODEL, D_EXPERT)), const((D_MODEL, D_EXPERT)), const((D_EXPERT, D_MODEL))],
        out_specs=[pl.BlockSpec((_TM, D_MODEL), row),
                   pl.BlockSpec((_TM * _FOLD, LANES), row),
                   pl.BlockSpec((N_EXPERTS, _TM), lambda i: (0, i))],
        out_shape=[jax.ShapeDtypeStruct((N_TOK, D_MODEL), F32),
                   jax.ShapeDtypeStruct((N_TOK * _FOLD, LANES), F32),
                   jax.ShapeDtypeStruct((N_EXPERTS, N_TOK), F32)],
        compiler_params=_params(("parallel",)),
        name="merge",
    )(oac, oal, ob, ga, gb, xp, xs, mod3, g2, wpa, wpb, wo, wrh, wrl, br, wgs, wus, wds)


_FOLD = D_MODEL // LANES
_NH = N_TOK // 2
_R = 512
_LIST = -(-_NH // _R) * _R
_SLOTS = 256
_ID_BITS = 13
_DISP_BITS = 12
_VALID_BIT = _ID_BITS + _DISP_BITS + 1


def _chunk_table(cnt):
    shape = (N_EXPERTS, _SLOTS)
    nch = jnp.broadcast_to((cnt + (_R - 1)) // _R, shape)
    erow = lax.broadcasted_iota(jnp.int32, shape, 0)
    q = lax.broadcasted_iota(jnp.int32, shape, 1)
    cum = nch
    s = 1
    while s < N_EXPERTS:
        cum = cum + jnp.where(erow >= s, pltpu.roll(cum, s, 0), 0)
        s *= 2
    total = cum[N_EXPERTS - 1:N_EXPERTS, :]
    used = q[:1] < total
    expert = jnp.minimum(jnp.sum((cum <= q).astype(jnp.int32), axis=0, keepdims=True), N_EXPERTS - 1)
    first = cum - nch
    start = jnp.max(jnp.where(first <= q, first, 0), axis=0, keepdims=True)
    chunk = jnp.where(used, q[:1] - start, 0)
    return jnp.concatenate([expert, chunk, used.astype(jnp.int32), total,
                            jnp.zeros((4, _SLOTS), jnp.int32)], axis=0)


def _dispatch_kernel(g_ref, src_ref, dst_ref, gl_ref, tbl_ref, cnt_ref):
    n = g_ref.shape[1]
    ne = 16
    lane = lax.broadcasted_iota(jnp.int32, (ne, n), 1)

    def expert_group(i, carry):
        rows = pl.ds(pl.multiple_of(i * ne, ne), ne)
        g = g_ref[rows, :]
        sel = g > 0.0
        m = sel.astype(jnp.int32)
        csum = m
        s = 1
        while s < n:
            csum = csum + jnp.where(lane >= s, pltpu.roll(csum, s, 1), 0)
            s *= 2
        cnt_ref[rows, :] = jnp.broadcast_to(csum[:, n - 1:n], (ne, LANES))
        disp = lane - (csum - m)
        packed = jnp.where(sel, (1 << _VALID_BIT) | (disp << _ID_BITS) | lane, 0)
        g = jnp.where(sel, g, 0.0)
        for b in range(_DISP_BITS):
            shift = n - (1 << b)
            inc = pltpu.roll(packed, shift, 1)
            inc_moves = ((inc >> (_ID_BITS + b)) & 1) == 1
            own_moves = ((packed >> (_ID_BITS + b)) & 1) == 1
            packed = jnp.where(inc_moves, inc, jnp.where(own_moves, 0, packed))
            g = jnp.where(inc_moves, pltpu.roll(g, shift, 1), jnp.where(own_moves, 0.0, g))
        valid = (packed >> _VALID_BIT) == 1
        tok = packed & ((1 << _ID_BITS) - 1)
        src_ref[0, rows, :n] = jnp.where(valid, tok, 0) * _FOLD
        dst_ref[0, rows, :n] = jnp.where(valid, tok, n) * _FOLD
        gl_ref[0, rows, :n] = g
        if _LIST > n:
            pad = (ne, _LIST - n)
            src_ref[0, rows, n:] = jnp.zeros(pad, jnp.int32)
            dst_ref[0, rows, n:] = jnp.full(pad, n * _FOLD, jnp.int32)
            gl_ref[0, rows, n:] = jnp.zeros(pad, F32)
        return carry

    lax.fori_loop(0, N_EXPERTS // ne, expert_group, 0)
    tbl_ref[0] = _chunk_table(cnt_ref[:, :1])


def _dispatch(gates_t):
    halves = N_TOK // _NH
    blk = pl.BlockSpec((1, N_EXPERTS, _LIST), lambda h: (h, 0, 0))
    lists = jax.ShapeDtypeStruct((halves, N_EXPERTS, _LIST), jnp.int32)
    return pl.pallas_call(
        _dispatch_kernel,
        grid=(halves,),
        in_specs=[pl.BlockSpec((N_EXPERTS, _NH), lambda h: (0, h))],
        out_specs=[blk, blk, blk, pl.BlockSpec((1, 8, _SLOTS), lambda h: (h, 0, 0))],
        out_shape=[lists, lists, jax.ShapeDtypeStruct((halves, N_EXPERTS, _LIST), F32),
                   jax.ShapeDtypeStruct((halves, 8, _SLOTS), jnp.int32)],
        scratch_shapes=[pltpu.VMEM((N_EXPERTS, LANES), jnp.int32)],
        compiler_params=_params(("parallel",)),
        name="dispatch",
    )(gates_t)


_GRP = 8
_MXU = 256
_KT = D_MODEL // _MXU
_UNF = 64
_OUT_ROWS = 128
_TBL = 8 * _SLOTS
assert _NH * TOP_K // _R + N_EXPERTS + 4 <= _SLOTS


def _tbl(tbl_ref, h, row, k):
    return tbl_ref[h * _TBL + row * _SLOTS + jnp.clip(k, 0, _SLOTS - 1)]


def _moe_kernel(tbl_ref, src_hbm, dst_hbm, gate_hbm, hf_ref, wg_hbm, wu_hbm, wd_hbm, x_hbm, mod_ref, fg_ref,
                yp_hbm, ys_hbm,
                acc_ref, wf_refs, wb_refs, xg_ref, xa_ref, xb_ref, ya_ref, yb_ref,
                src_s, dst_s, gate_v, xt_ref, yt_ref, list_sem, w_sem, in_sem, out_sem):
    h = pl.program_id(0)
    total = _tbl(tbl_ref, h, 3, 0)
    wgb_ref, wub_ref, wdb_ref = wb_refs

    for ref in (acc_ref, ya_ref, yb_ref):
        ref[...] = jnp.zeros_like(ref)

    @pl.when(h == 0)
    def _():
        for ref in (xa_ref, xb_ref, wgb_ref, wub_ref, wdb_ref):
            ref[...] = jnp.zeros_like(ref)

    def list_block(k):
        return (h * N_EXPERTS + _tbl(tbl_ref, h, 0, k)) * (_LIST // _R) + _tbl(tbl_ref, h, 1, k)

    def new_expert(k):
        return (k >= 0) & (_tbl(tbl_ref, h, 2, k) == 1) & (_tbl(tbl_ref, h, 1, k) == 0)

    def list_copies(t, p):
        return (pltpu.make_async_copy(src_hbm.at[list_block(t)], src_s.at[p], list_sem.at[p, 0]),
                pltpu.make_async_copy(dst_hbm.at[list_block(t - 2)], dst_s.at[p], list_sem.at[p, 1]),
                pltpu.make_async_copy(gate_hbm.at[list_block(t - 1)], gate_v.at[1 - p], list_sem.at[p, 2]))

    def weight_copies(t, p):
        e = _tbl(tbl_ref, h, 0, t - 1)
        return [pltpu.make_async_copy(w_hbm.at[e], w_f.at[p], w_sem.at[p, n])
                for n, (w_hbm, w_f) in enumerate(zip((wg_hbm, wu_hbm, wd_hbm), wf_refs))]

    def prefetch(t, p):
        for cp in list_copies(t, p):
            cp.start()

        @pl.when(new_expert(t - 1))
        def _():
            for cp in weight_copies(t, p):
                cp.start()

    def arrive_lists(t, p):
        for cp in list_copies(t, p):
            cp.wait()

    def arrive_weights(t, p, cast):
        @pl.when(new_expert(t - 1))
        def _():
            for cp in weight_copies(t, p):
                cp.wait()
            if cast:
                for w_f, w_b in zip(wf_refs, wb_refs):
                    w_b[...] = w_f[p].astype(BF16)

    def row(ref, r):
        return ref.at[pl.ds(pl.multiple_of(r, _FOLD), _FOLD), :]

    def stages(t, p, x_next, x_mm, y_mm, y_scatter):
        mm_used = (t >= 1) & (_tbl(tbl_ref, h, 2, t - 1) == 1)
        src_ref, dst_ref, gate_ref = src_s.at[p, 0], dst_s.at[p, 0], gate_v.at[1 - p]

        def gather_group(j0):
            for j in range(j0, j0 + _GRP):
                row(xg_ref, j * _FOLD)[...] = row(hf_ref, src_ref[j])[...]

        def scatter_group(j0):
            dsts = [row(acc_ref, dst_ref[j0 + u]) for u in range(_GRP)]
            news = [dsts[u][...] + row(y_scatter, (j0 + u) * _FOLD)[...] for u in range(_GRP)]
            for u in range(_GRP):
                dsts[u][...] = news[u]

        def unfold_block(r0):
            for k in range(_FOLD):
                x_next[r0:r0 + _UNF, k * LANES:(k + 1) * LANES] = xg_ref[
                    pl.ds(r0 * _FOLD + k, _UNF, stride=_FOLD), :].astype(BF16)

        moves = []
        for j0 in range(0, _R, _GRP):
            moves += [functools.partial(scatter_group, j0), functools.partial(gather_group, j0)]
            if (j0 + _GRP) % _UNF == 0:
                moves.append(functools.partial(unfold_block, j0 + _GRP - _UNF))
        n_pieces = 3 * _KT
        done = [0]

        def some_moves():
            done[0] += 1
            for f in moves[(done[0] - 1) * len(moves) // n_pieces:done[0] * len(moves) // n_pieces]:
                f()

        diag = (lax.broadcasted_iota(jnp.int32, (_R, _R), 0) == lax.broadcasted_iota(jnp.int32, (_R, _R), 1))
        gates = gate_ref[...] * mm_used.astype(F32)
        gcol = jnp.sum(jnp.where(diag, gates, 0.0), axis=1, keepdims=True)
        gt = up = None
        for kt in range(_KT):
            rows_k = slice(kt * _MXU, (kt + 1) * _MXU)
            xk = x_mm[:, rows_k]
            d = jnp.dot(xk, wgb_ref[rows_k, :], preferred_element_type=F32)
            gt = d if gt is None else gt + d
            some_moves()
            d = jnp.dot(xk, wub_ref[rows_k, :], preferred_element_type=F32)
            up = d if up is None else up + d
            some_moves()
        act = (_silu(gt) * up * gcol).astype(BF16)
        for nt in range(D_MODEL // _MXU):
            y = jnp.dot(act, wdb_ref[:, nt * _MXU:(nt + 1) * _MXU], preferred_element_type=F32)
            some_moves()
            for k in range(_MXU // LANES):
                y_mm[pl.ds(nt * _MXU // LANES + k, _R, stride=_FOLD), :] = y[:, k * LANES:(k + 1) * LANES]

    def step(t, p):
        arrive_lists(t, p)
        prefetch(t + 1, 1 - p)
        arrive_weights(t, p, cast=True)
        if p == 0:
            stages(t, p, xa_ref, xb_ref, yb_ref, ya_ref)
        else:
            stages(t, p, xb_ref, xa_ref, ya_ref, yb_ref)

    def two_steps(i, carry):
        step(2 * i, 0)
        step(2 * i + 1, 1)
        return carry

    pairs = (total + 3) // 2
    prefetch(0, 0)
    lax.fori_loop(0, pairs, two_steps, 0)
    arrive_lists(2 * pairs, 0)
    arrive_weights(2 * pairs, 0, cast=False)

    fg = fg_ref[...]

    def emit(half, tile0, n_tiles, dst_hbm, dst_row0, mod_row):
        gate = mod_ref[mod_row:mod_row + 1, 5 * D_MODEL:6 * D_MODEL]

        def in_copy(i):
            tok0 = half * _NH + (tile0 + i) * _OUT_ROWS
            return pltpu.make_async_copy(x_hbm.at[pl.ds(tok0, _OUT_ROWS)], xt_ref.at[i % 2], in_sem.at[i % 2])

        def out_copy(i):
            return pltpu.make_async_copy(yt_ref.at[i % 2], dst_hbm.at[pl.ds(dst_row0 + i * _OUT_ROWS, _OUT_ROWS)],
                                         out_sem.at[i % 2])

        in_copy(0).start()
        for i in range(n_tiles):
            if i + 1 < n_tiles:
                in_copy(i + 1).start()
            in_copy(i).wait()
            base = (tile0 + i) * _OUT_ROWS * _FOLD
            routed = jnp.concatenate(
                [acc_ref[pl.ds(base + k, _OUT_ROWS, stride=_FOLD), :] for k in range(_FOLD)], axis=1)
            x2 = xt_ref[i % 2] + gate * routed
            y = x2 * lax.rsqrt(jnp.mean(x2 * x2, axis=-1, keepdims=True) + EPS) * fg
            if i >= 2:
                out_copy(i - 2).wait()
            yt_ref[i % 2] = y
            out_copy(i).start()
        for i in range(max(n_tiles - 2, 0), n_tiles):
            out_copy(i).wait()

    per_half = _NH // _OUT_ROWS
    for half in range(N_TOK // _NH):
        @pl.when(h == half)
        def _(half=half):
            t = 0
            while t < per_half:
                tok = half * _NH + t * _OUT_ROWS
                if tok < N_CTX:
                    end_tok, dst, row0, mrow = N_CTX, yp_hbm, tok, 0
                else:
                    b = (tok - N_CTX) // DEC_SEQ
                    end_tok, dst, row0, mrow = N_CTX + (b + 1) * DEC_SEQ, ys_hbm, tok - N_CTX, 1 + b
                n = min(per_half - t, (end_tok - tok) // _OUT_ROWS)
                emit(half, t, n, dst, row0, mrow)
                t += n


def _moe(tbl, src, dst, gl, hfold, wg, wu, wd, x1s, mod, fg):
    halves = N_TOK // _NH
    hbm = pl.BlockSpec(memory_space=pl.ANY)
    const = lambda shape: pl.BlockSpec(shape, lambda h, tbl_ref: (0,) * len(shape))
    stage = pltpu.VMEM((_R * _FOLD, LANES), F32)
    packed = pltpu.VMEM((_R, D_MODEL), BF16)
    tile = pltpu.VMEM((2, _OUT_ROWS, D_MODEL), F32)
    up_shape, down_shape = (D_MODEL, D_EXPERT), (D_EXPERT, D_MODEL)
    return pl.pallas_call(
        _moe_kernel,
        grid_spec=pltpu.PrefetchScalarGridSpec(
            num_scalar_prefetch=1,
            grid=(halves,),
            in_specs=[hbm, hbm, hbm,
                      pl.BlockSpec((_NH * _FOLD, LANES), lambda h, tbl_ref: (h, 0)),
                      hbm, hbm, hbm,
                      hbm, const((8, N_MOD * D_MODEL)), const((1, D_MODEL))],
            out_specs=[hbm, hbm],
            scratch_shapes=[pltpu.VMEM(((_NH + 1) * _FOLD, LANES), F32),
                            tuple(pltpu.VMEM((2,) + s, F32) for s in (up_shape, up_shape, down_shape)),
                            tuple(pltpu.VMEM(s, BF16) for s in (up_shape, up_shape, down_shape)),
                            stage, packed, packed, stage, stage,
                            pltpu.SMEM((2, 1, _R), jnp.int32), pltpu.SMEM((2, 1, _R), jnp.int32),
                            pltpu.VMEM((2, 1, _R), F32), tile, tile,
                            pltpu.SemaphoreType.DMA((2, 3)), pltpu.SemaphoreType.DMA((2, 3)),
                            pltpu.SemaphoreType.DMA((2,)), pltpu.SemaphoreType.DMA((2,))]),
        out_shape=[jax.ShapeDtypeStruct((N_CTX, D_MODEL), F32),
                   jax.ShapeDtypeStruct((N_LAT, D_MODEL), F32)],
        compiler_params=_params(("arbitrary",)),
        name="moe",
    )(tbl, src, dst, gl, hfold, wg, wu, wd, x1s, mod, fg)


def kernel(x_prompt, x_sample, cache_k, cache_v, state_h_fwd, state_h_bwd, c, c_ctx, norm1_g, norm2_g, final_g, w_ada, b_ada, w_in, rpb, conv_w, conv_b, rg_wa, rg_ba, rg_wx, rg_bx, rg_lam, w_pa, w_pb, w_out, w_router, b_router, w_gate_e, w_up_e, w_down_e, w_gate_s, w_up_s, w_down_s):
    l = 0
    xp = x_prompt.reshape(N_CTX, D_MODEL)
    xs = x_sample.reshape(N_LAT, D_MODEL)

    cvecs = jnp.concatenate([c_ctx[None, :], c], axis=0)[:, :, None]
    mod = _adaln(cvecs, w_ada[l], b_ada[l][None, :])
    mod3 = mod.reshape(8, 1, N_MOD * D_MODEL)

    q, kc, vc, kl, vl, xr, yg, ga, gb = _inproj(xp, xs, mod3, norm1_g[l][None, :], w_in[l].astype(BF16))

    oac = _attn_ctx(q, kc, vc)
    ck = cache_k[:, l].reshape(DEC_BATCH, PAST_LEN, D_ATT)
    cv = cache_v[:, l].reshape(DEC_BATCH, PAST_LEN, D_ATT)
    oal = _attn_lat(q, kl, vl, ck, cv, _bias_tables(rpb[l]))

    w4 = jnp.concatenate([_block_diag(rg_wa[l, 0]), _block_diag(rg_wx[l, 0]),
                          _block_diag(rg_wa[l, 1]), _block_diag(rg_wx[l, 1])], axis=1).astype(BF16)
    b4 = jnp.concatenate([rg_ba[l, 0], rg_bx[l, 0], rg_ba[l, 1], rg_bx[l, 1]])[None, :]
    ob, hlf, hlb = _rglru(xr, yg, conv_w[l], conv_b[l][None, :], w4, b4, rg_lam[l],
                          state_h_fwd[:, l], state_h_bwd[:, l])

    wr_t = w_router[l].T
    wr_hi = wr_t.astype(BF16)
    wr_lo = (wr_t - wr_hi.astype(F32)).astype(BF16)
    x1s, hfold, gates_t = _merge(oac, oal, ob, ga, gb, xp, xs, mod3, norm2_g[l][None, :],
                                 w_pa[l].astype(BF16), w_pb[l].astype(BF16), w_out[l].astype(BF16),
                                 wr_hi, wr_lo, b_router[l][:, None],
                                 w_gate_s[l].astype(BF16), w_up_s[l].astype(BF16), w_down_s[l].astype(BF16))

    src, dst, gl, tbl = _dispatch(gates_t)
    yp, ys = _moe(tbl.reshape(-1), src.reshape(-1, 1, _R), dst.reshape(-1, 1, _R), gl.reshape(-1, 1, _R), hfold,
                  w_gate_e[l], w_up_e[l], w_down_e[l], x1s, mod, final_g[None, :])

    return (yp.reshape(BATCH, SEQ, D_MODEL),
            ys.reshape(DEC_BATCH, DEC_SEQ, D_MODEL),
            kc.reshape(BATCH, 1, SEQ, N_HEADS_A, HEAD_DIM_A),
            vc.reshape(BATCH, 1, SEQ, N_HEADS_A, HEAD_DIM_A),
            hlf.reshape(BATCH, 1, D_RNN),
            hlb.reshape(BATCH, 1, D_RNN))
```

```python
import functools
import math

import jax
import jax.numpy as jnp
from jax import lax
from jax.experimental import pallas as pl
from jax.experimental.pallas import tpu as pltpu

F32 = jnp.float32
BF16 = jnp.bfloat16

D_MODEL = 1024
BATCH = 16
SEQ = 256
DEC_BATCH = 2
DEC_SEQ = 1024
PAST_LEN = 512
GRID_W = 64
N_HEADS_A = 8
HEAD_DIM_A = 64
D_ATT = N_HEADS_A * HEAD_DIM_A
KH = 8
KW = 16
D_RNN = 512
N_RG_BLOCKS = 8
CONV_W = 4
RG_C = 8.0
N_EXPERTS = 64
TOP_K = 8
N_GROUPS = 8
GROUP_SIZE = N_EXPERTS // N_GROUPS
TOPK_GROUP = 4
D_EXPERT = 256
ROUTED_SCALE = 2.5
N_MOD = 6
EPS = 1e-6
NEG = -1e30
D_IN = 3 * D_ATT + 2 * D_RNN + 2 * D_MODEL

N_CTX = BATCH * SEQ
N_LAT = DEC_BATCH * DEC_SEQ
N_TOK = N_CTX + N_LAT
GRID_ROWS = DEC_SEQ // GRID_W

LANES = 128
VMEM_LIMIT = 56 * 1024 * 1024
MOE_VMEM_LIMIT = 60 * 1024 * 1024


def _params(sem, vmem=VMEM_LIMIT):
    return pltpu.CompilerParams(dimension_semantics=sem, vmem_limit_bytes=vmem)


def _mod_row(i, tile_rows):
    n_ctx_tiles = N_CTX // tile_rows
    return jnp.where(i < n_ctx_tiles, 0, 1 + (i - n_ctx_tiles) // (DEC_SEQ // tile_rows))


def _norm_mod(x, g, shift, scale):
    y = x * lax.rsqrt(jnp.mean(x * x, axis=-1, keepdims=True) + EPS)
    return (y * g) * (1.0 + scale) + shift


def _sigmoid(x):
    return 0.5 * jnp.tanh(0.5 * x) + 0.5


def _silu(x):
    return x * _sigmoid(x)


def _adaln_kernel(c_ref, w_ref, b_ref, o_ref):
    w = w_ref[...]
    rows = []
    for r in range(3):
        s = _silu(c_ref[r])
        rows.append(jnp.sum(w * s, axis=0, keepdims=True))
    rows.append(jnp.zeros((5, w.shape[1]), F32))
    o_ref[...] = jnp.concatenate(rows, axis=0) + b_ref[...]


def _adaln(cvecs, w_ada, b_ada):
    tn = 2048
    n = N_MOD * D_MODEL
    return pl.pallas_call(
        _adaln_kernel,
        grid=(n // tn,),
        in_specs=[pl.BlockSpec((3, D_MODEL, 1), lambda j: (0, 0, 0)),
                  pl.BlockSpec((D_MODEL, tn), lambda j: (0, j)),
                  pl.BlockSpec((1, tn), lambda j: (0, j))],
        out_specs=pl.BlockSpec((8, tn), lambda j: (0, j)),
        out_shape=jax.ShapeDtypeStruct((8, n), F32),
        compiler_params=_params(("parallel",)),
        name="adaln",
    )(cvecs, w_ada, b_ada)


_TM = 512
_CTX_TILES = N_CTX // _TM
_LAT_TILES = N_LAT // _TM


def _ctx_map(i):
    return (jnp.minimum(i, _CTX_TILES - 1), 0)


def _lat_map(i):
    return (jnp.maximum(i - _CTX_TILES, 0), 0)


def _inproj_kernel(xp_ref, xs_ref, mod_ref, g_ref, w_ref,
                   q_ref, kc_ref, vc_ref, kl_ref, vl_ref, xr_ref, yg_ref, ga_ref, gb_ref):
    i = pl.program_id(0)
    is_ctx = i < _CTX_TILES
    x = jnp.where(is_ctx, xp_ref[...], xs_ref[...])
    m = mod_ref[0]
    h = _norm_mod(x, g_ref[...], m[:, 0:D_MODEL], m[:, D_MODEL:2 * D_MODEL]).astype(BF16)

    def proj(a, b):
        return jnp.dot(h, w_ref[:, a:b], preferred_element_type=F32)

    q_ref[...] = proj(0, D_ATT).astype(BF16)
    k = proj(D_ATT, 2 * D_ATT)
    v = proj(2 * D_ATT, 3 * D_ATT)

    @pl.when(is_ctx)
    def _():
        kc_ref[...] = k
        vc_ref[...] = v

    @pl.when(jnp.logical_not(is_ctx))
    def _():
        kl_ref[...] = k
        vl_ref[...] = v

    o = 3 * D_ATT
    xr_ref[...] = proj(o, o + D_RNN)
    yg_ref[...] = proj(o + D_RNN, o + 2 * D_RNN).astype(BF16)
    o += 2 * D_RNN
    ga_ref[...] = proj(o, o + D_MODEL).astype(BF16)
    gb_ref[...] = proj(o + D_MODEL, o + 2 * D_MODEL).astype(BF16)


def _inproj(xp, xs, mod3, g1, w_in):
    row = lambda i: (i, 0)
    return pl.pallas_call(
        _inproj_kernel,
        grid=(N_TOK // _TM,),
        in_specs=[pl.BlockSpec((_TM, D_MODEL), _ctx_map),
                  pl.BlockSpec((_TM, D_MODEL), _lat_map),
                  pl.BlockSpec((1, 1, N_MOD * D_MODEL), lambda i: (_mod_row(i, _TM), 0, 0)),
                  pl.BlockSpec((1, D_MODEL), lambda i: (0, 0)),
                  pl.BlockSpec((D_MODEL, D_IN), lambda i: (0, 0))],
        out_specs=[pl.BlockSpec((_TM, D_ATT), row),
                   pl.BlockSpec((_TM, D_ATT), _ctx_map),
                   pl.BlockSpec((_TM, D_ATT), _ctx_map),
                   pl.BlockSpec((_TM, D_ATT), _lat_map),
                   pl.BlockSpec((_TM, D_ATT), _lat_map),
                   pl.BlockSpec((_TM, D_RNN), row),
                   pl.BlockSpec((_TM, D_RNN), row),
                   pl.BlockSpec((_TM, D_MODEL), row),
                   pl.BlockSpec((_TM, D_MODEL), row)],
        out_shape=[jax.ShapeDtypeStruct((N_TOK, D_ATT), BF16),
                   jax.ShapeDtypeStruct((N_CTX, D_ATT), F32),
                   jax.ShapeDtypeStruct((N_CTX, D_ATT), F32),
                   jax.ShapeDtypeStruct((N_LAT, D_ATT), F32),
                   jax.ShapeDtypeStruct((N_LAT, D_ATT), F32),
                   jax.ShapeDtypeStruct((N_TOK, D_RNN), F32),
                   jax.ShapeDtypeStruct((N_TOK, D_RNN), BF16),
                   jax.ShapeDtypeStruct((N_TOK, D_MODEL), BF16),
                   jax.ShapeDtypeStruct((N_TOK, D_MODEL), BF16)],
        compiler_params=_params(("arbitrary",)),
        name="inproj",
    )(xp, xs, mod3, g1, w_in)


def _pair_attention(qp, segs):
    scale = HEAD_DIM_A ** -0.5
    assert math.log2(scale).is_integer()
    lane = lax.broadcasted_iota(jnp.int32, (1, LANES), 1)
    qs = qp * scale
    outs = []
    for a in range(2):
        sel = (lane >= HEAD_DIM_A) if a else (lane < HEAD_DIM_A)
        qm = jnp.where(sel, qs, jnp.zeros_like(qs))
        ss = []
        for kp, _, bias_fn in segs:
            s = lax.dot_general(qm, kp, (((1,), (1,)), ((), ())), preferred_element_type=F32)
            if bias_fn is not None:
                s = s + bias_fn(a)
            ss.append(s)
        m = functools.reduce(jnp.maximum, [jnp.max(s, axis=-1, keepdims=True) for s in ss])
        es = [jnp.exp(s - m) for s in ss]
        inv = 1.0 / functools.reduce(jnp.add, [jnp.sum(e, axis=-1, keepdims=True) for e in es])
        o = inv * functools.reduce(jnp.add, [
            jnp.dot(e.astype(BF16), vp, preferred_element_type=F32)
            for e, (_, vp, _) in zip(es, segs)])
        outs.append(o)
    return jnp.where(lane < HEAD_DIM_A, outs[0], outs[1])


def _attn_ctx_kernel(q_ref, k_ref, v_ref, o_ref):
    for j in range(D_ATT // LANES):
        c = slice(j * LANES, (j + 1) * LANES)
        segs = [(k_ref[:, c].astype(BF16), v_ref[:, c].astype(BF16), None)]
        o_ref[:, c] = _pair_attention(q_ref[:, c], segs).astype(BF16)


def _attn_ctx(q_all, kc, vc):
    blk = pl.BlockSpec((SEQ, D_ATT), lambda b: (b, 0))
    return pl.pallas_call(
        _attn_ctx_kernel,
        grid=(BATCH,),
        in_specs=[blk, blk, blk],
        out_specs=blk,
        out_shape=jax.ShapeDtypeStruct((N_CTX, D_ATT), BF16),
        compiler_params=_params(("parallel",)),
        name="attn_ctx",
    )(q_all, kc, vc)


_QROWS = 4
_QB = _QROWS * GRID_W
_KROWS = 12
_KB = _KROWS * GRID_W


def _build_bias_tables(rows_ref, t2_ref):
    wq = lax.broadcasted_iota(jnp.int32, (GRID_W, LANES), 0)
    wk = lax.broadcasted_iota(jnp.int32, (GRID_W, LANES), 1) & (GRID_W - 1)
    col_start = jnp.clip(wq - KW // 2, 0, GRID_W - KW)
    col_in = (wk >= col_start) & (wk < col_start + KW)
    for r in range(N_HEADS_A * 2 * KH):
        x = jnp.broadcast_to(rows_ref[r:r + 1, :], (GRID_W, LANES))
        x = pltpu.roll(x, LANES - (KW - 1), 1, stride=1, stride_axis=0)
        t2_ref[r // (2 * KH), r % (2 * KH)] = jnp.where(col_in, x, NEG)


def _attn_lat_kernel(q_ref, k_ref, v_ref, ck_ref, cv_ref, rows_ref, o_ref, t2_ref):
    i = pl.program_id(1)

    @pl.when((pl.program_id(0) == 0) & (i == 0))
    def _():
        _build_bias_tables(rows_ref, t2_ref)

    ks = jnp.where(i < 2, 0, GRID_ROWS - _KROWS)
    kstart = pl.multiple_of(ks * GRID_W, 256)
    lane = lax.broadcasted_iota(jnp.int32, (1, LANES), 1)

    def bias_for_head(h):
        rows = []
        for rq in range(_QROWS):
            r = _QROWS * i + rq
            rs = jnp.clip(r - KH // 2, 0, GRID_ROWS - KH)
            tiles = []
            for t in range(_KROWS // 2):
                kr = ks + 2 * t
                d = jnp.clip(kr - r + (KH - 1), -1, 2 * KH - 2)
                tile = t2_ref[h, d + 1]
                v0 = ((kr >= rs) & (kr < rs + KH)).astype(jnp.int32)
                v1 = ((kr + 1 >= rs) & (kr + 1 < rs + KH)).astype(jnp.int32)
                valid = jnp.where(lane < GRID_W, v0, v1) > 0
                tiles.append(jnp.where(valid, tile, NEG))
            rows.append(jnp.concatenate(tiles, axis=1))
        return jnp.concatenate(rows, axis=0)

    for j in range(D_ATT // LANES):
        c = slice(j * LANES, (j + 1) * LANES)
        k_loc = k_ref[pl.ds(kstart, _KB), c].astype(BF16)
        v_loc = v_ref[pl.ds(kstart, _KB), c].astype(BF16)
        segs = [(k_loc, v_loc, lambda a, j=j: bias_for_head(2 * j + a)),
                (ck_ref[:, c].astype(BF16), cv_ref[:, c].astype(BF16), None)]
        o_ref[:, c] = _pair_attention(q_ref[:, c], segs).astype(BF16)


def _attn_lat(q_all, kl, vl, ck, cv, t2):
    qblocks = DEC_SEQ // _QB
    return pl.pallas_call(
        _attn_lat_kernel,
        grid=(DEC_BATCH, qblocks),
        in_specs=[pl.BlockSpec((_QB, D_ATT), lambda b, i: (N_CTX // _QB + b * qblocks + i, 0)),
                  pl.BlockSpec((DEC_SEQ, D_ATT), lambda b, i: (b, 0)),
                  pl.BlockSpec((DEC_SEQ, D_ATT), lambda b, i: (b, 0)),
                  pl.BlockSpec((None, PAST_LEN, D_ATT), lambda b, i: (b, 0, 0)),
                  pl.BlockSpec((None, PAST_LEN, D_ATT), lambda b, i: (b, 0, 0)),
                  pl.BlockSpec((N_HEADS_A * 2 * KH, LANES), lambda b, i: (0, 0))],
        out_specs=pl.BlockSpec((_QB, D_ATT), lambda b, i: (b * qblocks + i, 0)),
        out_shape=jax.ShapeDtypeStruct((N_LAT, D_ATT), BF16),
        scratch_shapes=[pltpu.VMEM((N_HEADS_A, 2 * KH, GRID_W, LANES), F32)],
        compiler_params=_params(("arbitrary", "arbitrary")),
        name="attn_lat",
    )(q_all, kl, vl, ck, cv, t2)


def _bias_tables(rpb):
    half = jnp.pad(rpb, ((0, 0), (0, 0), (0, GRID_W - rpb.shape[-1])))
    neg = jnp.full((N_HEADS_A, 1, GRID_W), NEG, F32)
    left = jnp.concatenate([neg, half], axis=1)
    right = jnp.concatenate([half, neg], axis=1)
    return jnp.concatenate([left, right], axis=-1).reshape(N_HEADS_A * 2 * KH, LANES)


_RG_ROWS = 2048
_RG_CHUNK = 256
_RG_SLABS = D_RNN // LANES
_LOG2E = 1.4426950408889634
_RG_CTX_STEPS = N_CTX // _RG_ROWS


def _rg_block(seq_len, h0f, h0b, xr_ref, yg_ref, cw_ref, cb_ref, w4_ref, b4_ref, lam_ref,
              ob_ref, xc_ref, af_ref, uf_ref, ab_ref, ub_ref, hf_ref, hb_ref, cf_ref, cr_ref):
    T, R = _RG_CHUNK, _RG_ROWS
    nb = R // T
    segs = seq_len // T
    linked = segs > 1
    xr = xr_ref[...]
    t_idx = lax.broadcasted_iota(jnp.int32, (R, 1), 0) & (seq_len - 1)
    cw = cw_ref[...]
    xc = jnp.zeros((R, D_RNN), F32) + cb_ref[...]
    for j in range(CONV_W):
        off = j - 2
        if off == 0:
            tap = xr
        else:
            tap = pltpu.roll(xr, shift=(-off) % R, axis=0)
            tap = jnp.where((t_idx + off >= 0) & (t_idx + off < seq_len), tap, 0.0)
        xc = xc + cw[j:j + 1, :] * tap
    xc_ref[...] = xc

    lam = lam_ref[...]
    z = -lam
    sp = jnp.maximum(z, 0.0) + jnp.log1p(jnp.exp(-jnp.abs(z)))
    decay_k = (-0.5 * RG_C * _LOG2E) * sp

    def put(ref, c, val):
        for k in range(_RG_SLABS):
            ref[k, pl.ds(c, T, stride=nb), :] = val[:, k * LANES:(k + 1) * LANES]

    def get(ref, c):
        return jnp.concatenate([ref[k, pl.ds(c, T, stride=nb), :] for k in range(_RG_SLABS)], axis=1)

    def gate_chunk(c, carry):
        x = xc_ref[pl.ds(pl.multiple_of(c * T, T), T), :]
        g4 = jnp.dot(x.astype(BF16), w4_ref[...], preferred_element_type=F32) + b4_ref[...]
        hx = 0.5 * x
        for d, (a_ref, u_ref) in enumerate(((af_ref, uf_ref), (ab_ref, ub_ref))):
            tr = jnp.tanh(0.5 * g4[:, (2 * d) * D_RNN:(2 * d + 1) * D_RNN])
            ti = jnp.tanh(0.5 * g4[:, (2 * d + 1) * D_RNN:(2 * d + 2) * D_RNN])
            k = decay_k[d:d + 1, :]
            a = jnp.exp2(k * tr + k)
            put(a_ref, c, a)
            put(u_ref, c, jnp.sqrt(jnp.maximum(1.0 - a * a, 0.0)) * (hx * ti + hx))
        return carry

    lax.fori_loop(0, nb, gate_chunk, 0)

    def step(t, carry):
        hf, hb, pf, pb = carry
        rf = pl.ds(pl.multiple_of(t * nb, nb), nb)
        rb = pl.ds(pl.multiple_of((T - 1 - t) * nb, nb), nb)
        nf, nbk, npf, npb = [], [], [], []
        for k in range(_RG_SLABS):
            a_f, a_b = af_ref[k, rf, :], ab_ref[k, rb, :]
            f = a_f * hf[k] + uf_ref[k, rf, :]
            b = a_b * hb[k] + ub_ref[k, rb, :]
            hf_ref[k, rf, :] = f
            hb_ref[k, rb, :] = b
            nf.append(f)
            nbk.append(b)
            if linked:
                npf.append(a_f * pf[k])
                npb.append(a_b * pb[k])
                af_ref[k, rf, :] = npf[-1]
                ab_ref[k, rb, :] = npb[-1]
        return tuple(nf), tuple(nbk), tuple(npf), tuple(npb)

    zeros = (jnp.zeros((nb, LANES), F32),) * _RG_SLABS
    ones = (jnp.ones((nb, LANES), F32),) * _RG_SLABS if linked else ()
    hf, hb, pf, pb = lax.fori_loop(0, T, step, (zeros, zeros, ones, ones))

    if linked:
        seg_row = lax.broadcasted_iota(jnp.int32, (nb, LANES), 0)
        for k in range(_RG_SLABS):
            lanes = slice(k * LANES, (k + 1) * LANES)
            enter_f = jnp.zeros((nb, LANES), F32)
            enter_b = jnp.zeros((nb, LANES), F32)
            for s0 in range(0, nb, segs):
                c = h0f[s0 // segs:s0 // segs + 1, lanes]
                for q in range(s0, s0 + segs):
                    enter_f = jnp.where(seg_row == q, c, enter_f)
                    c = hf[k][q:q + 1, :] + pf[k][q:q + 1, :] * c
                c = h0b[s0 // segs:s0 // segs + 1, lanes]
                for q in reversed(range(s0, s0 + segs)):
                    enter_b = jnp.where(seg_row == q, c, enter_b)
                    c = hb[k][q:q + 1, :] + pb[k][q:q + 1, :] * c
            cf_ref[:, lanes] = enter_f
            cr_ref[:, lanes] = enter_b

    def out_chunk(c, carry):
        rows = pl.ds(pl.multiple_of(c * T, T), T)
        hsum = get(hf_ref, c) + get(hb_ref, c)
        if linked:
            hsum = hsum + get(af_ref, c) * cf_ref[pl.ds(c, 1), :] + get(ab_ref, c) * cr_ref[pl.ds(c, 1), :]
        ob_ref[rows, :] = (hsum * jax.nn.gelu(yg_ref[rows, :].astype(F32))).astype(BF16)
        return carry

    lax.fori_loop(0, nb, out_chunk, 0)
    return hf, hb


def _rglru_kernel(xr_ref, yg_ref, cw_ref, cb_ref, w4_ref, b4_ref, lam_ref, sf_ref, sb_ref,
                  ob_ref, hlf_ref, hlb_ref, *scratch):
    i = pl.program_id(0)
    shared = (xr_ref, yg_ref, cw_ref, cb_ref, w4_ref, b4_ref, lam_ref, ob_ref) + scratch

    @pl.when(i < _RG_CTX_STEPS)
    def _():
        hf, hb = _rg_block(SEQ, None, None, *shared)
        hlf_ref[...] = jnp.concatenate(hf, axis=1)
        hlb_ref[...] = jnp.concatenate(hb, axis=1)

    @pl.when(i >= _RG_CTX_STEPS)
    def _():
        _rg_block(DEC_SEQ, sf_ref[...], sb_ref[...], *shared)


def _rglru(xr, yg, conv_w, conv_b, w4, b4, lam, sf, sb):
    steps = N_TOK // _RG_ROWS
    nb_ctx = _RG_ROWS // SEQ
    const = lambda shape: pl.BlockSpec(shape, lambda i: (0,) * len(shape))
    hl_map = lambda i: (jnp.minimum(i, _RG_CTX_STEPS - 1), 0)
    scan_buf = pltpu.VMEM((_RG_SLABS, _RG_ROWS, LANES), F32)
    enter_buf = pltpu.VMEM((_RG_ROWS // _RG_CHUNK, D_RNN), F32)
    return pl.pallas_call(
        _rglru_kernel,
        grid=(steps,),
        in_specs=[pl.BlockSpec((_RG_ROWS, D_RNN), lambda i: (i, 0)),
                  pl.BlockSpec((_RG_ROWS, D_RNN), lambda i: (i, 0)),
                  const((CONV_W, D_RNN)), const((1, D_RNN)),
                  const((D_RNN, 4 * D_RNN)), const((1, 4 * D_RNN)), const((2, D_RNN)),
                  const((DEC_BATCH, D_RNN)), const((DEC_BATCH, D_RNN))],
        out_specs=[pl.BlockSpec((_RG_ROWS, D_RNN), lambda i: (i, 0)),
                   pl.BlockSpec((nb_ctx, D_RNN), hl_map),
                   pl.BlockSpec((nb_ctx, D_RNN), hl_map)],
        out_shape=[jax.ShapeDtypeStruct((N_TOK, D_RNN), BF16),
                   jax.ShapeDtypeStruct((BATCH, D_RNN), F32),
                   jax.ShapeDtypeStruct((BATCH, D_RNN), F32)],
        scratch_shapes=[pltpu.VMEM((_RG_ROWS, D_RNN), F32)] + [scan_buf] * 6 + [enter_buf] * 2,
        compiler_params=_params(("arbitrary",)),
        name="rglru",
    )(xr, yg, conv_w, conv_b, w4, b4, lam, sf, sb)


def _block_diag(w):
    n, d, _ = w.shape
    eye = jnp.eye(n, dtype=w.dtype)
    return jnp.einsum('nij,nm->nimj', w, eye).reshape(n * d, n * d)


def _nt_dot(a, b):
    return lax.dot_general(a, b, (((1,), (1,)), ((), ())), preferred_element_type=F32)


def _route(s, sb):
    n = s.shape[1]
    low = -3.0e38
    sb3 = sb.reshape(N_GROUPS, GROUP_SIZE, n)
    iw = lax.broadcasted_iota(jnp.int32, sb3.shape, 1)
    m1 = jnp.max(sb3, axis=1, keepdims=True)
    i1 = jnp.min(jnp.where(sb3 == m1, iw, GROUP_SIZE), axis=1, keepdims=True)
    m2 = jnp.max(jnp.where(iw == i1, low, sb3), axis=1, keepdims=True)
    gscore = m1 + m2
    ig = lax.broadcasted_iota(jnp.int32, gscore.shape, 0)
    beaten = jnp.zeros(gscore.shape, jnp.int32)
    for g in range(N_GROUPS):
        row = gscore[g:g + 1]
        beaten = beaten + ((row > gscore) | ((row == gscore) & (g < ig))).astype(jnp.int32)
    gsel = beaten < TOPK_GROUP
    work = jnp.where(gsel, sb3, NEG).reshape(N_EXPERTS, n)
    ie = lax.broadcasted_iota(jnp.int32, work.shape, 0)
    chosen = jnp.zeros(work.shape, jnp.bool_)
    for _ in range(TOP_K):
        m = jnp.max(work, axis=0, keepdims=True)
        pick = ie == jnp.min(jnp.where(work == m, ie, N_EXPERTS), axis=0, keepdims=True)
        chosen = chosen | pick
        work = jnp.where(pick, low, work)
    wsel = jnp.where(chosen, s, 0.0)
    return wsel / jnp.sum(wsel, axis=0, keepdims=True) * ROUTED_SCALE


def _merge_kernel(oac_ref, oal_ref, ob_ref, ga_ref, gb_ref, xp_ref, xs_ref, mod_ref, g_ref,
                  wpa_ref, wpb_ref, wo_ref, wrh_ref, wrl_ref, br_ref, wgs_ref, wus_ref, wds_ref,
                  x1s_ref, hf_ref, gates_ref):
    i = pl.program_id(0)
    is_ctx = i < _CTX_TILES
    oa = jnp.where(is_ctx, oac_ref[...], oal_ref[...])
    x = jnp.where(is_ctx, xp_ref[...], xs_ref[...])
    m = mod_ref[0]
    sl = lambda k: m[:, k * D_MODEL:(k + 1) * D_MODEL]
    mix = (_sigmoid(ga_ref[...].astype(F32)) * jnp.dot(oa, wpa_ref[...], preferred_element_type=F32)
           + _sigmoid(gb_ref[...].astype(F32)) * jnp.dot(ob_ref[...], wpb_ref[...], preferred_element_type=F32))
    o = jnp.dot(mix.astype(BF16), wo_ref[...], preferred_element_type=F32)
    x1 = x + sl(2) * o
    h2 = _norm_mod(x1, g_ref[...], sl(3), sl(4))
    for c in range(_FOLD):
        hf_ref[pl.ds(c, _TM, stride=_FOLD), :] = h2[:, c * LANES:(c + 1) * LANES]
    h_hi = h2.astype(BF16)
    a = _silu(jnp.dot(h_hi, wgs_ref[...], preferred_element_type=F32)) * jnp.dot(
        h_hi, wus_ref[...], preferred_element_type=F32)
    x1s_ref[...] = x1 + sl(5) * jnp.dot(a.astype(BF16), wds_ref[...], preferred_element_type=F32)
    h_lo = (h2 - h_hi.astype(F32)).astype(BF16)
    logits = _nt_dot(wrh_ref[...], h_hi) + _nt_dot(wrh_ref[...], h_lo) + _nt_dot(wrl_ref[...], h_hi)
    s = jax.nn.sigmoid(logits)
    gates_ref[...] = _route(s, s + br_ref[...])


def _merge(oac, oal, ob, ga, gb, xp, xs, mod3, g2, wpa, wpb, wo, wrh, wrl, br, wgs, wus, wds):
    row = lambda i: (i, 0)
    const = lambda shape: pl.BlockSpec(shape, lambda i: (0,) * len(shape))
    return pl.pallas_call(
        _merge_kernel,
        grid=(N_TOK // _TM,),
        in_specs=[pl.BlockSpec((_TM, D_ATT), _ctx_map),
                  pl.BlockSpec((_TM, D_ATT), _lat_map),
                  pl.BlockSpec((_TM, D_RNN), row),
                  pl.BlockSpec((_TM, D_MODEL), row),
                  pl.BlockSpec((_TM, D_MODEL), row),
                  pl.BlockSpec((_TM, D_MODEL), _ctx_map),
                  pl.BlockSpec((_TM, D_MODEL), _lat_map),
                  pl.BlockSpec((1, 1, N_MOD * D_MODEL), lambda i: (_mod_row(i, _TM), 0, 0)),
                  const((1, D_MODEL)),
                  const((D_ATT, D_MODEL)), const((D_RNN, D_MODEL)), const((D_MODEL, D_MODEL)),
                  const((N_EXPERTS, D_MODEL)), const((N_EXPERTS, D_MODEL)), const((N_EXPERTS, 1)),
                  const((D_MODEL, D_EXPERT)), const((D_MODEL, D_EXPERT)), const((D_EXPERT, D_MODEL))],
        out_specs=[pl.BlockSpec((_TM, D_MODEL), row),
                   pl.BlockSpec((_TM * _FOLD, LANES), row),
                   pl.BlockSpec((N_EXPERTS, _TM), lambda i: (0, i))],
        out_shape=[jax.ShapeDtypeStruct((N_TOK, D_MODEL), F32),
                   jax.ShapeDtypeStruct((N_TOK * _FOLD, LANES), F32),
                   jax.ShapeDtypeStruct((N_EXPERTS, N_TOK), F32)],
        compiler_params=_params(("parallel",)),
        name="merge",
    )(oac, oal, ob, ga, gb, xp, xs, mod3, g2, wpa, wpb, wo, wrh, wrl, br, wgs, wus, wds)


_FOLD = D_MODEL // LANES
_NH = N_TOK // 2
_R = 512
_LIST = -(-_NH // _R) * _R
_SLOTS = 256
_ID_BITS = 13
_DISP_BITS = 12
_VALID_BIT = _ID_BITS + _DISP_BITS + 1


def _chunk_table(cnt):
    shape = (N_EXPERTS, _SLOTS)
    nch = jnp.broadcast_to((cnt + (_R - 1)) // _R, shape)
    erow = lax.broadcasted_iota(jnp.int32, shape, 0)
    q = lax.broadcasted_iota(jnp.int32, shape, 1)
    cum = nch
    s = 1
    while s < N_EXPERTS:
        cum = cum + jnp.where(erow >= s, pltpu.roll(cum, s, 0), 0)
        s *= 2
    total = cum[N_EXPERTS - 1:N_EXPERTS, :]
    used = q[:1] < total
    expert = jnp.minimum(jnp.sum((cum <= q).astype(jnp.int32), axis=0, keepdims=True), N_EXPERTS - 1)
    first = cum - nch
    start = jnp.max(jnp.where(first <= q, first, 0), axis=0, keepdims=True)
    chunk = jnp.where(used, q[:1] - start, 0)
    return jnp.concatenate([expert, chunk, used.astype(jnp.int32), total,
                            jnp.zeros((4, _SLOTS), jnp.int32)], axis=0)


def _dispatch_kernel(g_ref, src_ref, dst_ref, gl_ref, tbl_ref, cnt_ref):
    n = g_ref.shape[1]
    ne = 16
    lane = lax.broadcasted_iota(jnp.int32, (ne, n), 1)

    def expert_group(i, carry):
        rows = pl.ds(pl.multiple_of(i * ne, ne), ne)
        g = g_ref[rows, :]
        sel = g > 0.0
        m = sel.astype(jnp.int32)
        csum = m
        s = 1
        while s < n:
            csum = csum + jnp.where(lane >= s, pltpu.roll(csum, s, 1), 0)
            s *= 2
        cnt_ref[rows, :] = jnp.broadcast_to(csum[:, n - 1:n], (ne, LANES))
        disp = lane - (csum - m)
        packed = jnp.where(sel, (1 << _VALID_BIT) | (disp << _ID_BITS) | lane, 0)
        g = jnp.where(sel, g, 0.0)
        for b in range(_DISP_BITS):
            shift = n - (1 << b)
            inc = pltpu.roll(packed, shift, 1)
            inc_moves = ((inc >> (_ID_BITS + b)) & 1) == 1
            own_moves = ((packed >> (_ID_BITS + b)) & 1) == 1
            packed = jnp.where(inc_moves, inc, jnp.where(own_moves, 0, packed))
            g = jnp.where(inc_moves, pltpu.roll(g, shift, 1), jnp.where(own_moves, 0.0, g))
        valid = (packed >> _VALID_BIT) == 1
        tok = packed & ((1 << _ID_BITS) - 1)
        src_ref[0, rows, :n] = jnp.where(valid, tok, 0) * _FOLD
        dst_ref[0, rows, :n] = jnp.where(valid, tok, n) * _FOLD
        gl_ref[0, rows, :n] = g
        if _LIST > n:
            pad = (ne, _LIST - n)
            src_ref[0, rows, n:] = jnp.zeros(pad, jnp.int32)
            dst_ref[0, rows, n:] = jnp.full(pad, n * _FOLD, jnp.int32)
            gl_ref[0, rows, n:] = jnp.zeros(pad, F32)
        return carry

    lax.fori_loop(0, N_EXPERTS // ne, expert_group, 0)
    tbl_ref[0] = _chunk_table(cnt_ref[:, :1])


def _dispatch(gates_t):
    halves = N_TOK // _NH
    blk = pl.BlockSpec((1, N_EXPERTS, _LIST), lambda h: (h, 0, 0))
    lists = jax.ShapeDtypeStruct((halves, N_EXPERTS, _LIST), jnp.int32)
    return pl.pallas_call(
        _dispatch_kernel,
        grid=(halves,),
        in_specs=[pl.BlockSpec((N_EXPERTS, _NH), lambda h: (0, h))],
        out_specs=[blk, blk, blk, pl.BlockSpec((1, 8, _SLOTS), lambda h: (h, 0, 0))],
        out_shape=[lists, lists, jax.ShapeDtypeStruct((halves, N_EXPERTS, _LIST), F32),
                   jax.ShapeDtypeStruct((halves, 8, _SLOTS), jnp.int32)],
        scratch_shapes=[pltpu.VMEM((N_EXPERTS, LANES), jnp.int32)],
        compiler_params=_params(("parallel",)),
        name="dispatch",
    )(gates_t)


_GRP = 8
_MXU = 256
_KT = D_MODEL // _MXU
_UNF = 64
_OUT_ROWS = 128
_OUT_BUFS = 4
_TBL = 8 * _SLOTS
assert _NH * TOP_K // _R + N_EXPERTS + 4 <= _SLOTS


def _tbl(tbl_ref, h, row, k):
    return tbl_ref[h * _TBL + row * _SLOTS + jnp.clip(k, 0, _SLOTS - 1)]


def _moe_kernel(tbl_ref, src_hbm, dst_hbm, gate_hbm, hf_ref, wg_hbm, wu_hbm, wd_hbm, x_hbm, mod_ref, fg_ref,
                yp_hbm, ys_hbm,
                acc_ref, wf_refs, wb_refs, xg_ref, xa_ref, xb_ref, ya_ref, yb_ref,
                src_s, dst_s, gate_v, xt_ref, yt_ref, list_sem, w_sem, in_sem, out_sem):
    h = pl.program_id(0)
    total = _tbl(tbl_ref, h, 3, 0)
    wgb_ref, wub_ref, wdb_ref = wb_refs

    for ref in (acc_ref, ya_ref, yb_ref):
        ref[...] = jnp.zeros_like(ref)

    @pl.when(h == 0)
    def _():
        for ref in (xa_ref, xb_ref, wgb_ref, wub_ref, wdb_ref):
            ref[...] = jnp.zeros_like(ref)

    def list_block(k):
        return (h * N_EXPERTS + _tbl(tbl_ref, h, 0, k)) * (_LIST // _R) + _tbl(tbl_ref, h, 1, k)

    def new_expert(k):
        return (k >= 0) & (_tbl(tbl_ref, h, 2, k) == 1) & (_tbl(tbl_ref, h, 1, k) == 0)

    def list_copies(t, p):
        return (pltpu.make_async_copy(src_hbm.at[list_block(t)], src_s.at[p], list_sem.at[p, 0]),
                pltpu.make_async_copy(dst_hbm.at[list_block(t - 2)], dst_s.at[p], list_sem.at[p, 1]),
                pltpu.make_async_copy(gate_hbm.at[list_block(t - 1)], gate_v.at[1 - p], list_sem.at[p, 2]))

    def weight_copies(t, p):
        e = _tbl(tbl_ref, h, 0, t - 1)
        return [pltpu.make_async_copy(w_hbm.at[e], w_f.at[p], w_sem.at[p, n])
                for n, (w_hbm, w_f) in enumerate(zip((wg_hbm, wu_hbm, wd_hbm), wf_refs))]

    def prefetch(t, p):
        for cp in list_copies(t, p):
            cp.start()

        @pl.when(new_expert(t - 1))
        def _():
            for cp in weight_copies(t, p):
                cp.start()

    def arrive_lists(t, p):
        for cp in list_copies(t, p):
            cp.wait()

    def arrive_weights(t, p, cast):
        @pl.when(new_expert(t - 1))
        def _():
            for cp in weight_copies(t, p):
                cp.wait()
            if cast:
                for w_f, w_b in zip(wf_refs, wb_refs):
                    w_b[...] = w_f[p].astype(BF16)

    def row(ref, r):
        return ref.at[pl.ds(pl.multiple_of(r, _FOLD), _FOLD), :]

    def stages(t, p, x_next, x_mm, y_mm, y_scatter):
        mm_used = (t >= 1) & (_tbl(tbl_ref, h, 2, t - 1) == 1)
        src_ref, dst_ref, gate_ref = src_s.at[p, 0], dst_s.at[p, 0], gate_v.at[1 - p]

        def gather_group(j0):
            for j in range(j0, j0 + _GRP):
                row(xg_ref, j * _FOLD)[...] = row(hf_ref, src_ref[j])[...]

        def scatter_group(j0):
            dsts = [row(acc_ref, dst_ref[j0 + u]) for u in range(_GRP)]
            news = [dsts[u][...] + row(y_scatter, (j0 + u) * _FOLD)[...] for u in range(_GRP)]
            for u in range(_GRP):
                dsts[u][...] = news[u]

        def unfold_block(r0):
            for k in range(_FOLD):
                x_next[r0:r0 + _UNF, k * LANES:(k + 1) * LANES] = xg_ref[
                    pl.ds(r0 * _FOLD + k, _UNF, stride=_FOLD), :].astype(BF16)

        moves = []
        for j0 in range(0, _R, _GRP):
            moves += [functools.partial(scatter_group, j0), functools.partial(gather_group, j0)]
            if (j0 + _GRP) % _UNF == 0:
                moves.append(functools.partial(unfold_block, j0 + _GRP - _UNF))
        n_pieces = 3 * _KT
        done = [0]

        def some_moves():
            done[0] += 1
            for f in moves[(done[0] - 1) * len(moves) // n_pieces:done[0] * len(moves) // n_pieces]:
                f()

        diag = (lax.broadcasted_iota(jnp.int32, (_R, _R), 0) == lax.broadcasted_iota(jnp.int32, (_R, _R), 1))
        gates = gate_ref[...] * mm_used.astype(F32)
        gcol = jnp.sum(jnp.where(diag, gates, 0.0), axis=1, keepdims=True)
        gt = up = None
        for kt in range(_KT):
            rows_k = slice(kt * _MXU, (kt + 1) * _MXU)
            xk = x_mm[:, rows_k]
            d = jnp.dot(xk, wgb_ref[rows_k, :], preferred_element_type=F32)
            gt = d if gt is None else gt + d
            some_moves()
            d = jnp.dot(xk, wub_ref[rows_k, :], preferred_element_type=F32)
            up = d if up is None else up + d
            some_moves()
        act = (_silu(gt) * up * gcol).astype(BF16)
        for nt in range(D_MODEL // _MXU):
            y = jnp.dot(act, wdb_ref[:, nt * _MXU:(nt + 1) * _MXU], preferred_element_type=F32)
            some_moves()
            for k in range(_MXU // LANES):
                y_mm[pl.ds(nt * _MXU // LANES + k, _R, stride=_FOLD), :] = y[:, k * LANES:(k + 1) * LANES]

    def step(t, p):
        arrive_lists(t, p)
        prefetch(t + 1, 1 - p)
        arrive_weights(t, p, cast=True)
        if p == 0:
            stages(t, p, xa_ref, xb_ref, yb_ref, ya_ref)
        else:
            stages(t, p, xb_ref, xa_ref, ya_ref, yb_ref)

    def two_steps(i, carry):
        step(2 * i, 0)
        step(2 * i + 1, 1)
        return carry

    pairs = (total + 3) // 2
    prefetch(0, 0)
    lax.fori_loop(0, pairs, two_steps, 0)
    arrive_lists(2 * pairs, 0)
    arrive_weights(2 * pairs, 0, cast=False)

    fg = fg_ref[...]

    def emit(half, tile0, n_tiles, dst_hbm, dst_row0, mod_row):
        gate = mod_ref[mod_row:mod_row + 1, 5 * D_MODEL:6 * D_MODEL]

        def in_copy(i):
            tok0 = half * _NH + (tile0 + i) * _OUT_ROWS
            s = i % _OUT_BUFS
            return pltpu.make_async_copy(x_hbm.at[pl.ds(tok0, _OUT_ROWS)], xt_ref.at[s], in_sem.at[s])

        def out_copy(i):
            s = i % _OUT_BUFS
            return pltpu.make_async_copy(yt_ref.at[s], dst_hbm.at[pl.ds(dst_row0 + i * _OUT_ROWS, _OUT_ROWS)],
                                         out_sem.at[s])

        for i in range(min(_OUT_BUFS - 1, n_tiles)):
            in_copy(i).start()
        for i in range(n_tiles):
            if i + _OUT_BUFS - 1 < n_tiles:
                in_copy(i + _OUT_BUFS - 1).start()
            in_copy(i).wait()
            base = (tile0 + i) * _OUT_ROWS * _FOLD
            routed = jnp.concatenate(
                [acc_ref[pl.ds(base + k, _OUT_ROWS, stride=_FOLD), :] for k in range(_FOLD)], axis=1)
            x2 = xt_ref[i % _OUT_BUFS] + gate * routed
            y = x2 * lax.rsqrt(jnp.mean(x2 * x2, axis=-1, keepdims=True) + EPS) * fg
            if i >= _OUT_BUFS:
                out_copy(i - _OUT_BUFS).wait()
            yt_ref[i % _OUT_BUFS] = y
            out_copy(i).start()
        for i in range(max(n_tiles - _OUT_BUFS, 0), n_tiles):
            out_copy(i).wait()

    per_half = _NH // _OUT_ROWS
    for half in range(N_TOK // _NH):
        @pl.when(h == half)
        def _(half=half):
            t = 0
            while t < per_half:
                tok = half * _NH + t * _OUT_ROWS
                if tok < N_CTX:
                    end_tok, dst, row0, mrow = N_CTX, yp_hbm, tok, 0
                else:
                    b = (tok - N_CTX) // DEC_SEQ
                    end_tok, dst, row0, mrow = N_CTX + (b + 1) * DEC_SEQ, ys_hbm, tok - N_CTX, 1 + b
                n = min(per_half - t, (end_tok - tok) // _OUT_ROWS)
                emit(half, t, n, dst, row0, mrow)
                t += n


def _moe(tbl, src, dst, gl, hfold, wg, wu, wd, x1s, mod, fg):
    halves = N_TOK // _NH
    hbm = pl.BlockSpec(memory_space=pl.ANY)
    const = lambda shape: pl.BlockSpec(shape, lambda h, tbl_ref: (0,) * len(shape))
    stage = pltpu.VMEM((_R * _FOLD, LANES), F32)
    packed = pltpu.VMEM((_R, D_MODEL), BF16)
    tile = pltpu.VMEM((_OUT_BUFS, _OUT_ROWS, D_MODEL), F32)
    up_shape, down_shape = (D_MODEL, D_EXPERT), (D_EXPERT, D_MODEL)
    return pl.pallas_call(
        _moe_kernel,
        grid_spec=pltpu.PrefetchScalarGridSpec(
            num_scalar_prefetch=1,
            grid=(halves,),
            in_specs=[hbm, hbm, hbm,
                      pl.BlockSpec((_NH * _FOLD, LANES), lambda h, tbl_ref: (h, 0)),
                      hbm, hbm, hbm,
                      hbm, const((8, N_MOD * D_MODEL)), const((1, D_MODEL))],
            out_specs=[hbm, hbm],
            scratch_shapes=[pltpu.VMEM(((_NH + 1) * _FOLD, LANES), F32),
                            tuple(pltpu.VMEM((2,) + s, F32) for s in (up_shape, up_shape, down_shape)),
                            tuple(pltpu.VMEM(s, BF16) for s in (up_shape, up_shape, down_shape)),
                            stage, packed, packed, stage, stage,
                            pltpu.SMEM((2, 1, _R), jnp.int32), pltpu.SMEM((2, 1, _R), jnp.int32),
                            pltpu.VMEM((2, 1, _R), F32), tile, tile,
                            pltpu.SemaphoreType.DMA((2, 3)), pltpu.SemaphoreType.DMA((2, 3)),
                            pltpu.SemaphoreType.DMA((_OUT_BUFS,)), pltpu.SemaphoreType.DMA((_OUT_BUFS,))]),
        out_shape=[jax.ShapeDtypeStruct((N_CTX, D_MODEL), F32),
                   jax.ShapeDtypeStruct((N_LAT, D_MODEL), F32)],
        compiler_params=_params(("arbitrary",), vmem=MOE_VMEM_LIMIT),
        name="moe",
    )(tbl, src, dst, gl, hfold, wg, wu, wd, x1s, mod, fg)


def kernel(x_prompt, x_sample, cache_k, cache_v, state_h_fwd, state_h_bwd, c, c_ctx, norm1_g, norm2_g, final_g, w_ada, b_ada, w_in, rpb, conv_w, conv_b, rg_wa, rg_ba, rg_wx, rg_bx, rg_lam, w_pa, w_pb, w_out, w_router, b_router, w_gate_e, w_up_e, w_down_e, w_gate_s, w_up_s, w_down_s):
    l = 0
    xp = x_prompt.reshape(N_CTX, D_MODEL)
    xs = x_sample.reshape(N_LAT, D_MODEL)

    cvecs = jnp.concatenate([c_ctx[None, :], c], axis=0)[:, :, None]
    mod = _adaln(cvecs, w_ada[l], b_ada[l][None, :])
    mod3 = mod.reshape(8, 1, N_MOD * D_MODEL)

    q, kc, vc, kl, vl, xr, yg, ga, gb = _inproj(xp, xs, mod3, norm1_g[l][None, :], w_in[l].astype(BF16))

    oac = _attn_ctx(q, kc, vc)
    ck = cache_k[:, l].reshape(DEC_BATCH, PAST_LEN, D_ATT)
    cv = cache_v[:, l].reshape(DEC_BATCH, PAST_LEN, D_ATT)
    oal = _attn_lat(q, kl, vl, ck, cv, _bias_tables(rpb[l]))

    w4 = jnp.concatenate([_block_diag(rg_wa[l, 0]), _block_diag(rg_wx[l, 0]),
                          _block_diag(rg_wa[l, 1]), _block_diag(rg_wx[l, 1])], axis=1).astype(BF16)
    b4 = jnp.concatenate([rg_ba[l, 0], rg_bx[l, 0], rg_ba[l, 1], rg_bx[l, 1]])[None, :]
    ob, hlf, hlb = _rglru(xr, yg, conv_w[l], conv_b[l][None, :], w4, b4, rg_lam[l],
                          state_h_fwd[:, l], state_h_bwd[:, l])

    wr_t = w_router[l].T
    wr_hi = wr_t.astype(BF16)
    wr_lo = (wr_t - wr_hi.astype(F32)).astype(BF16)
    x1s, hfold, gates_t = _merge(oac, oal, ob, ga, gb, xp, xs, mod3, norm2_g[l][None, :],
                                 w_pa[l].astype(BF16), w_pb[l].astype(BF16), w_out[l].astype(BF16),
                                 wr_hi, wr_lo, b_router[l][:, None],
                                 w_gate_s[l].astype(BF16), w_up_s[l].astype(BF16), w_down_s[l].astype(BF16))

    src, dst, gl, tbl = _dispatch(gates_t)
    yp, ys = _moe(tbl.reshape(-1), src.reshape(-1, 1, _R), dst.reshape(-1, 1, _R), gl.reshape(-1, 1, _R), hfold,
                  w_gate_e[l], w_up_e[l], w_down_e[l], x1s, mod, final_g[None, :])

    return (yp.reshape(BATCH, SEQ, D_MODEL),
            ys.reshape(DEC_BATCH, DEC_SEQ, D_MODEL),
            kc.reshape(BATCH, 1, SEQ, N_HEADS_A, HEAD_DIM_A),
            vc.reshape(BATCH, 1, SEQ, N_HEADS_A, HEAD_DIM_A),
            hlf.reshape(BATCH, 1, D_RNN),
            hlb.reshape(BATCH, 1, D_RNN))
```

```python
import functools
import math

import jax
import jax.numpy as jnp
from jax import lax
from jax.experimental import pallas as pl
from jax.experimental.pallas import tpu as pltpu

F32 = jnp.float32
BF16 = jnp.bfloat16

D_MODEL = 1024
BATCH = 16
SEQ = 256
DEC_BATCH = 2
DEC_SEQ = 1024
PAST_LEN = 512
GRID_W = 64
N_HEADS_A = 8
HEAD_DIM_A = 64
D_ATT = N_HEADS_A * HEAD_DIM_A
KH = 8
KW = 16
D_RNN = 512
CONV_W = 4
RG_C = 8.0
N_EXPERTS = 64
TOP_K = 8
N_GROUPS = 8
GROUP_SIZE = N_EXPERTS // N_GROUPS
TOPK_GROUP = 4
D_EXPERT = 256
ROUTED_SCALE = 2.5
N_MOD = 6
EPS = 1e-6
NEG = -1e30
D_IN = 3 * D_ATT + 2 * D_RNN + 2 * D_MODEL

N_CTX = BATCH * SEQ
N_LAT = DEC_BATCH * DEC_SEQ
N_TOK = N_CTX + N_LAT
GRID_ROWS = DEC_SEQ // GRID_W

LANES = 128
VMEM_LIMIT = 56 * 1024 * 1024
MOE_VMEM_LIMIT = 60 * 1024 * 1024


def _params(sem, vmem=VMEM_LIMIT):
    return pltpu.CompilerParams(dimension_semantics=sem, vmem_limit_bytes=vmem)


def _mod_row(i, tile_rows):
    n_ctx_tiles = N_CTX // tile_rows
    return jnp.where(i < n_ctx_tiles, 0, 1 + (i - n_ctx_tiles) // (DEC_SEQ // tile_rows))


def _norm_mod(x, g, shift, scale):
    y = x * lax.rsqrt(jnp.mean(x * x, axis=-1, keepdims=True) + EPS)
    return (y * g) * (1.0 + scale) + shift


def _sigmoid(x):
    return 0.5 * jnp.tanh(0.5 * x) + 0.5


def _silu(x):
    return x * _sigmoid(x)


def _adaln_kernel(c_ref, w_ref, b_ref, o_ref):
    w = w_ref[...]
    rows = []
    for r in range(3):
        s = _silu(c_ref[r])
        rows.append(jnp.sum(w * s, axis=0, keepdims=True))
    rows.append(jnp.zeros((5, w.shape[1]), F32))
    o_ref[...] = jnp.concatenate(rows, axis=0) + b_ref[...]


def _adaln(cvecs, w_ada, b_ada):
    tn = 2048
    n = N_MOD * D_MODEL
    return pl.pallas_call(
        _adaln_kernel,
        grid=(n // tn,),
        in_specs=[pl.BlockSpec((3, D_MODEL, 1), lambda j: (0, 0, 0)),
                  pl.BlockSpec((D_MODEL, tn), lambda j: (0, j)),
                  pl.BlockSpec((1, tn), lambda j: (0, j))],
        out_specs=pl.BlockSpec((8, tn), lambda j: (0, j)),
        out_shape=jax.ShapeDtypeStruct((8, n), F32),
        compiler_params=_params(("parallel",)),
        name="adaln",
    )(cvecs, w_ada, b_ada)


_TM = 512
_W_CHUNK = 512
_CTX_TILES = N_CTX // _TM


def _ctx_map(i):
    return (jnp.minimum(i, _CTX_TILES - 1), 0)


def _lat_map(i):
    return (jnp.maximum(i - _CTX_TILES, 0), 0)


def _inproj_kernel(xp_ref, xs_ref, mod_ref, g_ref, w_hbm,
                   q_ref, kc_ref, vc_ref, kl_ref, vl_ref, xr_ref, yg_ref, ga_ref, gb_ref,
                   w_ref, wstage_ref, w_sem):
    i = pl.program_id(0)
    is_ctx = i < _CTX_TILES

    n_chunks = D_IN // _W_CHUNK

    def chunk_copy(c):
        cols = pl.ds(c * _W_CHUNK, _W_CHUNK)
        return pltpu.make_async_copy(w_hbm.at[:, cols], wstage_ref.at[c % 2], w_sem.at[c % 2])

    @pl.when(i == 0)
    def _():
        chunk_copy(0).start()

    x = jnp.where(is_ctx, xp_ref[...], xs_ref[...])
    m = mod_ref[0]
    h = _norm_mod(x, g_ref[...], m[:, 0:D_MODEL], m[:, D_MODEL:2 * D_MODEL]).astype(BF16)

    def proj(a, b):
        @pl.when(i == 0)
        def _():
            for c in range(a // _W_CHUNK, b // _W_CHUNK):
                if c + 1 < n_chunks:
                    chunk_copy(c + 1).start()
                chunk_copy(c).wait()
                w_ref[:, c * _W_CHUNK:(c + 1) * _W_CHUNK] = wstage_ref[c % 2].astype(BF16)

        return jnp.dot(h, w_ref[:, a:b], preferred_element_type=F32)

    q_ref[...] = proj(0, D_ATT).astype(BF16)
    k = proj(D_ATT, 2 * D_ATT)
    v = proj(2 * D_ATT, 3 * D_ATT)

    @pl.when(is_ctx)
    def _():
        kc_ref[...] = k
        vc_ref[...] = v

    @pl.when(jnp.logical_not(is_ctx))
    def _():
        kl_ref[...] = k
        vl_ref[...] = v

    o = 3 * D_ATT
    xr_ref[...] = proj(o, o + D_RNN)
    yg_ref[...] = proj(o + D_RNN, o + 2 * D_RNN).astype(BF16)
    o += 2 * D_RNN
    ga_ref[...] = proj(o, o + D_MODEL).astype(BF16)
    gb_ref[...] = proj(o + D_MODEL, o + 2 * D_MODEL).astype(BF16)


def _inproj(xp, xs, mod3, g1, w_in):
    row = lambda i: (i, 0)
    return pl.pallas_call(
        _inproj_kernel,
        grid=(N_TOK // _TM,),
        in_specs=[pl.BlockSpec((_TM, D_MODEL), _ctx_map),
                  pl.BlockSpec((_TM, D_MODEL), _lat_map),
                  pl.BlockSpec((1, 1, N_MOD * D_MODEL), lambda i: (_mod_row(i, _TM), 0, 0)),
                  pl.BlockSpec((1, D_MODEL), lambda i: (0, 0)),
                  pl.BlockSpec(memory_space=pl.ANY)],
        out_specs=[pl.BlockSpec((_TM, D_ATT), row),
                   pl.BlockSpec((_TM, D_ATT), _ctx_map),
                   pl.BlockSpec((_TM, D_ATT), _ctx_map),
                   pl.BlockSpec((_TM, D_ATT), _lat_map),
                   pl.BlockSpec((_TM, D_ATT), _lat_map),
                   pl.BlockSpec((_TM, D_RNN), row),
                   pl.BlockSpec((_TM, D_RNN), row),
                   pl.BlockSpec((_TM, D_MODEL), row),
                   pl.BlockSpec((_TM, D_MODEL), row)],
        out_shape=[jax.ShapeDtypeStruct((N_TOK, D_ATT), BF16),
                   jax.ShapeDtypeStruct((N_CTX, D_ATT), F32),
                   jax.ShapeDtypeStruct((N_CTX, D_ATT), F32),
                   jax.ShapeDtypeStruct((N_LAT, D_ATT), F32),
                   jax.ShapeDtypeStruct((N_LAT, D_ATT), F32),
                   jax.ShapeDtypeStruct((N_TOK, D_RNN), F32),
                   jax.ShapeDtypeStruct((N_TOK, D_RNN), BF16),
                   jax.ShapeDtypeStruct((N_TOK, D_MODEL), BF16),
                   jax.ShapeDtypeStruct((N_TOK, D_MODEL), BF16)],
        scratch_shapes=[pltpu.VMEM((D_MODEL, D_IN), BF16),
                        pltpu.VMEM((2, D_MODEL, _W_CHUNK), F32),
                        pltpu.SemaphoreType.DMA((2,))],
        compiler_params=_params(("arbitrary",)),
        name="inproj",
    )(xp, xs, mod3, g1, w_in)


def _pair_attention(qp, segs):
    scale = HEAD_DIM_A ** -0.5
    assert math.log2(scale).is_integer()
    lane = lax.broadcasted_iota(jnp.int32, (1, LANES), 1)
    qs = qp * scale
    outs = []
    for a in range(2):
        sel = (lane >= HEAD_DIM_A) if a else (lane < HEAD_DIM_A)
        qm = jnp.where(sel, qs, jnp.zeros_like(qs))
        ss = []
        for kp, _, bias_fn in segs:
            s = lax.dot_general(qm, kp, (((1,), (1,)), ((), ())), preferred_element_type=F32)
            if bias_fn is not None:
                s = s + bias_fn(a)
            ss.append(s)
        m = functools.reduce(jnp.maximum, [jnp.max(s, axis=-1, keepdims=True) for s in ss])
        es = [jnp.exp(s - m) for s in ss]
        inv = 1.0 / functools.reduce(jnp.add, [jnp.sum(e, axis=-1, keepdims=True) for e in es])
        o = inv * functools.reduce(jnp.add, [
            jnp.dot(e.astype(BF16), vp, preferred_element_type=F32)
            for e, (_, vp, _) in zip(es, segs)])
        outs.append(o)
    return jnp.where(lane < HEAD_DIM_A, outs[0], outs[1])


def _attn_ctx_kernel(q_ref, k_ref, v_ref, o_ref):
    for j in range(D_ATT // LANES):
        c = slice(j * LANES, (j + 1) * LANES)
        segs = [(k_ref[:, c].astype(BF16), v_ref[:, c].astype(BF16), None)]
        o_ref[:, c] = _pair_attention(q_ref[:, c], segs).astype(BF16)


def _attn_ctx(q_all, kc, vc):
    blk = pl.BlockSpec((SEQ, D_ATT), lambda b: (b, 0))
    return pl.pallas_call(
        _attn_ctx_kernel,
        grid=(BATCH,),
        in_specs=[blk, blk, blk],
        out_specs=blk,
        out_shape=jax.ShapeDtypeStruct((N_CTX, D_ATT), BF16),
        compiler_params=_params(("parallel",)),
        name="attn_ctx",
    )(q_all, kc, vc)


_QROWS = 4
_QB = _QROWS * GRID_W
_KROWS = 12
_KB = _KROWS * GRID_W


def _build_bias_tables(rows_ref, t2_ref):
    wq = lax.broadcasted_iota(jnp.int32, (GRID_W, LANES), 0)
    wk = lax.broadcasted_iota(jnp.int32, (GRID_W, LANES), 1) & (GRID_W - 1)
    col_start = jnp.clip(wq - KW // 2, 0, GRID_W - KW)
    col_in = (wk >= col_start) & (wk < col_start + KW)
    for r in range(N_HEADS_A * 2 * KH):
        x = jnp.broadcast_to(rows_ref[r:r + 1, :], (GRID_W, LANES))
        x = pltpu.roll(x, LANES - (KW - 1), 1, stride=1, stride_axis=0)
        t2_ref[r // (2 * KH), r % (2 * KH)] = jnp.where(col_in, x, NEG)


def _attn_lat_kernel(q_ref, k_ref, v_ref, ck_ref, cv_ref, rows_ref, o_ref, t2_ref):
    i = pl.program_id(1)

    @pl.when((pl.program_id(0) == 0) & (i == 0))
    def _():
        _build_bias_tables(rows_ref, t2_ref)

    ks = jnp.where(i < 2, 0, GRID_ROWS - _KROWS)
    kstart = pl.multiple_of(ks * GRID_W, 256)
    lane = lax.broadcasted_iota(jnp.int32, (1, LANES), 1)

    def bias_for_head(h):
        rows = []
        for rq in range(_QROWS):
            r = _QROWS * i + rq
            rs = jnp.clip(r - KH // 2, 0, GRID_ROWS - KH)
            tiles = []
            for t in range(_KROWS // 2):
                kr = ks + 2 * t
                d = jnp.clip(kr - r + (KH - 1), -1, 2 * KH - 2)
                tile = t2_ref[h, d + 1]
                v0 = ((kr >= rs) & (kr < rs + KH)).astype(jnp.int32)
                v1 = ((kr + 1 >= rs) & (kr + 1 < rs + KH)).astype(jnp.int32)
                valid = jnp.where(lane < GRID_W, v0, v1) > 0
                tiles.append(jnp.where(valid, tile, NEG))
            rows.append(jnp.concatenate(tiles, axis=1))
        return jnp.concatenate(rows, axis=0)

    for j in range(D_ATT // LANES):
        c = slice(j * LANES, (j + 1) * LANES)
        k_loc = k_ref[pl.ds(kstart, _KB), c].astype(BF16)
        v_loc = v_ref[pl.ds(kstart, _KB), c].astype(BF16)
        segs = [(k_loc, v_loc, lambda a, j=j: bias_for_head(2 * j + a)),
                (ck_ref[:, c].astype(BF16), cv_ref[:, c].astype(BF16), None)]
        o_ref[:, c] = _pair_attention(q_ref[:, c], segs).astype(BF16)


def _attn_lat(q_all, kl, vl, ck, cv, t2):
    qblocks = DEC_SEQ // _QB
    return pl.pallas_call(
        _attn_lat_kernel,
        grid=(DEC_BATCH, qblocks),
        in_specs=[pl.BlockSpec((_QB, D_ATT), lambda b, i: (N_CTX // _QB + b * qblocks + i, 0)),
                  pl.BlockSpec((DEC_SEQ, D_ATT), lambda b, i: (b, 0)),
                  pl.BlockSpec((DEC_SEQ, D_ATT), lambda b, i: (b, 0)),
                  pl.BlockSpec((None, PAST_LEN, D_ATT), lambda b, i: (b, 0, 0)),
                  pl.BlockSpec((None, PAST_LEN, D_ATT), lambda b, i: (b, 0, 0)),
                  pl.BlockSpec((N_HEADS_A * 2 * KH, LANES), lambda b, i: (0, 0))],
        out_specs=pl.BlockSpec((_QB, D_ATT), lambda b, i: (b * qblocks + i, 0)),
        out_shape=jax.ShapeDtypeStruct((N_LAT, D_ATT), BF16),
        scratch_shapes=[pltpu.VMEM((N_HEADS_A, 2 * KH, GRID_W, LANES), F32)],
        compiler_params=_params(("arbitrary", "arbitrary")),
        name="attn_lat",
    )(q_all, kl, vl, ck, cv, t2)


def _bias_tables(rpb):
    half = jnp.pad(rpb, ((0, 0), (0, 0), (0, GRID_W - rpb.shape[-1])))
    neg = jnp.full((N_HEADS_A, 1, GRID_W), NEG, F32)
    left = jnp.concatenate([neg, half], axis=1)
    right = jnp.concatenate([half, neg], axis=1)
    return jnp.concatenate([left, right], axis=-1).reshape(N_HEADS_A * 2 * KH, LANES)


_RG_ROWS = 2048
_RG_CHUNK = 256
_RG_SLABS = D_RNN // LANES
_LOG2E = 1.4426950408889634
_RG_CTX_STEPS = N_CTX // _RG_ROWS


def _rg_block(seq_len, h0f, h0b, xr_ref, yg_ref, cw_ref, cb_ref, w4_ref, b4_ref, lam_ref,
              ob_ref, xc_ref, af_ref, uf_ref, ab_ref, ub_ref, hf_ref, hb_ref, cf_ref, cr_ref):
    T, R = _RG_CHUNK, _RG_ROWS
    nb = R // T
    segs = seq_len // T
    linked = segs > 1
    xr = xr_ref[...]
    t_idx = lax.broadcasted_iota(jnp.int32, (R, 1), 0) & (seq_len - 1)
    cw = cw_ref[...]
    xc = jnp.zeros((R, D_RNN), F32) + cb_ref[...]
    for j in range(CONV_W):
        off = j - 2
        if off == 0:
            tap = xr
        else:
            tap = pltpu.roll(xr, shift=(-off) % R, axis=0)
            tap = jnp.where((t_idx + off >= 0) & (t_idx + off < seq_len), tap, 0.0)
        xc = xc + cw[j:j + 1, :] * tap
    xc_ref[...] = xc

    lam = lam_ref[...]
    z = -lam
    sp = jnp.maximum(z, 0.0) + jnp.log1p(jnp.exp(-jnp.abs(z)))
    decay_k = (-0.5 * RG_C * _LOG2E) * sp

    def put(ref, c, val):
        for k in range(_RG_SLABS):
            ref[k, pl.ds(c, T, stride=nb), :] = val[:, k * LANES:(k + 1) * LANES]

    def get(ref, c):
        return jnp.concatenate([ref[k, pl.ds(c, T, stride=nb), :] for k in range(_RG_SLABS)], axis=1)

    def gate_chunk(c, carry):
        x = xc_ref[pl.ds(pl.multiple_of(c * T, T), T), :]
        g4 = jnp.dot(x.astype(BF16), w4_ref[...], preferred_element_type=F32) + b4_ref[...]
        hx = 0.5 * x
        for d, (a_ref, u_ref) in enumerate(((af_ref, uf_ref), (ab_ref, ub_ref))):
            tr = jnp.tanh(0.5 * g4[:, (2 * d) * D_RNN:(2 * d + 1) * D_RNN])
            ti = jnp.tanh(0.5 * g4[:, (2 * d + 1) * D_RNN:(2 * d + 2) * D_RNN])
            k = decay_k[d:d + 1, :]
            a = jnp.exp2(k * tr + k)
            put(a_ref, c, a)
            m = 1.0 - a * a
            root = jnp.where(m > 0.0, m * lax.rsqrt(m), 0.0)
            put(u_ref, c, root * (hx * ti + hx))
        return carry

    lax.fori_loop(0, nb, gate_chunk, 0)

    def step(t, carry):
        hf, hb, pf, pb = carry
        rf = pl.ds(pl.multiple_of(t * nb, nb), nb)
        rb = pl.ds(pl.multiple_of((T - 1 - t) * nb, nb), nb)
        nf, nbk, npf, npb = [], [], [], []
        for k in range(_RG_SLABS):
            a_f, a_b = af_ref[k, rf, :], ab_ref[k, rb, :]
            f = a_f * hf[k] + uf_ref[k, rf, :]
            b = a_b * hb[k] + ub_ref[k, rb, :]
            hf_ref[k, rf, :] = f
            hb_ref[k, rb, :] = b
            nf.append(f)
            nbk.append(b)
            if linked:
                npf.append(a_f * pf[k])
                npb.append(a_b * pb[k])
                af_ref[k, rf, :] = npf[-1]
                ab_ref[k, rb, :] = npb[-1]
        return tuple(nf), tuple(nbk), tuple(npf), tuple(npb)

    zeros = (jnp.zeros((nb, LANES), F32),) * _RG_SLABS
    ones = (jnp.ones((nb, LANES), F32),) * _RG_SLABS if linked else ()
    hf, hb, pf, pb = lax.fori_loop(0, T, step, (zeros, zeros, ones, ones))

    if linked:
        seg_row = lax.broadcasted_iota(jnp.int32, (nb, LANES), 0)
        for k in range(_RG_SLABS):
            lanes = slice(k * LANES, (k + 1) * LANES)
            enter_f = jnp.zeros((nb, LANES), F32)
            enter_b = jnp.zeros((nb, LANES), F32)
            for s0 in range(0, nb, segs):
                c = h0f[s0 // segs:s0 // segs + 1, lanes]
                for q in range(s0, s0 + segs):
                    enter_f = jnp.where(seg_row == q, c, enter_f)
                    c = hf[k][q:q + 1, :] + pf[k][q:q + 1, :] * c
                c = h0b[s0 // segs:s0 // segs + 1, lanes]
                for q in reversed(range(s0, s0 + segs)):
                    enter_b = jnp.where(seg_row == q, c, enter_b)
                    c = hb[k][q:q + 1, :] + pb[k][q:q + 1, :] * c
            cf_ref[:, lanes] = enter_f
            cr_ref[:, lanes] = enter_b

    def out_chunk(c, carry):
        rows = pl.ds(pl.multiple_of(c * T, T), T)
        hsum = get(hf_ref, c) + get(hb_ref, c)
        if linked:
            hsum = hsum + get(af_ref, c) * cf_ref[pl.ds(c, 1), :] + get(ab_ref, c) * cr_ref[pl.ds(c, 1), :]
        ob_ref[rows, :] = (hsum * jax.nn.gelu(yg_ref[rows, :].astype(F32))).astype(BF16)
        return carry

    lax.fori_loop(0, nb, out_chunk, 0)
    return hf, hb


def _rglru_kernel(xr_ref, yg_ref, cw_ref, cb_ref, w4_ref, b4_ref, lam_ref, sf_ref, sb_ref,
                  ob_ref, hlf_ref, hlb_ref, *scratch):
    i = pl.program_id(0)
    shared = (xr_ref, yg_ref, cw_ref, cb_ref, w4_ref, b4_ref, lam_ref, ob_ref) + scratch

    @pl.when(i < _RG_CTX_STEPS)
    def _():
        hf, hb = _rg_block(SEQ, None, None, *shared)
        hlf_ref[...] = jnp.concatenate(hf, axis=1)
        hlb_ref[...] = jnp.concatenate(hb, axis=1)

    @pl.when(i >= _RG_CTX_STEPS)
    def _():
        _rg_block(DEC_SEQ, sf_ref[...], sb_ref[...], *shared)


def _rglru(xr, yg, conv_w, conv_b, w4, b4, lam, sf, sb):
    steps = N_TOK // _RG_ROWS
    nb_ctx = _RG_ROWS // SEQ
    const = lambda shape: pl.BlockSpec(shape, lambda i: (0,) * len(shape))
    hl_map = lambda i: (jnp.minimum(i, _RG_CTX_STEPS - 1), 0)
    scan_buf = pltpu.VMEM((_RG_SLABS, _RG_ROWS, LANES), F32)
    enter_buf = pltpu.VMEM((_RG_ROWS // _RG_CHUNK, D_RNN), F32)
    return pl.pallas_call(
        _rglru_kernel,
        grid=(steps,),
        in_specs=[pl.BlockSpec((_RG_ROWS, D_RNN), lambda i: (i, 0)),
                  pl.BlockSpec((_RG_ROWS, D_RNN), lambda i: (i, 0)),
                  const((CONV_W, D_RNN)), const((1, D_RNN)),
                  const((D_RNN, 4 * D_RNN)), const((1, 4 * D_RNN)), const((2, D_RNN)),
                  const((DEC_BATCH, D_RNN)), const((DEC_BATCH, D_RNN))],
        out_specs=[pl.BlockSpec((_RG_ROWS, D_RNN), lambda i: (i, 0)),
                   pl.BlockSpec((nb_ctx, D_RNN), hl_map),
                   pl.BlockSpec((nb_ctx, D_RNN), hl_map)],
        out_shape=[jax.ShapeDtypeStruct((N_TOK, D_RNN), BF16),
                   jax.ShapeDtypeStruct((BATCH, D_RNN), F32),
                   jax.ShapeDtypeStruct((BATCH, D_RNN), F32)],
        scratch_shapes=[pltpu.VMEM((_RG_ROWS, D_RNN), F32)] + [scan_buf] * 6 + [enter_buf] * 2,
        compiler_params=_params(("arbitrary",)),
        name="rglru",
    )(xr, yg, conv_w, conv_b, w4, b4, lam, sf, sb)


def _block_diag(w):
    n, d, _ = w.shape
    eye = jnp.eye(n, dtype=w.dtype)
    return jnp.einsum('nij,nm->nimj', w, eye).reshape(n * d, n * d)


def _nt_dot(a, b):
    return lax.dot_general(a, b, (((1,), (1,)), ((), ())), preferred_element_type=F32)


def _route(s, sb):
    n = s.shape[1]
    low = -3.0e38
    sb3 = sb.reshape(N_GROUPS, GROUP_SIZE, n)
    iw = lax.broadcasted_iota(jnp.int32, sb3.shape, 1)
    m1 = jnp.max(sb3, axis=1, keepdims=True)
    i1 = jnp.min(jnp.where(sb3 == m1, iw, GROUP_SIZE), axis=1, keepdims=True)
    m2 = jnp.max(jnp.where(iw == i1, low, sb3), axis=1, keepdims=True)
    gscore = m1 + m2
    ig = lax.broadcasted_iota(jnp.int32, gscore.shape, 0)
    beaten = jnp.zeros(gscore.shape, jnp.int32)
    for g in range(N_GROUPS):
        row = gscore[g:g + 1]
        beaten = beaten + ((row > gscore) | ((row == gscore) & (g < ig))).astype(jnp.int32)
    gsel = beaten < TOPK_GROUP
    work = jnp.where(gsel, sb3, NEG).reshape(N_EXPERTS, n)
    ie = lax.broadcasted_iota(jnp.int32, work.shape, 0)
    chosen = jnp.zeros(work.shape, jnp.bool_)
    for _ in range(TOP_K):
        m = jnp.max(work, axis=0, keepdims=True)
        pick = ie == jnp.min(jnp.where(work == m, ie, N_EXPERTS), axis=0, keepdims=True)
        chosen = chosen | pick
        work = jnp.where(pick, low, work)
    wsel = jnp.where(chosen, s, 0.0)
    return wsel / jnp.sum(wsel, axis=0, keepdims=True) * ROUTED_SCALE


def _merge_kernel(oac_ref, oal_ref, ob_ref, ga_ref, gb_ref, xp_ref, xs_ref, mod_ref, g_ref,
                  wpa_ref, wpb_ref, wo_ref, wrh_ref, wrl_ref, br_ref, wgs_ref, wus_ref, wds_ref,
                  x1s_ref, hf_ref, gates_ref):
    i = pl.program_id(0)
    is_ctx = i < _CTX_TILES
    oa = jnp.where(is_ctx, oac_ref[...], oal_ref[...])
    x = jnp.where(is_ctx, xp_ref[...], xs_ref[...])
    m = mod_ref[0]
    sl = lambda k: m[:, k * D_MODEL:(k + 1) * D_MODEL]
    mix = (_sigmoid(ga_ref[...].astype(F32)) * jnp.dot(oa, wpa_ref[...], preferred_element_type=F32)
           + _sigmoid(gb_ref[...].astype(F32)) * jnp.dot(ob_ref[...], wpb_ref[...], preferred_element_type=F32))
    o = jnp.dot(mix.astype(BF16), wo_ref[...], preferred_element_type=F32)
    x1 = x + sl(2) * o
    h2 = _norm_mod(x1, g_ref[...], sl(3), sl(4))
    for c in range(_FOLD):
        hf_ref[pl.ds(c, _TM, stride=_FOLD), :] = h2[:, c * LANES:(c + 1) * LANES]
    h_hi = h2.astype(BF16)
    a = _silu(jnp.dot(h_hi, wgs_ref[...], preferred_element_type=F32)) * jnp.dot(
        h_hi, wus_ref[...], preferred_element_type=F32)
    x1s_ref[...] = x1 + sl(5) * jnp.dot(a.astype(BF16), wds_ref[...], preferred_element_type=F32)
    h_lo = (h2 - h_hi.astype(F32)).astype(BF16)
    logits = _nt_dot(wrh_ref[...], h_hi) + _nt_dot(wrh_ref[...], h_lo) + _nt_dot(wrl_ref[...], h_hi)
    s = jax.nn.sigmoid(logits)
    gates_ref[...] = _route(s, s + br_ref[...])


def _merge(oac, oal, ob, ga, gb, xp, xs, mod3, g2, wpa, wpb, wo, wrh, wrl, br, wgs, wus, wds):
    row = lambda i: (i, 0)
    const = lambda shape: pl.BlockSpec(shape, lambda i: (0,) * len(shape))
    return pl.pallas_call(
        _merge_kernel,
        grid=(N_TOK // _TM,),
        in_specs=[pl.BlockSpec((_TM, D_ATT), _ctx_map),
                  pl.BlockSpec((_TM, D_ATT), _lat_map),
                  pl.BlockSpec((_TM, D_RNN), row),
                  pl.BlockSpec((_TM, D_MODEL), row),
                  pl.BlockSpec((_TM, D_MODEL), row),
                  pl.BlockSpec((_TM, D_MODEL), _ctx_map),
                  pl.BlockSpec((_TM, D_MODEL), _lat_map),
                  pl.BlockSpec((1, 1, N_MOD * D_MODEL), lambda i: (_mod_row(i, _TM), 0, 0)),
                  const((1, D_MODEL)),
                  const((D_ATT, D_MODEL)), const((D_RNN, D_MODEL)), const((D_MODEL, D_MODEL)),
                  const((N_EXPERTS, D_MODEL)), const((N_EXPERTS, D_MODEL)), const((N_EXPERTS, 1)),
                  const((D_MODEL, D_EXPERT)), const((D_MODEL, D_EXPERT)), const((D_EXPERT, D_MODEL))],
        out_specs=[pl.BlockSpec((_TM, D_MODEL), row),
                   pl.BlockSpec((_TM * _FOLD, LANES), row),
                   pl.BlockSpec((N_EXPERTS, _TM), lambda i: (0, i))],
        out_shape=[jax.ShapeDtypeStruct((N_TOK, D_MODEL), F32),
                   jax.ShapeDtypeStruct((N_TOK * _FOLD, LANES), F32),
                   jax.ShapeDtypeStruct((N_EXPERTS, N_TOK), F32)],
        compiler_params=_params(("parallel",)),
        name="merge",
    )(oac, oal, ob, ga, gb, xp, xs, mod3, g2, wpa, wpb, wo, wrh, wrl, br, wgs, wus, wds)


_FOLD = D_MODEL // LANES
_NH = N_TOK // 2
_R = 512
_LIST = -(-_NH // _R) * _R
_SLOTS = 256
_ID_BITS = 13
_DISP_BITS = 12
_VALID_BIT = _ID_BITS + _DISP_BITS + 1


def _chunk_table(cnt):
    shape = (N_EXPERTS, _SLOTS)
    nch = jnp.broadcast_to((cnt + (_R - 1)) // _R, shape)
    erow = lax.broadcasted_iota(jnp.int32, shape, 0)
    q = lax.broadcasted_iota(jnp.int32, shape, 1)
    cum = nch
    s = 1
    while s < N_EXPERTS:
        cum = cum + jnp.where(erow >= s, pltpu.roll(cum, s, 0), 0)
        s *= 2
    total = cum[N_EXPERTS - 1:N_EXPERTS, :]
    used = q[:1] < total
    expert = jnp.minimum(jnp.sum((cum <= q).astype(jnp.int32), axis=0, keepdims=True), N_EXPERTS - 1)
    first = cum - nch
    start = jnp.max(jnp.where(first <= q, first, 0), axis=0, keepdims=True)
    chunk = jnp.where(used, q[:1] - start, 0)
    return jnp.concatenate([expert, chunk, used.astype(jnp.int32), total,
                            jnp.zeros((4, _SLOTS), jnp.int32)], axis=0)


def _dispatch_kernel(g_ref, src_ref, dst_ref, gl_ref, tbl_ref, cnt_ref):
    n = g_ref.shape[1]
    ne = 16
    lane = lax.broadcasted_iota(jnp.int32, (ne, n), 1)

    def expert_group(i, carry):
        rows = pl.ds(pl.multiple_of(i * ne, ne), ne)
        g = g_ref[rows, :]
        sel = g > 0.0
        m = sel.astype(jnp.int32)
        csum = m
        s = 1
        while s < n:
            csum = csum + jnp.where(lane >= s, pltpu.roll(csum, s, 1), 0)
            s *= 2
        cnt_ref[rows, :] = jnp.broadcast_to(csum[:, n - 1:n], (ne, LANES))
        disp = lane - (csum - m)
        packed = jnp.where(sel, (1 << _VALID_BIT) | (disp << _ID_BITS) | lane, 0)
        g = jnp.where(sel, g, 0.0)
        for b in range(_DISP_BITS):
            shift = n - (1 << b)
            inc = pltpu.roll(packed, shift, 1)
            inc_moves = ((inc >> (_ID_BITS + b)) & 1) == 1
            own_moves = ((packed >> (_ID_BITS + b)) & 1) == 1
            packed = jnp.where(inc_moves, inc, jnp.where(own_moves, 0, packed))
            g = jnp.where(inc_moves, pltpu.roll(g, shift, 1), jnp.where(own_moves, 0.0, g))
        valid = (packed >> _VALID_BIT) == 1
        tok = packed & ((1 << _ID_BITS) - 1)
        src_ref[0, rows, :n] = jnp.where(valid, tok, 0) * _FOLD
        dst_ref[0, rows, :n] = jnp.where(valid, tok, n) * _FOLD
        gl_ref[0, rows, :n] = g
        if _LIST > n:
            pad = (ne, _LIST - n)
            src_ref[0, rows, n:] = jnp.zeros(pad, jnp.int32)
            dst_ref[0, rows, n:] = jnp.full(pad, n * _FOLD, jnp.int32)
            gl_ref[0, rows, n:] = jnp.zeros(pad, F32)
        return carry

    lax.fori_loop(0, N_EXPERTS // ne, expert_group, 0)
    tbl_ref[0] = _chunk_table(cnt_ref[:, :1])


def _dispatch(gates_t):
    halves = N_TOK // _NH
    blk = pl.BlockSpec((1, N_EXPERTS, _LIST), lambda h: (h, 0, 0))
    lists = jax.ShapeDtypeStruct((halves, N_EXPERTS, _LIST), jnp.int32)
    return pl.pallas_call(
        _dispatch_kernel,
        grid=(halves,),
        in_specs=[pl.BlockSpec((N_EXPERTS, _NH), lambda h: (0, h))],
        out_specs=[blk, blk, blk, pl.BlockSpec((1, 8, _SLOTS), lambda h: (h, 0, 0))],
        out_shape=[lists, lists, jax.ShapeDtypeStruct((halves, N_EXPERTS, _LIST), F32),
                   jax.ShapeDtypeStruct((halves, 8, _SLOTS), jnp.int32)],
        scratch_shapes=[pltpu.VMEM((N_EXPERTS, LANES), jnp.int32)],
        compiler_params=_params(("parallel",)),
        name="dispatch",
    )(gates_t)


_GRP = 8
_MXU = 256
_KT = D_MODEL // _MXU
_UNF = 64
_OUT_ROWS = 128
_OUT_BUFS = 4
_TBL = 8 * _SLOTS
assert _NH * TOP_K // _R + N_EXPERTS + 4 <= _SLOTS


def _tbl(tbl_ref, h, row, k):
    return tbl_ref[h * _TBL + row * _SLOTS + jnp.clip(k, 0, _SLOTS - 1)]


def _moe_kernel(tbl_ref, src_hbm, dst_hbm, gate_hbm, hf_ref, wg_hbm, wu_hbm, wd_hbm, x_hbm, mod_ref, fg_ref,
                yp_hbm, ys_hbm,
                acc_ref, wf_refs, wb_refs, xg_ref, xa_ref, xb_ref, ya_ref, yb_ref,
                src_s, dst_s, gate_v, xt_ref, yt_ref, list_sem, w_sem, in_sem, out_sem):
    h = pl.program_id(0)
    total = _tbl(tbl_ref, h, 3, 0)
    wgb_ref, wub_ref, wdb_ref = wb_refs

    for ref in (acc_ref, ya_ref, yb_ref):
        ref[...] = jnp.zeros_like(ref)

    @pl.when(h == 0)
    def _():
        for ref in (xa_ref, xb_ref, wgb_ref, wub_ref, wdb_ref):
            ref[...] = jnp.zeros_like(ref)

    def list_block(k):
        return (h * N_EXPERTS + _tbl(tbl_ref, h, 0, k)) * (_LIST // _R) + _tbl(tbl_ref, h, 1, k)

    def new_expert(k):
        return (k >= 0) & (_tbl(tbl_ref, h, 2, k) == 1) & (_tbl(tbl_ref, h, 1, k) == 0)

    def list_copies(t, p):
        return (pltpu.make_async_copy(src_hbm.at[list_block(t)], src_s.at[p], list_sem.at[p, 0]),
                pltpu.make_async_copy(dst_hbm.at[list_block(t - 2)], dst_s.at[p], list_sem.at[p, 1]),
                pltpu.make_async_copy(gate_hbm.at[list_block(t - 1)], gate_v.at[1 - p], list_sem.at[p, 2]))

    def weight_copies(t, p):
        e = _tbl(tbl_ref, h, 0, t - 1)
        return [pltpu.make_async_copy(w_hbm.at[e], w_f.at[p], w_sem.at[p, n])
                for n, (w_hbm, w_f) in enumerate(zip((wg_hbm, wu_hbm, wd_hbm), wf_refs))]

    def prefetch(t, p):
        for cp in list_copies(t, p):
            cp.start()

        @pl.when(new_expert(t - 1))
        def _():
            for cp in weight_copies(t, p):
                cp.start()

    def arrive_lists(t, p):
        for cp in list_copies(t, p):
            cp.wait()

    def arrive_weights(t, p, cast):
        @pl.when(new_expert(t - 1))
        def _():
            for cp in weight_copies(t, p):
                cp.wait()
            if cast:
                for w_f, w_b in zip(wf_refs, wb_refs):
                    w_b[...] = w_f[p].astype(BF16)

    def row(ref, r):
        return ref.at[pl.ds(pl.multiple_of(r, _FOLD), _FOLD), :]

    def stages(t, p, x_next, x_mm, y_mm, y_scatter):
        mm_used = (t >= 1) & (_tbl(tbl_ref, h, 2, t - 1) == 1)
        src_ref, dst_ref, gate_ref = src_s.at[p, 0], dst_s.at[p, 0], gate_v.at[1 - p]

        def gather_group(j0):
            for j in range(j0, j0 + _GRP):
                row(xg_ref, j * _FOLD)[...] = row(hf_ref, src_ref[j])[...]

        def scatter_group(j0):
            dsts = [row(acc_ref, dst_ref[j0 + u]) for u in range(_GRP)]
            news = [dsts[u][...] + row(y_scatter, (j0 + u) * _FOLD)[...] for u in range(_GRP)]
            for u in range(_GRP):
                dsts[u][...] = news[u]

        def unfold_block(r0):
            for k in range(_FOLD):
                x_next[r0:r0 + _UNF, k * LANES:(k + 1) * LANES] = xg_ref[
                    pl.ds(r0 * _FOLD + k, _UNF, stride=_FOLD), :].astype(BF16)

        moves = []
        for j0 in range(0, _R, _GRP):
            moves += [functools.partial(scatter_group, j0), functools.partial(gather_group, j0)]
            if (j0 + _GRP) % _UNF == 0:
                moves.append(functools.partial(unfold_block, j0 + _GRP - _UNF))
        n_pieces = 3 * _KT
        done = [0]

        def some_moves():
            done[0] += 1
            for f in moves[(done[0] - 1) * len(moves) // n_pieces:done[0] * len(moves) // n_pieces]:
                f()

        diag = (lax.broadcasted_iota(jnp.int32, (_R, _R), 0) == lax.broadcasted_iota(jnp.int32, (_R, _R), 1))
        gates = gate_ref[...] * mm_used.astype(F32)
        gcol = jnp.sum(jnp.where(diag, gates, 0.0), axis=1, keepdims=True)
        gt = up = None
        for kt in range(_KT):
            rows_k = slice(kt * _MXU, (kt + 1) * _MXU)
            xk = x_mm[:, rows_k]
            d = jnp.dot(xk, wgb_ref[rows_k, :], preferred_element_type=F32)
            gt = d if gt is None else gt + d
            some_moves()
            d = jnp.dot(xk, wub_ref[rows_k, :], preferred_element_type=F32)
            up = d if up is None else up + d
            some_moves()
        act = (_silu(gt) * up * gcol).astype(BF16)
        for nt in range(D_MODEL // _MXU):
            y = jnp.dot(act, wdb_ref[:, nt * _MXU:(nt + 1) * _MXU], preferred_element_type=F32)
            some_moves()
            for k in range(_MXU // LANES):
                y_mm[pl.ds(nt * _MXU // LANES + k, _R, stride=_FOLD), :] = y[:, k * LANES:(k + 1) * LANES]

    def step(t, p):
        arrive_lists(t, p)
        prefetch(t + 1, 1 - p)
        arrive_weights(t, p, cast=True)
        if p == 0:
            stages(t, p, xa_ref, xb_ref, yb_ref, ya_ref)
        else:
            stages(t, p, xb_ref, xa_ref, ya_ref, yb_ref)

    def two_steps(i, carry):
        step(2 * i, 0)
        step(2 * i + 1, 1)
        return carry

    pairs = (total + 3) // 2
    prefetch(0, 0)
    lax.fori_loop(0, pairs, two_steps, 0)
    arrive_lists(2 * pairs, 0)
    arrive_weights(2 * pairs, 0, cast=False)

    fg = fg_ref[...]

    def emit(half, tile0, n_tiles, dst_hbm, dst_row0, mod_row):
        gate = mod_ref[mod_row:mod_row + 1, 5 * D_MODEL:6 * D_MODEL]

        def in_copy(i):
            tok0 = half * _NH + (tile0 + i) * _OUT_ROWS
            s = i % _OUT_BUFS
            return pltpu.make_async_copy(x_hbm.at[pl.ds(tok0, _OUT_ROWS)], xt_ref.at[s], in_sem.at[s])

        def out_copy(i):
            s = i % _OUT_BUFS
            return pltpu.make_async_copy(yt_ref.at[s], dst_hbm.at[pl.ds(dst_row0 + i * _OUT_ROWS, _OUT_ROWS)],
                                         out_sem.at[s])

        for i in range(min(_OUT_BUFS - 1, n_tiles)):
            in_copy(i).start()
        for i in range(n_tiles):
            if i + _OUT_BUFS - 1 < n_tiles:
                in_copy(i + _OUT_BUFS - 1).start()
            in_copy(i).wait()
            base = (tile0 + i) * _OUT_ROWS * _FOLD
            routed = jnp.concatenate(
                [acc_ref[pl.ds(base + k, _OUT_ROWS, stride=_FOLD), :] for k in range(_FOLD)], axis=1)
            x2 = xt_ref[i % _OUT_BUFS] + gate * routed
            y = x2 * lax.rsqrt(jnp.mean(x2 * x2, axis=-1, keepdims=True) + EPS) * fg
            if i >= _OUT_BUFS:
                out_copy(i - _OUT_BUFS).wait()
            yt_ref[i % _OUT_BUFS] = y
            out_copy(i).start()
        for i in range(max(n_tiles - _OUT_BUFS, 0), n_tiles):
            out_copy(i).wait()

    per_half = _NH // _OUT_ROWS
    for half in range(N_TOK // _NH):
        @pl.when(h == half)
        def _(half=half):
            t = 0
            while t < per_half:
                tok = half * _NH + t * _OUT_ROWS
                if tok < N_CTX:
                    end_tok, dst, row0, mrow = N_CTX, yp_hbm, tok, 0
                else:
                    b = (tok - N_CTX) // DEC_SEQ
                    end_tok, dst, row0, mrow = N_CTX + (b + 1) * DEC_SEQ, ys_hbm, tok - N_CTX, 1 + b
                n = min(per_half - t, (end_tok - tok) // _OUT_ROWS)
                emit(half, t, n, dst, row0, mrow)
                t += n


def _moe(tbl, src, dst, gl, hfold, wg, wu, wd, x1s, mod, fg):
    halves = N_TOK // _NH
    hbm = pl.BlockSpec(memory_space=pl.ANY)
    const = lambda shape: pl.BlockSpec(shape, lambda h, tbl_ref: (0,) * len(shape))
    stage = pltpu.VMEM((_R * _FOLD, LANES), F32)
    packed = pltpu.VMEM((_R, D_MODEL), BF16)
    tile = pltpu.VMEM((_OUT_BUFS, _OUT_ROWS, D_MODEL), F32)
    up_shape, down_shape = (D_MODEL, D_EXPERT), (D_EXPERT, D_MODEL)
    return pl.pallas_call(
        _moe_kernel,
        grid_spec=pltpu.PrefetchScalarGridSpec(
            num_scalar_prefetch=1,
            grid=(halves,),
            in_specs=[hbm, hbm, hbm,
                      pl.BlockSpec((_NH * _FOLD, LANES), lambda h, tbl_ref: (h, 0)),
                      hbm, hbm, hbm,
                      hbm, const((8, N_MOD * D_MODEL)), const((1, D_MODEL))],
            out_specs=[hbm, hbm],
            scratch_shapes=[pltpu.VMEM(((_NH + 1) * _FOLD, LANES), F32),
                            tuple(pltpu.VMEM((2,) + s, F32) for s in (up_shape, up_shape, down_shape)),
                            tuple(pltpu.VMEM(s, BF16) for s in (up_shape, up_shape, down_shape)),
                            stage, packed, packed, stage, stage,
                            pltpu.SMEM((2, 1, _R), jnp.int32), pltpu.SMEM((2, 1, _R), jnp.int32),
                            pltpu.VMEM((2, 1, _R), F32), tile, tile,
                            pltpu.SemaphoreType.DMA((2, 3)), pltpu.SemaphoreType.DMA((2, 3)),
                            pltpu.SemaphoreType.DMA((_OUT_BUFS,)), pltpu.SemaphoreType.DMA((_OUT_BUFS,))]),
        out_shape=[jax.ShapeDtypeStruct((N_CTX, D_MODEL), F32),
                   jax.ShapeDtypeStruct((N_LAT, D_MODEL), F32)],
        compiler_params=_params(("arbitrary",), vmem=MOE_VMEM_LIMIT),
        name="moe",
    )(tbl, src, dst, gl, hfold, wg, wu, wd, x1s, mod, fg)


def kernel(x_prompt, x_sample, cache_k, cache_v, state_h_fwd, state_h_bwd, c, c_ctx, norm1_g, norm2_g, final_g, w_ada, b_ada, w_in, rpb, conv_w, conv_b, rg_wa, rg_ba, rg_wx, rg_bx, rg_lam, w_pa, w_pb, w_out, w_router, b_router, w_gate_e, w_up_e, w_down_e, w_gate_s, w_up_s, w_down_s):
    l = 0
    xp = x_prompt.reshape(N_CTX, D_MODEL)
    xs = x_sample.reshape(N_LAT, D_MODEL)

    cvecs = jnp.concatenate([c_ctx[None, :], c], axis=0)[:, :, None]
    mod = _adaln(cvecs, w_ada[l], b_ada[l][None, :])
    mod3 = mod.reshape(8, 1, N_MOD * D_MODEL)

    q, kc, vc, kl, vl, xr, yg, ga, gb = _inproj(xp, xs, mod3, norm1_g[l][None, :], w_in[l])

    oac = _attn_ctx(q, kc, vc)
    ck = cache_k[:, l].reshape(DEC_BATCH, PAST_LEN, D_ATT)
    cv = cache_v[:, l].reshape(DEC_BATCH, PAST_LEN, D_ATT)
    oal = _attn_lat(q, kl, vl, ck, cv, _bias_tables(rpb[l]))

    w4 = jnp.concatenate([_block_diag(rg_wa[l, 0]), _block_diag(rg_wx[l, 0]),
                          _block_diag(rg_wa[l, 1]), _block_diag(rg_wx[l, 1])], axis=1).astype(BF16)
    b4 = jnp.concatenate([rg_ba[l, 0], rg_bx[l, 0], rg_ba[l, 1], rg_bx[l, 1]])[None, :]
    ob, hlf, hlb = _rglru(xr, yg, conv_w[l], conv_b[l][None, :], w4, b4, rg_lam[l],
                          state_h_fwd[:, l], state_h_bwd[:, l])

    wr_t = w_router[l].T
    wr_hi = wr_t.astype(BF16)
    wr_lo = (wr_t - wr_hi.astype(F32)).astype(BF16)
    x1s, hfold, gates_t = _merge(oac, oal, ob, ga, gb, xp, xs, mod3, norm2_g[l][None, :],
                                 w_pa[l].astype(BF16), w_pb[l].astype(BF16), w_out[l].astype(BF16),
                                 wr_hi, wr_lo, b_router[l][:, None],
                                 w_gate_s[l].astype(BF16), w_up_s[l].astype(BF16), w_down_s[l].astype(BF16))

    src, dst, gl, tbl = _dispatch(gates_t)
    yp, ys = _moe(tbl.reshape(-1), src.reshape(-1, 1, _R), dst.reshape(-1, 1, _R), gl.reshape(-1, 1, _R), hfold,
                  w_gate_e[l], w_up_e[l], w_down_e[l], x1s, mod, final_g[None, :])

    return (yp.reshape(BATCH, SEQ, D_MODEL),
            ys.reshape(DEC_BATCH, DEC_SEQ, D_MODEL),
            kc.reshape(BATCH, 1, SEQ, N_HEADS_A, HEAD_DIM_A),
            vc.reshape(BATCH, 1, SEQ, N_HEADS_A, HEAD_DIM_A),
            hlf.reshape(BATCH, 1, D_RNN),
            hlb.reshape(BATCH, 1, D_RNN))
```
